```python
import math
import jax, jax.numpy as jnp
from jax import lax
import numpy as np

D_MODEL = 2048
BATCH = 4
SEQ = 2048
DEPTH = 4

HEAD_DIM_A = 128
N_HEADS_A = D_MODEL // (2 * HEAD_DIM_A)
WIDTH_A = N_HEADS_A * HEAD_DIM_A
CONV_K = 5
CHUNK = 64
WIDTH_B = D_MODEL // 2
S5_GROUP_CH = 16
N_GROUPS_B = WIDTH_B // S5_GROUP_CH
S5_STATE = 64
DT_MIN = 0.001
DT_MAX = 0.1
RMS_EPS = 1e-6

COL_QKV = 3 * WIDTH_A
COL_ZA = WIDTH_A
COL_BETA = 2 * N_HEADS_A
COL_ALPHA = 2 * N_HEADS_A
COL_U = WIDTH_B
COL_ZB = WIDTH_B
COL_GATES = 2 * D_MODEL
PROJ_WIDTH = COL_QKV + COL_ZA + COL_BETA + COL_ALPHA + COL_U + COL_ZB + COL_GATES
SPLIT_POINTS = list(np.cumsum([COL_QKV, COL_ZA, COL_BETA, COL_ALPHA, COL_U, COL_ZB]).tolist())

kernel_name = "hybrid_gdn_s5_bidir_encoder"


def rmsnorm(x, g):
    xf = x.astype(jnp.float32)
    y = xf * lax.rsqrt(jnp.mean(xf * xf, axis=-1, keepdims=True) + RMS_EPS)
    return (y * g.astype(jnp.float32)).astype(x.dtype)


def l2norm(x):
    xf = x.astype(jnp.float32)
    return xf * lax.rsqrt(jnp.sum(xf * xf, axis=-1, keepdims=True) + RMS_EPS)


def centred_dwconv(x, w):
    pad = (CONV_K - 1) // 2
    L = x.shape[1]
    xp = jnp.pad(x, ((0, 0), (pad, pad), (0, 0)))
    return sum(xp[:, i:i + L] * w[i] for i in range(CONV_K))


def _to_chunks(t):
    b, l, h = t.shape[:3]
    t = t.reshape((b, l // CHUNK, CHUNK, h) + t.shape[3:])
    return jnp.moveaxis(jnp.moveaxis(t, 1, 0), 3, 2)


def gated_delta_chunked(q, k, v, g, beta):
    b_, l_, h_, dv = v.shape
    qc, kc, vc = _to_chunks(q), _to_chunks(k), _to_chunks(v)
    gc = jnp.cumsum(_to_chunks(g), axis=-1)
    bc = _to_chunks(beta)
    idx = jnp.arange(CHUNK)
    incl = idx[:, None] >= idx[None, :]
    strict = idx[:, None] > idx[None, :]
    decay = jnp.exp(jnp.where(incl, gc[..., :, None] - gc[..., None, :], -jnp.inf))
    kb = kc * bc[..., None]
    vb = vc * bc[..., None]
    lmat = jnp.where(strict, jnp.einsum('nbhck,nbhsk->nbhcs', kb, kc) * decay, 0.0)
    u = lax.linalg.triangular_solve(lmat, vb, left_side=True, lower=True, unit_diagonal=True)
    w = lax.linalg.triangular_solve(lmat, kb * jnp.exp(gc)[..., None], left_side=True, lower=True, unit_diagonal=True)
    qk = jnp.einsum('nbhck,nbhsk->nbhcs', qc, kc) * decay

    def step(S, xs):
        q_c, k_c, u_c, w_c, g_c, qk_c = xs
        v_new = u_c - jnp.einsum('bhck,bhkv->bhcv', w_c, S)
        o_c = (jnp.einsum('bhck,bhkv->bhcv', q_c * jnp.exp(g_c)[..., None], S)
               + jnp.einsum('bhcs,bhsv->bhcv', qk_c, v_new))
        g_last = g_c[..., -1:]
        S = (S * jnp.exp(g_last)[..., None]
             + jnp.einsum('bhck,bhcv->bhkv', k_c * jnp.exp(g_last - g_c)[..., None], v_new))
        return S, o_c

    s0 = jnp.zeros((b_, h_, q.shape[-1], dv), jnp.float32)
    _, o = lax.scan(step, s0, (qc, kc, u, w, gc, qk))
    o = jnp.moveaxis(jnp.moveaxis(o, 2, 3), 0, 1)
    return o.reshape(b_, l_, h_, dv)


def bidir_gated_delta(q, k, v, g, beta):
    flip = lambda t: jnp.flip(t, axis=1)
    fwd = gated_delta_chunked(q, k, v, g[:, :, 0], beta[:, :, 0])
    bwd = flip(gated_delta_chunked(flip(q), flip(k), flip(v), flip(g[:, :, 1]), flip(beta[:, :, 1])))
    return fwd + bwd


def _ssm_combine(left, right):
    a_i, b_i = left
    a_j, b_j = right
    return a_j * a_i, a_j * b_i + b_j


def s5_bidirectional(u, lam_re, lam_im, log_dt, b_re, b_im, c_re, c_im, d_skip):
    f32 = jnp.float32
    bsz, L, _ = u.shape
    ug = u.astype(f32).reshape(bsz, L, N_GROUPS_B, S5_GROUP_CH)
    ugc = ug.astype(jnp.complex64)
    lam = lax.complex(lam_re.astype(f32), lam_im.astype(f32))
    dt = jnp.exp(log_dt.astype(f32))[..., None]
    lam_bar = jnp.exp(lam * dt)
    b_bar = ((lam_bar - 1.0) / lam)[..., None] * lax.complex(b_re.astype(f32), b_im.astype(f32))
    c = lax.complex(c_re.astype(f32), c_im.astype(f32))

    def one_direction(d, reverse):
        bu = jnp.einsum('gpc,blgc->blgp', b_bar[d], ugc)
        a = jnp.broadcast_to(lam_bar[d], bu.shape)
        _, states = lax.associative_scan(_ssm_combine, (a, bu), axis=1, reverse=reverse)
        return jnp.einsum('gcp,blgp->blgc', c[d], states).real

    y = (one_direction(0, False) + one_direction(1, True)
         + ug * d_skip.astype(f32).reshape(N_GROUPS_B, S5_GROUP_CH))
    return y.reshape(bsz, L, WIDTH_B).astype(u.dtype)


def hybrid_layer(x, ln_g, w_in, conv_w, a_log, dt_bias, head_norm_g, lam_re, lam_im, log_dt,
                 b_re, b_im, c_re, c_im, d_skip, w_glu, b_glu, w_pa, w_pb, b_gate, w_out):
    bsz, L, _ = x.shape
    h = rmsnorm(x, ln_g)
    proj = h @ w_in
    qkv, z_a, beta_logit, alpha_logit, u, z_b, gate_logit = jnp.split(proj, SPLIT_POINTS, axis=-1)

    qkv = jax.nn.silu(centred_dwconv(qkv, conv_w))
    q, k, v = jnp.split(qkv, 3, axis=-1)
    q = l2norm(q.reshape(bsz, L, N_HEADS_A, HEAD_DIM_A)) * (HEAD_DIM_A ** -0.5)
    k = l2norm(k.reshape(bsz, L, N_HEADS_A, HEAD_DIM_A))
    v = v.reshape(bsz, L, N_HEADS_A, HEAD_DIM_A).astype(jnp.float32)
    beta = jax.nn.sigmoid(beta_logit.astype(jnp.float32).reshape(bsz, L, 2, N_HEADS_A))
    g = -jnp.exp(a_log.astype(jnp.float32)) * jax.nn.softplus(
        alpha_logit.astype(jnp.float32).reshape(bsz, L, 2, N_HEADS_A) + dt_bias.astype(jnp.float32))
    o_a = bidir_gated_delta(q, k, v, g, beta)
    o_a = rmsnorm(o_a, head_norm_g).reshape(bsz, L, WIDTH_A).astype(x.dtype)
    y_a = (o_a * jax.nn.silu(z_a)) @ w_pa

    y_s = jax.nn.gelu(s5_bidirectional(u, lam_re, lam_im, log_dt, b_re, b_im, c_re, c_im, d_skip))
    y_s = y_s * jax.nn.sigmoid(y_s @ w_glu + b_glu)
    y_b = (y_s * jax.nn.silu(z_b)) @ w_pb

    gate_a, gate_b = jnp.split(jax.nn.sigmoid(gate_logit + b_gate), 2, axis=-1)
    merged = gate_a * y_a + gate_b * y_b
    return x + merged @ w_out


def setup_inputs(seed: int = 0) -> dict:
    key = jax.random.key(seed)
    ks = jax.random.split(key, 24)
    f32 = jnp.float32
    nrm = lambda k, shape, scale: scale * jax.random.normal(k, shape, f32)
    x = jax.random.normal(ks[0], (BATCH, SEQ, D_MODEL), f32)
    ln_g = 1.0 + nrm(ks[1], (DEPTH, D_MODEL), 0.02)
    w_in = nrm(ks[2], (DEPTH, D_MODEL, PROJ_WIDTH), D_MODEL ** -0.5)
    conv_w = nrm(ks[3], (DEPTH, CONV_K, 3 * WIDTH_A), CONV_K ** -0.5)
    a_log = jnp.log(jax.random.uniform(ks[4], (DEPTH, 2, N_HEADS_A), f32, 1.0, 16.0))
    dt = jnp.exp(jax.random.uniform(ks[5], (DEPTH, 2, N_HEADS_A), f32, math.log(DT_MIN), math.log(DT_MAX)))
    dt_bias = dt + jnp.log(-jnp.expm1(-dt))
    head_norm_g = 1.0 + nrm(ks[6], (DEPTH, HEAD_DIM_A), 0.02)
    ssm_shape = (DEPTH, 2, N_GROUPS_B, S5_STATE)
    lam_re = -0.5 + nrm(ks[7], ssm_shape, 0.01)
    lam_im = jnp.pi * jnp.arange(S5_STATE, dtype=f32) + nrm(ks[8], ssm_shape, 0.01)
    log_dt = jax.random.uniform(ks[9], (DEPTH, 2, N_GROUPS_B), f32, math.log(DT_MIN), math.log(DT_MAX))
    b_shape = (DEPTH, 2, N_GROUPS_B, S5_STATE, S5_GROUP_CH)
    b_re = nrm(ks[10], b_shape, (2 * S5_GROUP_CH) ** -0.5)
    b_im = nrm(ks[11], b_shape, (2 * S5_GROUP_CH) ** -0.5)
    c_shape = (DEPTH, 2, N_GROUPS_B, S5_GROUP_CH, S5_STATE)
    c_re = nrm(ks[12], c_shape, (2 * S5_STATE) ** -0.5)
    c_im = nrm(ks[13], c_shape, (2 * S5_STATE) ** -0.5)
    d_skip = nrm(ks[14], (DEPTH, WIDTH_B), 1.0)
    w_glu = nrm(ks[15], (DEPTH, WIDTH_B, WIDTH_B), WIDTH_B ** -0.5)
    b_glu = nrm(ks[16], (DEPTH, WIDTH_B), 0.01)
    w_pa = nrm(ks[17], (DEPTH, WIDTH_A, D_MODEL), WIDTH_A ** -0.5)
    w_pb = nrm(ks[18], (DEPTH, WIDTH_B, D_MODEL), WIDTH_B ** -0.5)
    b_gate = nrm(ks[19], (DEPTH, 2 * D_MODEL), 0.01)
    w_out = nrm(ks[20], (DEPTH, D_MODEL, D_MODEL), D_MODEL ** -0.5)
    final_g = 1.0 + nrm(ks[21], (D_MODEL,), 0.02)
    return {"x": x, "ln_g": ln_g, "w_in": w_in, "conv_w": conv_w, "a_log": a_log,
            "dt_bias": dt_bias, "head_norm_g": head_norm_g, "lam_re": lam_re, "lam_im": lam_im,
            "log_dt": log_dt, "b_re": b_re, "b_im": b_im, "c_re": c_re, "c_im": c_im,
            "d_skip": d_skip, "w_glu": w_glu, "b_glu": b_glu, "w_pa": w_pa, "w_pb": w_pb,
            "b_gate": b_gate, "w_out": w_out, "final_g": final_g}


def reference(x, ln_g, w_in, conv_w, a_log, dt_bias, head_norm_g, lam_re, lam_im, log_dt,
              b_re, b_im, c_re, c_im, d_skip, w_glu, b_glu, w_pa, w_pb, b_gate, w_out, final_g):
    for l in range(DEPTH):
        x = hybrid_layer(x, ln_g[l], w_in[l], conv_w[l], a_log[l], dt_bias[l], head_norm_g[l],
                         lam_re[l], lam_im[l], log_dt[l], b_re[l], b_im[l], c_re[l], c_im[l],
                         d_skip[l], w_glu[l], b_glu[l], w_pa[l], w_pb[l], b_gate[l], w_out[l])
    return rmsnorm(x, final_g)
```

```python
import functools
import math

import jax
import jax.numpy as jnp
from jax import lax
from jax.experimental import pallas as pl
from jax.experimental.pallas import tpu as pltpu

F32 = jnp.float32
BF16 = jnp.bfloat16

LANE = 128
SUBLANE = 8
HEAD_DIM = 128
CHUNK = 64
CONV_K = 5
CONV_PAD = (CONV_K - 1) // 2
S5_GROUP_CH = 16
S5_STATE = 64
S5_GROUPS_PER_BLOCK = LANE // S5_GROUP_CH
S5_BLOCK_STATE = S5_GROUPS_PER_BLOCK * S5_STATE
RMS_EPS = 1e-6
VMEM_LIMIT_BYTES = 56 * 1024 * 1024


def _compiler_params(semantics):
    return pltpu.CompilerParams(dimension_semantics=semantics, vmem_limit_bytes=VMEM_LIMIT_BYTES)


def _silu(x):
    return x * jax.nn.sigmoid(x)


def _largest_tile(n, cap, unit):
    best = unit
    t = unit
    while t <= min(n, cap):
        if n % t == 0:
            best = t
        t += unit
    return best


def _inproj_kernel(x_ref, g_ref, w_ref, o_ref, h_ref):
    @pl.when(pl.program_id(1) == 0)
    def _():
        x = x_ref[...]
        ms = jnp.mean(x * x, axis=-1, keepdims=True)
        h_ref[...] = (x * lax.rsqrt(ms + RMS_EPS) * g_ref[...]).astype(BF16)

    o_ref[...] = jnp.dot(h_ref[...], w_ref[...], preferred_element_type=F32)


def _inproj(x2, g_row, w_bf16):
    t, d = x2.shape
    n = w_bf16.shape[1]
    tm = _largest_tile(t, 1024, SUBLANE)
    tn = _largest_tile(n, 1280, LANE)
    return pl.pallas_call(
        _inproj_kernel,
        grid=(t // tm, n // tn),
        in_specs=[
            pl.BlockSpec((tm, d), lambda i, j: (i, 0)),
            pl.BlockSpec((1, d), lambda i, j: (0, 0)),
            pl.BlockSpec((d, tn), lambda i, j: (0, j)),
        ],
        out_specs=pl.BlockSpec((tm, tn), lambda i, j: (i, j)),
        out_shape=jax.ShapeDtypeStruct((t, n), F32),
        scratch_shapes=[pltpu.VMEM((tm, d), BF16)],
        compiler_params=_compiler_params(("parallel", "arbitrary")),
        name="inproj",
    )(x2, g_row, w_bf16)


def _gates_kernel(ba_ref, alog_ref, dtb_ref, o_ref, *, n_heads):
    x = ba_ref[...]
    tm = x.shape[0]
    lane = lax.broadcasted_iota(jnp.int32, x.shape, 1)
    beta = jax.nn.sigmoid(x)
    z = x + dtb_ref[...]
    softplus = jnp.maximum(z, 0.0) + jnp.log1p(jnp.exp(-jnp.abs(z)))
    g = -jnp.exp(alog_ref[...]) * softplus
    r = lax.broadcasted_iota(jnp.int32, (tm, tm), 0)
    c = lax.broadcasted_iota(jnp.int32, (tm, tm), 1)
    shift = int(math.log2(CHUNK))
    same_chunk = (r >> shift) == (c >> shift)
    m_prefix = jnp.where(same_chunk & (c <= r), 1.0, 0.0).astype(F32)
    m_suffix = jnp.where(same_chunk & (c >= r), 1.0, 0.0).astype(F32)
    g_prefix = jnp.dot(m_prefix, g, precision=lax.Precision.HIGHEST, preferred_element_type=F32)
    g_suffix = jnp.dot(m_suffix, g, precision=lax.Precision.HIGHEST, preferred_element_type=F32)
    gc = jnp.where(lane < 3 * n_heads, g_prefix, g_suffix)
    o_ref[...] = jnp.where(lane < 2 * n_heads, beta, gc)


def _gates(proj, ba_block, alog_row, dtb_row, n_heads):
    t = proj.shape[0]
    tm = _largest_tile(t, 512, CHUNK)
    return pl.pallas_call(
        functools.partial(_gates_kernel, n_heads=n_heads),
        grid=(t // tm,),
        in_specs=[
            pl.BlockSpec((tm, LANE), lambda i: (i, ba_block)),
            pl.BlockSpec((1, LANE), lambda i: (0, 0)),
            pl.BlockSpec((1, LANE), lambda i: (0, 0)),
        ],
        out_specs=pl.BlockSpec((tm, LANE), lambda i: (i, 0)),
        out_shape=jax.ShapeDtypeStruct((t, LANE), F32),
        compiler_params=_compiler_params(("parallel",)),
        name="gates",
    )(proj, alog_row, dtb_row)


def _dot_nt(a, b):
    return lax.dot_general(a, b, (((1,), (1,)), ((), ())), preferred_element_type=F32)


def _dot_bf16(a, b):
    return jnp.dot(a.astype(BF16), b.astype(BF16), preferred_element_type=F32)


def _unit_triangular_inverse(a):
    n = a.shape[0]
    r = lax.broadcasted_iota(jnp.int32, (n, n), 0)
    c = lax.broadcasted_iota(jnp.int32, (n, n), 1)
    eye = jnp.where(r == c, 1.0, 0.0).astype(F32)
    m = -a
    p = eye + m
    for _ in range(int(math.log2(n)) - 1):
        m = _dot_bf16(m, m)
        p = p + _dot_bf16(p, m)
    return p


def _delta_kernel(q_ref, k_ref, v_ref, wq_ref, wk_ref, wv_ref, cols_ref, rows_ref, hg_ref, o_ref,
                  pad_scr, q_scr, k_scr, v_scr, u_scr, wq_scr, kdt_scr, qk_scr, o_scr):
    seq = q_ref.shape[0]
    n_chunks = seq // CHUNK
    halo = SUBLANE

    pad_scr[0:halo, :] = jnp.zeros((halo, HEAD_DIM), F32)
    pad_scr[seq + halo:seq + 2 * halo, :] = jnp.zeros((halo, HEAD_DIM), F32)
    conv_tile = _largest_tile(seq, 256, SUBLANE)
    for src_ref, w_ref, dst, scale in ((q_ref, wq_ref, q_scr, HEAD_DIM ** -0.5),
                                       (k_ref, wk_ref, k_scr, None),
                                       (v_ref, wv_ref, v_scr, None)):
        pad_scr[halo:seq + halo, :] = src_ref[...]
        w = w_ref[...]
        for t0 in range(0, seq, conv_tile):
            acc = None
            for i in range(CONV_K):
                term = pad_scr[t0 + halo + i - CONV_PAD:t0 + halo + i - CONV_PAD + conv_tile, :] * w[i:i + 1, :]
                acc = term if acc is None else acc + term
            y = _silu(acc)
            if dst is not v_scr:
                y = y * lax.rsqrt(jnp.sum(y * y, axis=-1, keepdims=True) + RMS_EPS)
            if scale is not None:
                y = y * scale
            dst[t0:t0 + conv_tile, :] = y

    ri = lax.broadcasted_iota(jnp.int32, (CHUNK, CHUNK), 0)
    ci = lax.broadcasted_iota(jnp.int32, (CHUNK, CHUNK), 1)
    incl = (ri >= ci, ri <= ci)
    strict = (ri > ci, ri < ci)

    def precompute(n, carry):
        r0 = pl.multiple_of(n * CHUNK, CHUNK)
        q = q_scr[pl.ds(r0, CHUNK), :]
        k = k_scr[pl.ds(r0, CHUNK), :]
        v = v_scr[pl.ds(r0, CHUNK), :]
        cols = cols_ref[pl.ds(r0, CHUNK), :]
        rows = rows_ref[n]
        kb16 = k.astype(BF16)
        kk = _dot_nt(kb16, kb16)
        qk = _dot_nt(q.astype(BF16), kb16)
        for d in range(2):
            gc_col = cols[:, d:d + 1]
            beta_col = cols[:, 2 + d:3 + d]
            gc_row = rows[d:d + 1, :]
            decay = jnp.where(incl[d], jnp.exp(gc_col - gc_row), 0.0)
            lmat = jnp.where(strict[d], beta_col * kk * decay, 0.0)
            t_inv = _unit_triangular_inverse(lmat)
            eg = jnp.exp(gc_col)
            rhs = jnp.concatenate([v * beta_col, k * (beta_col * eg)], axis=1)
            uw = _dot_bf16(t_inv, rhs)
            u_scr[d, n] = uw[:, :HEAD_DIM]
            wq_scr[d, n, 0:CHUNK, :] = uw[:, HEAD_DIM:].astype(BF16)
            wq_scr[d, n, CHUNK:2 * CHUNK, :] = (q * eg).astype(BF16)
            g_last = gc_row[:, CHUNK - 1:CHUNK] if d == 0 else gc_row[:, 0:1]
            kdec = k * jnp.exp(g_last - gc_col)
            kdt_scr[d, n] = kdec.T.astype(BF16)
            qk_scr[d, n] = (qk * decay).astype(BF16)
        return carry

    lax.fori_loop(0, n_chunks, precompute, 0)

    def chunk_step(d, n, state):
        ws = jnp.dot(wq_scr[d, n], state.astype(BF16), preferred_element_type=F32)
        v_new = u_scr[d, n] - ws[:CHUNK]
        v_new16 = v_new.astype(BF16)
        o = ws[CHUNK:] + jnp.dot(qk_scr[d, n], v_new16, preferred_element_type=F32)
        o_scr[d, pl.ds(pl.multiple_of(n * CHUNK, CHUNK), CHUNK), :] = o
        rows = rows_ref[n]
        g_last = rows[0:1, CHUNK - 1:CHUNK] if d == 0 else rows[1:2, 0:1]
        return state * jnp.exp(g_last) + jnp.dot(kdt_scr[d, n], v_new16, preferred_element_type=F32)

    def recurrence(i, states):
        s_f, s_b = states
        return chunk_step(0, i, s_f), chunk_step(1, n_chunks - 1 - i, s_b)

    zero_state = jnp.zeros((HEAD_DIM, HEAD_DIM), F32)
    lax.fori_loop(0, n_chunks, recurrence, (zero_state, zero_state))

    for t0 in range(0, seq, conv_tile):
        o = o_scr[0, t0:t0 + conv_tile, :] + o_scr[1, t0:t0 + conv_tile, :]
        ms = jnp.mean(o * o, axis=-1, keepdims=True)
        o_ref[t0:t0 + conv_tile, :] = o * lax.rsqrt(ms + RMS_EPS) * hg_ref[...]


def _delta(proj3, conv_w, cols, rows, hg_row, n_heads):
    bsz, seq, _ = proj3.shape
    n_chunks = seq // CHUNK
    tok = lambda off: pl.BlockSpec((None, seq, HEAD_DIM), lambda b, h: (b, 0, off + h))
    cw = lambda off: pl.BlockSpec((CONV_K, HEAD_DIM), lambda b, h: (0, off + h))
    return pl.pallas_call(
        _delta_kernel,
        grid=(bsz, n_heads),
        in_specs=[
            tok(0), tok(n_heads), tok(2 * n_heads),
            cw(0), cw(n_heads), cw(2 * n_heads),
            pl.BlockSpec((None, None, seq, 4), lambda b, h: (b, h, 0, 0)),
            pl.BlockSpec((None, None, n_chunks, 2, CHUNK), lambda b, h: (b, h, 0, 0, 0)),
            pl.BlockSpec((1, HEAD_DIM), lambda b, h: (0, 0)),
        ],
        out_specs=pl.BlockSpec((None, seq, HEAD_DIM), lambda b, h: (b, 0, h)),
        out_shape=jax.ShapeDtypeStruct((bsz, seq, n_heads * HEAD_DIM), F32),
        scratch_shapes=[
            pltpu.VMEM((seq + 2 * SUBLANE, HEAD_DIM), F32),
            pltpu.VMEM((seq, HEAD_DIM), F32),
            pltpu.VMEM((seq, HEAD_DIM), F32),
            pltpu.VMEM((seq, HEAD_DIM), F32),
            pltpu.VMEM((2, n_chunks, CHUNK, HEAD_DIM), F32),
            pltpu.VMEM((2, n_chunks, 2 * CHUNK, HEAD_DIM), BF16),
            pltpu.VMEM((2, n_chunks, HEAD_DIM, CHUNK), BF16),
            pltpu.VMEM((2, n_chunks, CHUNK, CHUNK), BF16),
            pltpu.VMEM((2, seq, HEAD_DIM), F32),
        ],
        compiler_params=_compiler_params(("parallel", "parallel")),
        name="delta",
    )(proj3, proj3, proj3, conv_w, conv_w, conv_w, cols, rows, hg_row)


def _s5_kernel(u_ref, bblk_ref, cblk_ref, lam_ref, dsk_ref, y_ref, uperm, bu, yacc):
    seq = u_ref.shape[0]
    seg_len = seq // SUBLANE
    nst = S5_BLOCK_STATE

    def gather_rows(i, carry):
        uperm[pl.ds(pl.multiple_of(i * SUBLANE, SUBLANE), SUBLANE), :] = u_ref[pl.ds(i, SUBLANE, stride=seg_len), :]
        return carry

    lax.fori_loop(0, seg_len, gather_rows, 0)
    u16 = uperm[...].astype(BF16)
    yacc[...] = uperm[...] * dsk_ref[...]

    for d in range(2):
        bu[...] = jnp.dot(u16, bblk_ref[d], preferred_element_type=F32)
        lam = lam_ref[d]
        a_re1, a_im1 = lam[:, :nst], lam[:, nst:]
        a_re = jnp.broadcast_to(a_re1, (SUBLANE, nst))
        a_im = jnp.broadcast_to(a_im1, (SUBLANE, nst))

        def row_of(i, d=d):
            return i if d == 0 else seg_len - 1 - i

        def advance(i, x_re, x_im):
            r0 = pl.multiple_of(row_of(i) * SUBLANE, SUBLANE)
            b_re = bu[pl.ds(r0, SUBLANE), 0:nst]
            b_im = bu[pl.ds(r0, SUBLANE), nst:2 * nst]
            return a_re * x_re - a_im * x_im + b_re, a_re * x_im + a_im * x_re + b_im, r0

        def pass1(i, x):
            n_re, n_im, _ = advance(i, *x)
            return n_re, n_im

        zeros = jnp.zeros((SUBLANE, nst), F32)
        loc_re, loc_im = lax.fori_loop(0, seg_len, pass1, (zeros, zeros))

        p_re, p_im = a_re1, a_im1
        for _ in range(int(math.log2(seg_len))):
            p_re, p_im = p_re * p_re - p_im * p_im, 2.0 * p_re * p_im

        order = range(SUBLANE) if d == 0 else range(SUBLANE - 1, -1, -1)
        c_re = [None] * SUBLANE
        c_im = [None] * SUBLANE
        prev_re = jnp.zeros((1, nst), F32)
        prev_im = jnp.zeros((1, nst), F32)
        for j in order:
            c_re[j], c_im[j] = prev_re, prev_im
            end_re = loc_re[j:j + 1, :] + p_re * prev_re - p_im * prev_im
            end_im = loc_im[j:j + 1, :] + p_re * prev_im + p_im * prev_re
            prev_re, prev_im = end_re, end_im
        carry_re = jnp.concatenate(c_re, axis=0)
        carry_im = jnp.concatenate(c_im, axis=0)

        def pass2(i, x):
            n_re, n_im, r0 = advance(i, *x)
            bu[pl.ds(r0, SUBLANE), 0:nst] = n_re
            bu[pl.ds(r0, SUBLANE), nst:2 * nst] = n_im
            return n_re, n_im

        lax.fori_loop(0, seg_len, pass2, (carry_re, carry_im))
        yacc[...] += jnp.dot(bu[...].astype(BF16), cblk_ref[d], preferred_element_type=F32)

    def scatter_rows(i, carry):
        y_ref[pl.ds(i, SUBLANE, stride=seg_len), :] = yacc[pl.ds(pl.multiple_of(i * SUBLANE, SUBLANE), SUBLANE), :]
        return carry

    lax.fori_loop(0, seg_len, scatter_rows, 0)


def _s5(proj3, u_block0, bblk, cblk, lamrow, dsk_row):
    bsz, seq, _ = proj3.shape
    n_blocks = bblk.shape[1]
    nst2 = 2 * S5_BLOCK_STATE
    return pl.pallas_call(
        _s5_kernel,
        grid=(bsz, n_blocks),
        in_specs=[
            pl.BlockSpec((None, seq, LANE), lambda b, g: (b, 0, u_block0 + g)),
            pl.BlockSpec((2, None, LANE, nst2), lambda b, g: (0, g, 0, 0)),
            pl.BlockSpec((2, None, nst2, LANE), lambda b, g: (0, g, 0, 0)),
            pl.BlockSpec((2, None, 1, nst2), lambda b, g: (0, g, 0, 0)),
            pl.BlockSpec((1, LANE), lambda b, g: (0, g)),
        ],
        out_specs=pl.BlockSpec((None, seq, LANE), lambda b, g: (b, 0, g)),
        out_shape=jax.ShapeDtypeStruct((bsz, seq, n_blocks * LANE), F32),
        scratch_shapes=[
            pltpu.VMEM((seq, LANE), F32),
            pltpu.VMEM((seq, nst2), F32),
            pltpu.VMEM((seq, LANE), F32),
        ],
        compiler_params=_compiler_params(("parallel", "parallel")),
        name="s5",
    )(proj3, bblk, cblk, lamrow, dsk_row)


def _s5_params(lam_re, lam_im, log_dt, b_re, b_im, c_re, c_im):
    n_groups = lam_re.shape[1]
    nb = n_groups // S5_GROUPS_PER_BLOCK
    gpb = S5_GROUPS_PER_BLOCK
    lam = lax.complex(lam_re.astype(F32), lam_im.astype(F32))
    dt = jnp.exp(log_dt.astype(F32))[..., None]
    lam_bar = jnp.exp(lam * dt)
    b_bar = ((lam_bar - 1.0) / lam)[..., None] * lax.complex(b_re.astype(F32), b_im.astype(F32))
    c = lax.complex(c_re.astype(F32), c_im.astype(F32))
    eye = jnp.eye(gpb, dtype=F32)

    def b_blocks(part):
        p5 = part.reshape(2, nb, gpb, S5_STATE, S5_GROUP_CH)
        return jnp.einsum('dbgpc,gh->dbgchp', p5, eye).reshape(2, nb, LANE, S5_BLOCK_STATE)

    def c_blocks(part):
        p5 = part.reshape(2, nb, gpb, S5_GROUP_CH, S5_STATE)
        return jnp.einsum('dbgcp,gh->dbgphc', p5, eye).reshape(2, nb, S5_BLOCK_STATE, LANE)

    bblk = jnp.concatenate([b_blocks(b_bar.real), b_blocks(b_bar.imag)], axis=-1).astype(BF16)
    cblk = jnp.concatenate([c_blocks(c.real), c_blocks(-c.imag)], axis=-2).astype(BF16)
    lamrow = jnp.concatenate([lam_bar.real.reshape(2, nb, 1, S5_BLOCK_STATE),
                              lam_bar.imag.reshape(2, nb, 1, S5_BLOCK_STATE)], axis=-1)
    return bblk, cblk, lamrow


def _out_kernel(oa_ref, za_ref, ys_ref, zb_ref, ga_ref, gb_ref, x_ref,
                wpa_ref, wglu_ref, bglu_ref, wpb_ref, bga_ref, bgb_ref, wout_ref, fg_ref,
                o_ref, *, final_norm):
    a = (oa_ref[...] * _silu(za_ref[...])).astype(BF16)
    y_a = jnp.dot(a, wpa_ref[...], preferred_element_type=F32)
    y_s = jax.nn.gelu(ys_ref[...])
    glu = jnp.dot(y_s.astype(BF16), wglu_ref[...], preferred_element_type=F32) + bglu_ref[...]
    y_s = y_s * jax.nn.sigmoid(glu)
    y_b = jnp.dot((y_s * _silu(zb_ref[...])).astype(BF16), wpb_ref[...], preferred_element_type=F32)
    merged = (jax.nn.sigmoid(ga_ref[...] + bga_ref[...]) * y_a
              + jax.nn.sigmoid(gb_ref[...] + bgb_ref[...]) * y_b)
    out = x_ref[...] + jnp.dot(merged.astype(BF16), wout_ref[...], preferred_element_type=F32)
    if final_norm:
        ms = jnp.mean(out * out, axis=-1, keepdims=True)
        out = out * lax.rsqrt(ms + RMS_EPS) * fg_ref[...]
    o_ref[...] = out


def _out_stage(o_a, proj, y_s5, x2, w_pa, w_glu, b_glu, w_pb, b_gate, w_out, final_g, *,
               za_block, zb_block, gate_block0, final_norm):
    t, d = x2.shape
    wa = o_a.shape[1]
    wb = y_s5.shape[1]
    tm = _largest_tile(t, 256, SUBLANE)
    row = lambda width, blk: pl.BlockSpec((tm, width), lambda i: (i, blk))
    const = lambda shape, blk=0: pl.BlockSpec(shape, lambda i: (0, blk), pipeline_mode=pl.Buffered(1))
    return pl.pallas_call(
        functools.partial(_out_kernel, final_norm=final_norm),
        grid=(t // tm,),
        in_specs=[
            row(wa, 0), row(wa, za_block), row(wb, 0), row(wb, zb_block),
            row(d, gate_block0), row(d, gate_block0 + 1), row(d, 0),
            const((wa, d)), const((wb, wb)), const((1, wb)), const((wb, d)),
            const((1, d), 0), const((1, d), 1), const((d, d)), const((1, d)),
        ],
        out_specs=pl.BlockSpec((tm, d), lambda i: (i, 0)),
        out_shape=jax.ShapeDtypeStruct((t, d), F32),
        compiler_params=_compiler_params(("parallel",)),
        name="out_stage",
    )(o_a, proj, y_s5, proj, proj, proj, x2, w_pa, w_glu, b_glu, w_pb, b_gate, b_gate, w_out, final_g)


def _layer(x2, bsz, seq, ln_g, w_in, conv_w, a_log, dt_bias, head_norm_g, lam_re, lam_im, log_dt,
           b_re, b_im, c_re, c_im, d_skip, w_glu, b_glu, w_pa, w_pb, b_gate, w_out, final_g, final_norm):
    t, d = x2.shape
    n_heads = a_log.shape[1]
    wa = n_heads * HEAD_DIM
    wb = d_skip.shape[0]
    assert wa == wb and seq % (CHUNK * SUBLANE) == 0 and 4 * n_heads <= LANE

    c_qkv, c_za, c_beta, c_alpha, c_u, c_zb = 3 * wa, wa, 2 * n_heads, 2 * n_heads, wb, wb
    o_za = c_qkv
    o_beta = o_za + c_za
    o_alpha = o_beta + c_beta
    o_u = o_alpha + c_alpha
    o_zb = o_u + c_u
    o_gate = o_zb + c_zb
    w_cols = jnp.concatenate(
        [w_in[:, :o_beta], w_in[:, o_u:], w_in[:, o_beta:o_u],
         jnp.zeros((d, LANE - c_beta - c_alpha), w_in.dtype)], axis=1).astype(BF16)
    n_proj = w_cols.shape[1]
    ba_block = (n_proj - LANE) // LANE

    proj = _inproj(x2, ln_g.reshape(1, d), w_cols)
    proj3 = proj.reshape(bsz, seq, n_proj)

    pad_row = lambda p: jnp.pad(p.astype(F32).reshape(1, -1), ((0, 0), (c_beta, LANE - c_beta - c_alpha)))
    gates = _gates(proj, ba_block, pad_row(a_log), pad_row(dt_bias), n_heads)
    beta = gates[:, :c_beta].reshape(bsz, seq, 2, n_heads)
    gc = gates[:, c_beta:c_beta + c_alpha].reshape(bsz, seq, 2, n_heads)
    cols = jnp.stack([gc[:, :, 0], gc[:, :, 1], beta[:, :, 0], beta[:, :, 1]], axis=-1)
    cols = jnp.transpose(cols, (0, 2, 1, 3))
    rows = jnp.transpose(gc.reshape(bsz, seq // CHUNK, CHUNK, 2, n_heads), (0, 4, 1, 3, 2))

    o_a = _delta(proj3, conv_w, cols, rows, head_norm_g.reshape(1, HEAD_DIM), n_heads)
    o_a = o_a.reshape(t, wa)

    bblk, cblk, lamrow = _s5_params(lam_re, lam_im, log_dt, b_re, b_im, c_re, c_im)
    y_s5 = _s5(proj3, (c_qkv + c_za) // LANE, bblk, cblk, lamrow, d_skip.reshape(1, wb)).reshape(t, wb)

    return _out_stage(
        o_a, proj, y_s5, x2, w_pa.astype(BF16), w_glu.astype(BF16), b_glu.reshape(1, wb),
        w_pb.astype(BF16), b_gate.reshape(1, 2 * d), w_out.astype(BF16), final_g.reshape(1, d),
        za_block=(c_qkv) // wa, zb_block=(c_qkv + c_za + c_u) // wb,
        gate_block0=(c_qkv + c_za + c_u + c_zb) // d, final_norm=final_norm)


def kernel(x, ln_g, w_in, conv_w, a_log, dt_bias, head_norm_g, lam_re, lam_im, log_dt, b_re, b_im, c_re, c_im, d_skip, w_glu, b_glu, w_pa, w_pb, b_gate, w_out, final_g):
    bsz, seq, d = x.shape
    depth = ln_g.shape[0]
    x2 = x.reshape(bsz * seq, d)
    for l in range(depth):
        x2 = _layer(x2, bsz, seq, ln_g[l], w_in[l], conv_w[l], a_log[l], dt_bias[l], head_norm_g[l],
                    lam_re[l], lam_im[l], log_dt[l], b_re[l], b_im[l], c_re[l], c_im[l], d_skip[l],
                    w_glu[l], b_glu[l], w_pa[l], w_pb[l], b_gate[l], w_out[l], final_g,
                    final_norm=(l == depth - 1))
    return x2.reshape(bsz, seq, d)
```

```python
import functools
import math

import jax
import jax.numpy as jnp
from jax import lax
from jax.experimental import pallas as pl
from jax.experimental.pallas import tpu as pltpu

F32 = jnp.float32
BF16 = jnp.bfloat16

LANE = 128
SUBLANE = 8
HEAD_DIM = 128
CHUNK = 64
CONV_K = 5
CONV_PAD = (CONV_K - 1) // 2
DELTA_HEADS_PER_PROGRAM = 2
LEVEL_GROUP_CHUNKS = 8
S5_GROUP_CH = 16
S5_STATE = 64
S5_GROUPS_PER_BLOCK = LANE // S5_GROUP_CH
S5_BLOCK_STATE = S5_GROUPS_PER_BLOCK * S5_STATE
RMS_EPS = 1e-6
VMEM_LIMIT_BYTES = 56 * 1024 * 1024


def _compiler_params(semantics):
    return pltpu.CompilerParams(dimension_semantics=semantics, vmem_limit_bytes=VMEM_LIMIT_BYTES)


def _silu(x):
    return x * jax.nn.sigmoid(x)


def _largest_tile(n, cap, unit):
    best = unit
    t = unit
    while t <= min(n, cap):
        if n % t == 0:
            best = t
        t += unit
    return best


def _inproj_kernel(x_ref, g_ref, w_ref, o_ref, h_ref):
    @pl.when(pl.program_id(1) == 0)
    def _():
        x = x_ref[...]
        ms = jnp.mean(x * x, axis=-1, keepdims=True)
        h_ref[...] = (x * lax.rsqrt(ms + RMS_EPS) * g_ref[...]).astype(BF16)

    o_ref[...] = jnp.dot(h_ref[...], w_ref[...], preferred_element_type=F32)


def _inproj(x2, g_row, w_bf16):
    t, d = x2.shape
    n = w_bf16.shape[1]
    tm = _largest_tile(t, 1024, SUBLANE)
    tn = _largest_tile(n, 1280, LANE)
    return pl.pallas_call(
        _inproj_kernel,
        grid=(t // tm, n // tn),
        in_specs=[
            pl.BlockSpec((tm, d), lambda i, j: (i, 0)),
            pl.BlockSpec((1, d), lambda i, j: (0, 0)),
            pl.BlockSpec((d, tn), lambda i, j: (0, j)),
        ],
        out_specs=pl.BlockSpec((tm, tn), lambda i, j: (i, j)),
        out_shape=jax.ShapeDtypeStruct((t, n), F32),
        scratch_shapes=[pltpu.VMEM((tm, d), BF16)],
        compiler_params=_compiler_params(("parallel", "arbitrary")),
        name="inproj",
    )(x2, g_row, w_bf16)


def _gates_kernel(ba_ref, alog_ref, dtb_ref, o_ref, *, n_heads):
    x = ba_ref[...]
    tm = x.shape[0]
    lane = lax.broadcasted_iota(jnp.int32, x.shape, 1)
    beta = jax.nn.sigmoid(x)
    z = x + dtb_ref[...]
    softplus = jnp.maximum(z, 0.0) + jnp.log1p(jnp.exp(-jnp.abs(z)))
    g = -jnp.exp(alog_ref[...]) * softplus
    r = lax.broadcasted_iota(jnp.int32, (tm, tm), 0)
    c = lax.broadcasted_iota(jnp.int32, (tm, tm), 1)
    shift = int(math.log2(CHUNK))
    same_chunk = (r >> shift) == (c >> shift)
    m_prefix = jnp.where(same_chunk & (c <= r), 1.0, 0.0).astype(F32)
    m_suffix = jnp.where(same_chunk & (c >= r), 1.0, 0.0).astype(F32)
    g_prefix = jnp.dot(m_prefix, g, precision=lax.Precision.HIGHEST, preferred_element_type=F32)
    g_suffix = jnp.dot(m_suffix, g, precision=lax.Precision.HIGHEST, preferred_element_type=F32)
    gc = jnp.where(lane < 3 * n_heads, g_prefix, g_suffix)
    o_ref[...] = jnp.where(lane < 2 * n_heads, beta, gc)


def _gates(proj, ba_block, alog_row, dtb_row, n_heads):
    t = proj.shape[0]
    tm = _largest_tile(t, 512, CHUNK)
    return pl.pallas_call(
        functools.partial(_gates_kernel, n_heads=n_heads),
        grid=(t // tm,),
        in_specs=[
            pl.BlockSpec((tm, LANE), lambda i: (i, ba_block)),
            pl.BlockSpec((1, LANE), lambda i: (0, 0)),
            pl.BlockSpec((1, LANE), lambda i: (0, 0)),
        ],
        out_specs=pl.BlockSpec((tm, LANE), lambda i: (i, 0)),
        out_shape=jax.ShapeDtypeStruct((t, LANE), F32),
        compiler_params=_compiler_params(("parallel",)),
        name="gates",
    )(proj, alog_row, dtb_row)


def _dot_nt(a, b):
    return lax.dot_general(a, b, (((1,), (1,)), ((), ())), preferred_element_type=F32)


def _delta_kernel(q_ref, k_ref, v_ref, wq_ref, wk_ref, wv_ref, cols_ref, rows_ref, hg_ref, o_ref,
                  pad_scr, r_scr, m_scr, wq_scr, kdt_scr, qk_scr, ob_scr, s_scr):
    seq = q_ref.shape[0]
    hpb = q_ref.shape[1] // HEAD_DIM
    n_chunks = seq // CHUNK
    halo = SUBLANE
    c2 = 2 * CHUNK
    operands = ((q_ref, wq_ref), (k_ref, wk_ref), (v_ref, wv_ref))
    for j in range(len(operands)):
        pad_scr[j, 0:halo, :] = jnp.zeros((halo, HEAD_DIM), F32)
        pad_scr[j, seq + halo:seq + 2 * halo, :] = jnp.zeros((halo, HEAD_DIM), F32)

    def conv_silu(j, w, r0):
        acc = None
        for i in range(CONV_K):
            term = pad_scr[j, pl.ds(r0 + halo + i - CONV_PAD, CHUNK), :] * w[i:i + 1, :]
            acc = term if acc is None else acc + term
        return _silu(acc)

    def l2norm(y):
        return y * lax.rsqrt(jnp.sum(y * y, axis=-1, keepdims=True) + RMS_EPS)

    ri = lax.broadcasted_iota(jnp.int32, (c2, c2), 0)
    ci = lax.broadcasted_iota(jnp.int32, (c2, c2), 1)
    fwd_block = (ri < CHUNK) & (ci < CHUNK)
    bwd_block = (ri >= CHUNK) & (ci >= CHUNK)
    strict2 = (fwd_block & (ri > ci)) | (bwd_block & (ri < ci))
    rw = lax.broadcasted_iota(jnp.int32, (CHUNK, c2), 0)
    cw = lax.broadcasted_iota(jnp.int32, (CHUNK, c2), 1)
    fwd_lanes = cw < CHUNK
    incl_wide = (fwd_lanes & (rw >= cw)) | (~fwd_lanes & (rw <= cw - CHUNK))
    top_rows = ri < CHUNK

    for p in range(hpb):
        lanes = slice(p * HEAD_DIM, (p + 1) * HEAD_DIM)

        def fill(n, carry, lanes=lanes):
            r0 = pl.multiple_of(n * CHUNK, CHUNK)
            for j, (src_ref, _) in enumerate(operands):
                pad_scr[j, pl.ds(r0 + halo, CHUNK), :] = src_ref[pl.ds(r0, CHUNK), lanes]
            return carry

        lax.fori_loop(0, n_chunks, fill, 0, unroll=2)

        def setup(n, carry, p=p, lanes=lanes):
            r0 = pl.multiple_of(n * CHUNK, CHUNK)
            q = l2norm(conv_silu(0, wq_ref[:, lanes], r0)) * (HEAD_DIM ** -0.5)
            k = l2norm(conv_silu(1, wk_ref[:, lanes], r0))
            v = conv_silu(2, wv_ref[:, lanes], r0)
            k16 = k.astype(BF16)
            k2_16 = jnp.concatenate([k16, k16], axis=0)
            kk2 = _dot_nt(k2_16, k2_16)
            qk_wide = _dot_nt(q.astype(BF16), k2_16)
            cols = cols_ref[pl.ds(r0, CHUNK), :]
            across = lambda c: jnp.broadcast_to(cols[:, 4 * p + c:4 * p + c + 1], (CHUNK, c2))
            gc2 = jnp.concatenate([across(0), across(1)], axis=0)
            be2 = jnp.concatenate([across(2), across(3)], axis=0)
            gc_row2 = rows_ref[p, n]
            k2 = jnp.concatenate([k, k], axis=0)
            v2 = jnp.concatenate([v, v], axis=0)
            q2 = jnp.concatenate([q, q], axis=0)
            decay2 = jnp.where(strict2, jnp.exp(gc2 - gc_row2), 0.0)
            m_scr[p, n] = (-(be2 * kk2 * decay2)).astype(BF16)
            eg2 = jnp.exp(gc2)
            r_scr[p, n] = jnp.concatenate([v2 * be2, k2 * (be2 * eg2)], axis=1)
            qg2 = (q2 * eg2).astype(BF16)
            wq_scr[p, 0, n, CHUNK:c2, :] = qg2[:CHUNK]
            wq_scr[p, 1, n, CHUNK:c2, :] = qg2[CHUNK:]
            g_last2 = jnp.where(top_rows, gc_row2[:, CHUNK - 1:CHUNK], gc_row2[:, CHUNK:CHUNK + 1])
            kdec2 = k2 * jnp.exp(g_last2 - gc2)
            kdt_scr[p, n] = kdec2.T.astype(BF16)
            gc_wide = jnp.where(fwd_lanes, gc2[:CHUNK], gc2[CHUNK:])
            qk_scr[p, n] = (qk_wide * jnp.where(incl_wide, jnp.exp(gc_wide - gc_row2), 0.0)).astype(BF16)
            return carry

        lax.fori_loop(0, n_chunks, setup, 0, unroll=2)

    group = _largest_tile(n_chunks, LEVEL_GROUP_CHUNKS, 1)

    def level(_, carry):
        def per_group(it, carry):
            tiles = [(p, it * group + g) for g in range(group) for p in range(hpb)]
            outs = []
            for p, n in tiles:
                m = m_scr[p, n]
                rhs = jnp.concatenate([r_scr[p, n].astype(BF16), m], axis=1)
                outs.append(jnp.dot(m, rhs, preferred_element_type=F32))
            for (p, n), out in zip(tiles, outs):
                r_scr[p, n] += out[:, :2 * HEAD_DIM]
                m_scr[p, n] = out[:, 2 * HEAD_DIM:].astype(BF16)
            return carry

        return lax.fori_loop(0, n_chunks // group, per_group, carry)

    lax.fori_loop(0, int(math.log2(CHUNK)), level, 0)

    def finish(n, carry):
        for p in range(hpb):
            w2 = r_scr[p, n, :, HEAD_DIM:2 * HEAD_DIM].astype(BF16)
            wq_scr[p, 0, n, 0:CHUNK, :] = w2[:CHUNK]
            wq_scr[p, 1, n, 0:CHUNK, :] = w2[CHUNK:]
        return carry

    lax.fori_loop(0, n_chunks, finish, 0, unroll=4)

    s_scr[...] = jnp.zeros(s_scr.shape, F32)
    zero_half = jnp.zeros((CHUNK, HEAD_DIM), BF16)
    chains = [(p, d) for p in range(hpb) for d in range(2)]

    def recurrence(i, carry):
        chunk_of = (i, n_chunks - 1 - i)
        states = [s_scr[p, d] for p, d in chains]
        ws = [jnp.dot(wq_scr[p, d, chunk_of[d]], s.astype(BF16), preferred_element_type=F32)
              for (p, d), s in zip(chains, states)]
        new_states = []
        for (p, d), s, w in zip(chains, states, ws):
            n = chunk_of[d]
            r0 = pl.multiple_of(n * CHUNK, CHUNK)
            v_new = (r_scr[p, n, d * CHUNK:(d + 1) * CHUNK, 0:HEAD_DIM] - w[:CHUNK]).astype(BF16)
            v_pad = jnp.concatenate([v_new, zero_half] if d == 0 else [zero_half, v_new], axis=0)
            o = w[CHUNK:] + jnp.dot(qk_scr[p, n], v_pad, preferred_element_type=F32)
            if d == 0:
                o_ref[pl.ds(r0, CHUNK), p * HEAD_DIM:(p + 1) * HEAD_DIM] = o
            else:
                ob_scr[p, pl.ds(r0, CHUNK), :] = o
            gc_row2 = rows_ref[p, n]
            g_last = gc_row2[:, CHUNK - 1:CHUNK] if d == 0 else gc_row2[:, CHUNK:CHUNK + 1]
            new_states.append(s * jnp.exp(g_last) + jnp.dot(kdt_scr[p, n], v_pad, preferred_element_type=F32))
        for (p, d), s in zip(chains, new_states):
            s_scr[p, d] = s
        return carry

    lax.fori_loop(0, n_chunks, recurrence, 0)

    def head_norm(n, carry):
        r0 = pl.multiple_of(n * CHUNK, CHUNK)
        for p in range(hpb):
            lanes = slice(p * HEAD_DIM, (p + 1) * HEAD_DIM)
            o = o_ref[pl.ds(r0, CHUNK), lanes] + ob_scr[p, pl.ds(r0, CHUNK), :]
            ms = jnp.mean(o * o, axis=-1, keepdims=True)
            o_ref[pl.ds(r0, CHUNK), lanes] = o * lax.rsqrt(ms + RMS_EPS) * hg_ref[...]
        return carry

    lax.fori_loop(0, n_chunks, head_norm, 0, unroll=2)


def _delta(proj3, conv_w, cols, rows, hg_row, n_heads):
    bsz, seq, _ = proj3.shape
    n_chunks = seq // CHUNK
    hpb = DELTA_HEADS_PER_PROGRAM
    width = hpb * HEAD_DIM
    n_blocks = n_heads // hpb
    tok = lambda off: pl.BlockSpec((None, seq, width), lambda b, h: (b, 0, off + h))
    cw = lambda off: pl.BlockSpec((CONV_K, width), lambda b, h: (0, off + h))
    return pl.pallas_call(
        _delta_kernel,
        grid=(bsz, n_blocks),
        in_specs=[
            tok(0), tok(n_blocks), tok(2 * n_blocks),
            cw(0), cw(n_blocks), cw(2 * n_blocks),
            pl.BlockSpec((None, None, seq, 4 * hpb), lambda b, h: (b, h, 0, 0)),
            pl.BlockSpec((None, hpb, n_chunks, 1, 2 * CHUNK), lambda b, h: (b, h, 0, 0, 0)),
            pl.BlockSpec((1, HEAD_DIM), lambda b, h: (0, 0)),
        ],
        out_specs=pl.BlockSpec((None, seq, width), lambda b, h: (b, 0, h)),
        out_shape=jax.ShapeDtypeStruct((bsz, seq, n_heads * HEAD_DIM), F32),
        scratch_shapes=[
            pltpu.VMEM((3, seq + 2 * SUBLANE, HEAD_DIM), F32),
            pltpu.VMEM((hpb, n_chunks, 2 * CHUNK, 2 * HEAD_DIM), F32),
            pltpu.VMEM((hpb, n_chunks, 2 * CHUNK, 2 * CHUNK), BF16),
            pltpu.VMEM((hpb, 2, n_chunks, 2 * CHUNK, HEAD_DIM), BF16),
            pltpu.VMEM((hpb, n_chunks, HEAD_DIM, 2 * CHUNK), BF16),
            pltpu.VMEM((hpb, n_chunks, CHUNK, 2 * CHUNK), BF16),
            pltpu.VMEM((hpb, seq, HEAD_DIM), F32),
            pltpu.VMEM((hpb, 2, HEAD_DIM, HEAD_DIM), F32),
        ],
        compiler_params=_compiler_params(("parallel", "parallel")),
        name="delta",
    )(proj3, proj3, proj3, conv_w, conv_w, conv_w, cols, rows, hg_row)


def _s5_kernel(u_ref, bblk_ref, cblk_ref, lam_ref, dsk_ref, y_ref, uperm, bu, yacc):
    seq = u_ref.shape[0]
    seg_len = seq // SUBLANE
    nst = S5_BLOCK_STATE

    def gather_rows(i, carry):
        uperm[pl.ds(pl.multiple_of(i * SUBLANE, SUBLANE), SUBLANE), :] = u_ref[pl.ds(i, SUBLANE, stride=seg_len), :]
        return carry

    lax.fori_loop(0, seg_len, gather_rows, 0)
    u16 = uperm[...].astype(BF16)
    yacc[...] = uperm[...] * dsk_ref[...]

    for d in range(2):
        bu[...] = jnp.dot(u16, bblk_ref[d], preferred_element_type=F32)
        lam = lam_ref[d]
        a_re1, a_im1 = lam[:, :nst], lam[:, nst:]
        a_re = jnp.broadcast_to(a_re1, (SUBLANE, nst))
        a_im = jnp.broadcast_to(a_im1, (SUBLANE, nst))

        def row_of(i, d=d):
            return i if d == 0 else seg_len - 1 - i

        def advance(i, x_re, x_im):
            r0 = pl.multiple_of(row_of(i) * SUBLANE, SUBLANE)
            b_re = bu[pl.ds(r0, SUBLANE), 0:nst]
            b_im = bu[pl.ds(r0, SUBLANE), nst:2 * nst]
            return a_re * x_re - a_im * x_im + b_re, a_re * x_im + a_im * x_re + b_im, r0

        def pass1(i, x):
            n_re, n_im, _ = advance(i, *x)
            return n_re, n_im

        zeros = jnp.zeros((SUBLANE, nst), F32)
        loc_re, loc_im = lax.fori_loop(0, seg_len, pass1, (zeros, zeros))

        p_re, p_im = a_re1, a_im1
        for _ in range(int(math.log2(seg_len))):
            p_re, p_im = p_re * p_re - p_im * p_im, 2.0 * p_re * p_im

        order = range(SUBLANE) if d == 0 else range(SUBLANE - 1, -1, -1)
        c_re = [None] * SUBLANE
        c_im = [None] * SUBLANE
        prev_re = jnp.zeros((1, nst), F32)
        prev_im = jnp.zeros((1, nst), F32)
        for j in order:
            c_re[j], c_im[j] = prev_re, prev_im
            end_re = loc_re[j:j + 1, :] + p_re * prev_re - p_im * prev_im
            end_im = loc_im[j:j + 1, :] + p_re * prev_im + p_im * prev_re
            prev_re, prev_im = end_re, end_im
        carry_re = jnp.concatenate(c_re, axis=0)
        carry_im = jnp.concatenate(c_im, axis=0)

        def pass2(i, x):
            n_re, n_im, r0 = advance(i, *x)
            bu[pl.ds(r0, SUBLANE), 0:nst] = n_re
            bu[pl.ds(r0, SUBLANE), nst:2 * nst] = n_im
            return n_re, n_im

        lax.fori_loop(0, seg_len, pass2, (carry_re, carry_im))
        yacc[...] += jnp.dot(bu[...].astype(BF16), cblk_ref[d], preferred_element_type=F32)

    def scatter_rows(i, carry):
        y_ref[pl.ds(i, SUBLANE, stride=seg_len), :] = yacc[pl.ds(pl.multiple_of(i * SUBLANE, SUBLANE), SUBLANE), :]
        return carry

    lax.fori_loop(0, seg_len, scatter_rows, 0)


def _s5(proj3, u_block0, bblk, cblk, lamrow, dsk_row):
    bsz, seq, _ = proj3.shape
    n_blocks = bblk.shape[1]
    nst2 = 2 * S5_BLOCK_STATE
    return pl.pallas_call(
        _s5_kernel,
        grid=(bsz, n_blocks),
        in_specs=[
            pl.BlockSpec((None, seq, LANE), lambda b, g: (b, 0, u_block0 + g)),
            pl.BlockSpec((2, None, LANE, nst2), lambda b, g: (0, g, 0, 0)),
            pl.BlockSpec((2, None, nst2, LANE), lambda b, g: (0, g, 0, 0)),
            pl.BlockSpec((2, None, 1, nst2), lambda b, g: (0, g, 0, 0)),
            pl.BlockSpec((1, LANE), lambda b, g: (0, g)),
        ],
        out_specs=pl.BlockSpec((None, seq, LANE), lambda b, g: (b, 0, g)),
        out_shape=jax.ShapeDtypeStruct((bsz, seq, n_blocks * LANE), F32),
        scratch_shapes=[
            pltpu.VMEM((seq, LANE), F32),
            pltpu.VMEM((seq, nst2), F32),
            pltpu.VMEM((seq, LANE), F32),
        ],
        compiler_params=_compiler_params(("parallel", "parallel")),
        name="s5",
    )(proj3, bblk, cblk, lamrow, dsk_row)


def _s5_params(lam_re, lam_im, log_dt, b_re, b_im, c_re, c_im):
    n_groups = lam_re.shape[1]
    nb = n_groups // S5_GROUPS_PER_BLOCK
    gpb = S5_GROUPS_PER_BLOCK
    l_re, l_im = lam_re.astype(F32), lam_im.astype(F32)
    dt = jnp.exp(log_dt.astype(F32))[..., None]
    mag = jnp.exp(l_re * dt)
    bar_re, bar_im = mag * jnp.cos(l_im * dt), mag * jnp.sin(l_im * dt)
    n_re, n_im = bar_re - 1.0, bar_im
    den = l_re * l_re + l_im * l_im
    k_re = ((n_re * l_re + n_im * l_im) / den)[..., None]
    k_im = ((n_im * l_re - n_re * l_im) / den)[..., None]
    bb_re = k_re * b_re.astype(F32) - k_im * b_im.astype(F32)
    bb_im = k_re * b_im.astype(F32) + k_im * b_re.astype(F32)
    eye = jnp.eye(gpb, dtype=F32)

    def b_blocks(part):
        p5 = part.reshape(2, nb, gpb, S5_STATE, S5_GROUP_CH)
        return jnp.einsum('dbgpc,gh->dbgchp', p5, eye).reshape(2, nb, LANE, S5_BLOCK_STATE)

    def c_blocks(part):
        p5 = part.reshape(2, nb, gpb, S5_GROUP_CH, S5_STATE)
        return jnp.einsum('dbgcp,gh->dbgphc', p5, eye).reshape(2, nb, S5_BLOCK_STATE, LANE)

    bblk = jnp.concatenate([b_blocks(bb_re), b_blocks(bb_im)], axis=-1).astype(BF16)
    cblk = jnp.concatenate([c_blocks(c_re.astype(F32)), c_blocks(-c_im.astype(F32))], axis=-2).astype(BF16)
    lamrow = jnp.concatenate([bar_re.reshape(2, nb, 1, S5_BLOCK_STATE),
                              bar_im.reshape(2, nb, 1, S5_BLOCK_STATE)], axis=-1)
    return bblk, cblk, lamrow


def _out_kernel(oa_ref, za_ref, ys_ref, zb_ref, ga_ref, gb_ref, x_ref,
                wpa_ref, wglu_ref, bglu_ref, wpb_ref, bga_ref, bgb_ref, wout_ref, fg_ref,
                o_ref, *, final_norm):
    a = (oa_ref[...] * _silu(za_ref[...])).astype(BF16)
    y_a = jnp.dot(a, wpa_ref[...], preferred_element_type=F32)
    y_s = jax.nn.gelu(ys_ref[...])
    glu = jnp.dot(y_s.astype(BF16), wglu_ref[...], preferred_element_type=F32) + bglu_ref[...]
    y_s = y_s * jax.nn.sigmoid(glu)
    y_b = jnp.dot((y_s * _silu(zb_ref[...])).astype(BF16), wpb_ref[...], preferred_element_type=F32)
    merged = (jax.nn.sigmoid(ga_ref[...] + bga_ref[...]) * y_a
              + jax.nn.sigmoid(gb_ref[...] + bgb_ref[...]) * y_b)
    out = x_ref[...] + jnp.dot(merged.astype(BF16), wout_ref[...], preferred_element_type=F32)
    if final_norm:
        ms = jnp.mean(out * out, axis=-1, keepdims=True)
        out = out * lax.rsqrt(ms + RMS_EPS) * fg_ref[...]
    o_ref[...] = out


def _out_stage(o_a, proj, y_s5, x2, w_pa, w_glu, b_glu, w_pb, b_gate, w_out, final_g, *,
               za_block, zb_block, gate_block0, final_norm):
    t, d = x2.shape
    wa = o_a.shape[1]
    wb = y_s5.shape[1]
    tm = _largest_tile(t, 256, SUBLANE)
    row = lambda width, blk: pl.BlockSpec((tm, width), lambda i: (i, blk))
    const = lambda shape, blk=0: pl.BlockSpec(shape, lambda i: (0, blk), pipeline_mode=pl.Buffered(1))
    return pl.pallas_call(
        functools.partial(_out_kernel, final_norm=final_norm),
        grid=(t // tm,),
        in_specs=[
            row(wa, 0), row(wa, za_block), row(wb, 0), row(wb, zb_block),
            row(d, gate_block0), row(d, gate_block0 + 1), row(d, 0),
            const((wa, d)), const((wb, wb)), const((1, wb)), const((wb, d)),
            const((1, d), 0), const((1, d), 1), const((d, d)), const((1, d)),
        ],
        out_specs=pl.BlockSpec((tm, d), lambda i: (i, 0)),
        out_shape=jax.ShapeDtypeStruct((t, d), F32),
        compiler_params=_compiler_params(("parallel",)),
        name="out_stage",
    )(o_a, proj, y_s5, proj, proj, proj, x2, w_pa, w_glu, b_glu, w_pb, b_gate, b_gate, w_out, final_g)


def _layer(x2, bsz, seq, ln_g, w_in, conv_w, a_log, dt_bias, head_norm_g, lam_re, lam_im, log_dt,
           b_re, b_im, c_re, c_im, d_skip, w_glu, b_glu, w_pa, w_pb, b_gate, w_out, final_g, final_norm):
    t, d = x2.shape
    n_heads = a_log.shape[1]
    wa = n_heads * HEAD_DIM
    wb = d_skip.shape[0]
    assert wa == wb and seq % (CHUNK * SUBLANE) == 0 and 4 * n_heads <= LANE

    c_qkv, c_za, c_beta, c_alpha, c_u, c_zb = 3 * wa, wa, 2 * n_heads, 2 * n_heads, wb, wb
    o_za = c_qkv
    o_beta = o_za + c_za
    o_alpha = o_beta + c_beta
    o_u = o_alpha + c_alpha
    o_zb = o_u + c_u
    o_gate = o_zb + c_zb
    w_cols = jnp.concatenate(
        [w_in[:, :o_beta], w_in[:, o_u:], w_in[:, o_beta:o_u],
         jnp.zeros((d, LANE - c_beta - c_alpha), w_in.dtype)], axis=1).astype(BF16)
    n_proj = w_cols.shape[1]
    ba_block = (n_proj - LANE) // LANE

    proj = _inproj(x2, ln_g.reshape(1, d), w_cols)
    proj3 = proj.reshape(bsz, seq, n_proj)

    pad_row = lambda p: jnp.pad(p.astype(F32).reshape(1, -1), ((0, 0), (c_beta, LANE - c_beta - c_alpha)))
    gates = _gates(proj, ba_block, pad_row(a_log), pad_row(dt_bias), n_heads)
    beta = gates[:, :c_beta].reshape(bsz, seq, 2, n_heads)
    gc = gates[:, c_beta:c_beta + c_alpha].reshape(bsz, seq, 2, n_heads)
    hpb = DELTA_HEADS_PER_PROGRAM
    cols = jnp.stack([gc[:, :, 0], gc[:, :, 1], beta[:, :, 0], beta[:, :, 1]], axis=-1)
    cols = jnp.transpose(cols.reshape(bsz, seq, n_heads // hpb, 4 * hpb), (0, 2, 1, 3))
    rows = jnp.transpose(gc.reshape(bsz, seq // CHUNK, CHUNK, 2, n_heads), (0, 4, 1, 3, 2))
    rows = rows.reshape(bsz, n_heads, seq // CHUNK, 1, 2 * CHUNK)

    o_a = _delta(proj3, conv_w, cols, rows, head_norm_g.reshape(1, HEAD_DIM), n_heads)
    o_a = o_a.reshape(t, wa)

    bblk, cblk, lamrow = _s5_params(lam_re, lam_im, log_dt, b_re, b_im, c_re, c_im)
    y_s5 = _s5(proj3, (c_qkv + c_za) // LANE, bblk, cblk, lamrow, d_skip.reshape(1, wb)).reshape(t, wb)

    return _out_stage(
        o_a, proj, y_s5, x2, w_pa.astype(BF16), w_glu.astype(BF16), b_glu.reshape(1, wb),
        w_pb.astype(BF16), b_gate.reshape(1, 2 * d), w_out.astype(BF16), final_g.reshape(1, d),
        za_block=(c_qkv) // wa, zb_block=(c_qkv + c_za + c_u) // wb,
        gate_block0=(c_qkv + c_za + c_u + c_zb) // d, final_norm=final_norm)


def kernel(x, ln_g, w_in, conv_w, a_log, dt_bias, head_norm_g, lam_re, lam_im, log_dt, b_re, b_im, c_re, c_im, d_skip, w_glu, b_glu, w_pa, w_pb, b_gate, w_out, final_g):
    bsz, seq, d = x.shape
    depth = ln_g.shape[0]
    x2 = x.reshape(bsz * seq, d)
    for l in range(depth):
        x2 = _layer(x2, bsz, seq, ln_g[l], w_in[l], conv_w[l], a_log[l], dt_bias[l], head_norm_g[l],
                    lam_re[l], lam_im[l], log_dt[l], b_re[l], b_im[l], c_re[l], c_im[l], d_skip[l],
                    w_glu[l], b_glu[l], w_pa[l], w_pb[l], b_gate[l], w_out[l], final_g,
                    final_norm=(l == depth - 1))
    return x2.reshape(bsz, seq, d)
```

```python
import functools
import math

import jax
import jax.numpy as jnp
from jax import lax
from jax.experimental import pallas as pl
from jax.experimental.pallas import tpu as pltpu

F32 = jnp.float32
BF16 = jnp.bfloat16

LANE = 128
SUBLANE = 8
HEAD_DIM = 128
CHUNK = 64
CONV_K = 5
CONV_PAD = (CONV_K - 1) // 2
DELTA_HEADS_PER_PROGRAM = 2
LEVEL_GROUP_CHUNKS = 8
S5_GROUP_CH = 16
S5_STATE = 64
S5_BLOCK = 16
RMS_EPS = 1e-6
VMEM_LIMIT_BYTES = 56 * 1024 * 1024


def _compiler_params(semantics):
    return pltpu.CompilerParams(dimension_semantics=semantics, vmem_limit_bytes=VMEM_LIMIT_BYTES)


def _silu(x):
    return x * jax.nn.sigmoid(x)


def _largest_tile(n, cap, unit):
    best = unit
    t = unit
    while t <= min(n, cap):
        if n % t == 0:
            best = t
        t += unit
    return best


def _inproj_kernel(x_ref, g_ref, w_ref, o_ref, h_ref):
    @pl.when(pl.program_id(1) == 0)
    def _():
        x = x_ref[...]
        ms = jnp.mean(x * x, axis=-1, keepdims=True)
        h_ref[...] = (x * lax.rsqrt(ms + RMS_EPS) * g_ref[...]).astype(BF16)

    o_ref[...] = jnp.dot(h_ref[...], w_ref[...], preferred_element_type=F32)


def _inproj(x2, g_rows, w_bf16, layer):
    t, d = x2.shape
    n = w_bf16.shape[2]
    tm = _largest_tile(t, 1024, SUBLANE)
    tn = _largest_tile(n, 1280, LANE)
    return pl.pallas_call(
        _inproj_kernel,
        grid=(t // tm, n // tn),
        in_specs=[
            pl.BlockSpec((tm, d), lambda i, j: (i, 0)),
            pl.BlockSpec((None, 1, d), lambda i, j: (layer, 0, 0)),
            pl.BlockSpec((None, d, tn), lambda i, j: (layer, 0, j)),
        ],
        out_specs=pl.BlockSpec((tm, tn), lambda i, j: (i, j)),
        out_shape=jax.ShapeDtypeStruct((t, n), F32),
        scratch_shapes=[pltpu.VMEM((tm, d), BF16)],
        compiler_params=_compiler_params(("parallel", "arbitrary")),
        name="inproj",
    )(x2, g_rows, w_bf16)


def _gates_kernel(ba_ref, alog_ref, dtb_ref, o_ref, *, n_heads):
    x = ba_ref[...]
    tm = x.shape[0]
    lane = lax.broadcasted_iota(jnp.int32, x.shape, 1)
    beta = jax.nn.sigmoid(x)
    z = x + dtb_ref[...]
    softplus = jnp.maximum(z, 0.0) + jnp.log1p(jnp.exp(-jnp.abs(z)))
    g = -jnp.exp(alog_ref[...]) * softplus
    r = lax.broadcasted_iota(jnp.int32, (tm, tm), 0)
    c = lax.broadcasted_iota(jnp.int32, (tm, tm), 1)
    shift = int(math.log2(CHUNK))
    same_chunk = (r >> shift) == (c >> shift)
    m_prefix = jnp.where(same_chunk & (c <= r), 1.0, 0.0).astype(F32)
    m_suffix = jnp.where(same_chunk & (c >= r), 1.0, 0.0).astype(F32)
    g_prefix = jnp.dot(m_prefix, g, precision=lax.Precision.HIGHEST, preferred_element_type=F32)
    g_suffix = jnp.dot(m_suffix, g, precision=lax.Precision.HIGHEST, preferred_element_type=F32)
    gc = jnp.where(lane < 3 * n_heads, g_prefix, g_suffix)
    o_ref[...] = jnp.where(lane < 2 * n_heads, beta, gc)


def _gates(proj, ba_block, alog_rows, dtb_rows, layer, n_heads):
    t = proj.shape[0]
    tm = _largest_tile(t, 512, CHUNK)
    return pl.pallas_call(
        functools.partial(_gates_kernel, n_heads=n_heads),
        grid=(t // tm,),
        in_specs=[
            pl.BlockSpec((tm, LANE), lambda i: (i, ba_block)),
            pl.BlockSpec((None, 1, LANE), lambda i: (layer, 0, 0)),
            pl.BlockSpec((None, 1, LANE), lambda i: (layer, 0, 0)),
        ],
        out_specs=pl.BlockSpec((tm, LANE), lambda i: (i, 0)),
        out_shape=jax.ShapeDtypeStruct((t, LANE), F32),
        compiler_params=_compiler_params(("parallel",)),
        name="gates",
    )(proj, alog_rows, dtb_rows)


def _dot_nt(a, b):
    return lax.dot_general(a, b, (((1,), (1,)), ((), ())), preferred_element_type=F32)


def _delta_kernel(q_ref, k_ref, v_ref, wq_ref, wk_ref, wv_ref, cols_ref, rows_ref, hg_ref, o_ref,
                  pad_scr, r_scr, m_scr, wq_scr, kdt_scr, qk_scr, ob_scr, s_scr):
    seq = q_ref.shape[0]
    hpb = q_ref.shape[1] // HEAD_DIM
    n_chunks = seq // CHUNK
    halo = SUBLANE
    c2 = 2 * CHUNK
    operands = ((q_ref, wq_ref), (k_ref, wk_ref), (v_ref, wv_ref))
    for j in range(len(operands)):
        pad_scr[j, 0:halo, :] = jnp.zeros((halo, HEAD_DIM), F32)
        pad_scr[j, seq + halo:seq + 2 * halo, :] = jnp.zeros((halo, HEAD_DIM), F32)

    def conv_silu(j, w, r0):
        acc = None
        for i in range(CONV_K):
            term = pad_scr[j, pl.ds(r0 + halo + i - CONV_PAD, CHUNK), :] * w[i:i + 1, :]
            acc = term if acc is None else acc + term
        return _silu(acc)

    def l2norm(y):
        return y * lax.rsqrt(jnp.sum(y * y, axis=-1, keepdims=True) + RMS_EPS)

    ri = lax.broadcasted_iota(jnp.int32, (c2, c2), 0)
    ci = lax.broadcasted_iota(jnp.int32, (c2, c2), 1)
    fwd_block = (ri < CHUNK) & (ci < CHUNK)
    bwd_block = (ri >= CHUNK) & (ci >= CHUNK)
    strict2 = (fwd_block & (ri > ci)) | (bwd_block & (ri < ci))
    rw = lax.broadcasted_iota(jnp.int32, (CHUNK, c2), 0)
    cw = lax.broadcasted_iota(jnp.int32, (CHUNK, c2), 1)
    fwd_lanes = cw < CHUNK
    incl_wide = (fwd_lanes & (rw >= cw)) | (~fwd_lanes & (rw <= cw - CHUNK))
    top_rows = ri < CHUNK

    for p in range(hpb):
        lanes = slice(p * HEAD_DIM, (p + 1) * HEAD_DIM)

        def fill(n, carry, lanes=lanes):
            r0 = pl.multiple_of(n * CHUNK, CHUNK)
            for j, (src_ref, _) in enumerate(operands):
                pad_scr[j, pl.ds(r0 + halo, CHUNK), :] = src_ref[pl.ds(r0, CHUNK), lanes]
            return carry

        lax.fori_loop(0, n_chunks, fill, 0, unroll=2)

        def setup(n, carry, p=p, lanes=lanes):
            r0 = pl.multiple_of(n * CHUNK, CHUNK)
            q = l2norm(conv_silu(0, wq_ref[:, lanes], r0)) * (HEAD_DIM ** -0.5)
            k = l2norm(conv_silu(1, wk_ref[:, lanes], r0))
            v = conv_silu(2, wv_ref[:, lanes], r0)
            k16 = k.astype(BF16)
            k2_16 = jnp.concatenate([k16, k16], axis=0)
            kk2 = _dot_nt(k2_16, k2_16)
            qk_wide = _dot_nt(q.astype(BF16), k2_16)
            cols = cols_ref[pl.ds(r0, CHUNK), :]
            across = lambda c: jnp.broadcast_to(cols[:, 4 * p + c:4 * p + c + 1], (CHUNK, c2))
            gc2 = jnp.concatenate([across(0), across(1)], axis=0)
            be2 = jnp.concatenate([across(2), across(3)], axis=0)
            gc_row2 = rows_ref[p, n]
            k2 = jnp.concatenate([k, k], axis=0)
            v2 = jnp.concatenate([v, v], axis=0)
            q2 = jnp.concatenate([q, q], axis=0)
            decay2 = jnp.where(strict2, jnp.exp(gc2 - gc_row2), 0.0)
            m_scr[p, n] = (-(be2 * kk2 * decay2)).astype(BF16)
            eg2 = jnp.exp(gc2)
            r_scr[p, n] = jnp.concatenate([v2 * be2, k2 * (be2 * eg2)], axis=1)
            qg2 = (q2 * eg2).astype(BF16)
            wq_scr[p, 0, n, CHUNK:c2, :] = qg2[:CHUNK]
            wq_scr[p, 1, n, CHUNK:c2, :] = qg2[CHUNK:]
            g_last2 = jnp.where(top_rows, gc_row2[:, CHUNK - 1:CHUNK], gc_row2[:, CHUNK:CHUNK + 1])
            kdec2 = k2 * jnp.exp(g_last2 - gc2)
            kdt_scr[p, n] = kdec2.T.astype(BF16)
            gc_wide = jnp.where(fwd_lanes, gc2[:CHUNK], gc2[CHUNK:])
            qk_scr[p, n] = (qk_wide * jnp.where(incl_wide, jnp.exp(gc_wide - gc_row2), 0.0)).astype(BF16)
            return carry

        lax.fori_loop(0, n_chunks, setup, 0, unroll=2)

    group = _largest_tile(n_chunks, LEVEL_GROUP_CHUNKS, 1)

    def level(_, carry):
        def per_group(it, carry):
            tiles = [(p, it * group + g) for g in range(group) for p in range(hpb)]
            outs = []
            for p, n in tiles:
                m = m_scr[p, n]
                rhs = jnp.concatenate([r_scr[p, n].astype(BF16), m], axis=1)
                outs.append(jnp.dot(m, rhs, preferred_element_type=F32))
            for (p, n), out in zip(tiles, outs):
                r_scr[p, n] += out[:, :2 * HEAD_DIM]
                m_scr[p, n] = out[:, 2 * HEAD_DIM:].astype(BF16)
            return carry

        return lax.fori_loop(0, n_chunks // group, per_group, carry)

    lax.fori_loop(0, int(math.log2(CHUNK)), level, 0)

    def finish(n, carry):
        for p in range(hpb):
            w2 = r_scr[p, n, :, HEAD_DIM:2 * HEAD_DIM].astype(BF16)
            wq_scr[p, 0, n, 0:CHUNK, :] = w2[:CHUNK]
            wq_scr[p, 1, n, 0:CHUNK, :] = w2[CHUNK:]
        return carry

    lax.fori_loop(0, n_chunks, finish, 0, unroll=4)

    s_scr[...] = jnp.zeros(s_scr.shape, F32)
    zero_half = jnp.zeros((CHUNK, HEAD_DIM), BF16)
    chains = [(p, d) for p in range(hpb) for d in range(2)]

    def recurrence(i, carry):
        chunk_of = (i, n_chunks - 1 - i)
        states = [s_scr[p, d] for p, d in chains]
        ws = [jnp.dot(wq_scr[p, d, chunk_of[d]], s.astype(BF16), preferred_element_type=F32)
              for (p, d), s in zip(chains, states)]
        new_states = []
        for (p, d), s, w in zip(chains, states, ws):
            n = chunk_of[d]
            r0 = pl.multiple_of(n * CHUNK, CHUNK)
            v_new = (r_scr[p, n, d * CHUNK:(d + 1) * CHUNK, 0:HEAD_DIM] - w[:CHUNK]).astype(BF16)
            v_pad = jnp.concatenate([v_new, zero_half] if d == 0 else [zero_half, v_new], axis=0)
            o = w[CHUNK:] + jnp.dot(qk_scr[p, n], v_pad, preferred_element_type=F32)
            if d == 0:
                o_ref[pl.ds(r0, CHUNK), p * HEAD_DIM:(p + 1) * HEAD_DIM] = o
            else:
                ob_scr[p, pl.ds(r0, CHUNK), :] = o
            gc_row2 = rows_ref[p, n]
            g_last = gc_row2[:, CHUNK - 1:CHUNK] if d == 0 else gc_row2[:, CHUNK:CHUNK + 1]
            new_states.append(s * jnp.exp(g_last) + jnp.dot(kdt_scr[p, n], v_pad, preferred_element_type=F32))
        for (p, d), s in zip(chains, new_states):
            s_scr[p, d] = s
        return carry

    lax.fori_loop(0, n_chunks, recurrence, 0)

    def head_norm(n, carry):
        r0 = pl.multiple_of(n * CHUNK, CHUNK)
        for p in range(hpb):
            lanes = slice(p * HEAD_DIM, (p + 1) * HEAD_DIM)
            o = o_ref[pl.ds(r0, CHUNK), lanes] + ob_scr[p, pl.ds(r0, CHUNK), :]
            ms = jnp.mean(o * o, axis=-1, keepdims=True)
            o_ref[pl.ds(r0, CHUNK), lanes] = o * lax.rsqrt(ms + RMS_EPS) * hg_ref[...]
        return carry

    lax.fori_loop(0, n_chunks, head_norm, 0, unroll=2)


def _delta(proj3, conv_w, cols, rows, hg_rows, layer, n_heads):
    bsz, seq, _ = proj3.shape
    n_chunks = seq // CHUNK
    hpb = DELTA_HEADS_PER_PROGRAM
    width = hpb * HEAD_DIM
    n_blocks = n_heads // hpb
    tok = lambda off: pl.BlockSpec((None, seq, width), lambda b, h: (b, 0, off + h))
    cw = lambda off: pl.BlockSpec((None, CONV_K, width), lambda b, h: (layer, 0, off + h))
    return pl.pallas_call(
        _delta_kernel,
        grid=(bsz, n_blocks),
        in_specs=[
            tok(0), tok(n_blocks), tok(2 * n_blocks),
            cw(0), cw(n_blocks), cw(2 * n_blocks),
            pl.BlockSpec((None, None, seq, 4 * hpb), lambda b, h: (b, h, 0, 0)),
            pl.BlockSpec((None, hpb, n_chunks, 1, 2 * CHUNK), lambda b, h: (b, h, 0, 0, 0)),
            pl.BlockSpec((None, 1, HEAD_DIM), lambda b, h: (layer, 0, 0)),
        ],
        out_specs=pl.BlockSpec((None, seq, width), lambda b, h: (b, 0, h)),
        out_shape=jax.ShapeDtypeStruct((bsz, seq, n_heads * HEAD_DIM), F32),
        scratch_shapes=[
            pltpu.VMEM((3, seq + 2 * SUBLANE, HEAD_DIM), F32),
            pltpu.VMEM((hpb, n_chunks, 2 * CHUNK, 2 * HEAD_DIM), F32),
            pltpu.VMEM((hpb, n_chunks, 2 * CHUNK, 2 * CHUNK), BF16),
            pltpu.VMEM((hpb, 2, n_chunks, 2 * CHUNK, HEAD_DIM), BF16),
            pltpu.VMEM((hpb, n_chunks, HEAD_DIM, 2 * CHUNK), BF16),
            pltpu.VMEM((hpb, n_chunks, CHUNK, 2 * CHUNK), BF16),
            pltpu.VMEM((hpb, seq, HEAD_DIM), F32),
            pltpu.VMEM((hpb, 2, HEAD_DIM, HEAD_DIM), F32),
        ],
        compiler_params=_compiler_params(("parallel", "parallel")),
        name="delta",
    )(proj3, proj3, proj3, conv_w, conv_w, conv_w, cols, rows, hg_rows)


def _s5_kernel(u_ref, m_ref, wsum_ref, wout_ref, a_ref, y_ref, s_scr, *, bsz):
    rows = u_ref.shape[1]
    w = 2 * S5_STATE
    u = [u_ref[0], u_ref[1]]
    s_scr[...] = jnp.dot(jnp.concatenate(u, axis=1), wsum_ref[...], preferred_element_type=F32)
    a = a_ref[...]
    coef = [jnp.broadcast_to(a[:, i * w:(i + 1) * w], (SUBLANE, w)) for i in range(4)]
    per_tile = SUBLANE // bsz
    n_tiles = rows // SUBLANE
    row = lax.broadcasted_iota(jnp.int32, (SUBLANE, w), 0)
    in_slot = [(row >= k * bsz) & (row < (k + 1) * bsz) for k in range(per_tile)]

    def scan_tile(r0, lane0, a_re, a_im, x_re, x_im, forward):
        s_re, s_im = s_scr[pl.ds(r0, SUBLANE), lane0:lane0 + w], s_scr[pl.ds(r0, SUBLANE), lane0 + w:lane0 + 2 * w]
        shift = bsz if forward else SUBLANE - bsz
        in_re, in_im = jnp.zeros((SUBLANE, w), F32), jnp.zeros((SUBLANE, w), F32)
        for k in range(per_tile):
            slot = in_slot[k] if forward else in_slot[per_tile - 1 - k]
            in_re, in_im = jnp.where(slot, x_re, in_re), jnp.where(slot, x_im, in_im)
            y_re = a_re * x_re - a_im * x_im + s_re
            y_im = a_re * x_im + a_im * x_re + s_im
            x_re, x_im = pltpu.roll(y_re, shift, axis=0), pltpu.roll(y_im, shift, axis=0)
        s_scr[pl.ds(r0, SUBLANE), lane0:lane0 + w] = in_re
        s_scr[pl.ds(r0, SUBLANE), lane0 + w:lane0 + 2 * w] = in_im
        return x_re, x_im

    def step(i, xs):
        rf = pl.multiple_of(i * SUBLANE, SUBLANE)
        rb = pl.multiple_of((n_tiles - 1 - i) * SUBLANE, SUBLANE)
        return (scan_tile(rf, 0, coef[0], coef[1], xs[0], xs[1], True)
                + scan_tile(rb, 2 * w, coef[2], coef[3], xs[2], xs[3], False))

    zero = jnp.zeros((SUBLANE, w), F32)
    lax.fori_loop(0, n_tiles, step, (zero, zero, zero, zero))

    carried = jnp.dot(s_scr[...].astype(BF16), wout_ref[...], preferred_element_type=F32)
    lanes = u_ref.shape[2]
    for g in range(2):
        y_ref[g] = jnp.dot(u[g], m_ref[g], preferred_element_type=F32) + carried[:, g * lanes:(g + 1) * lanes]


def _s5(u_blocks, m_mat, w_sum, w_out, a_blk, layer, bsz):
    n_groups, rows, lanes = u_blocks.shape
    n_pairs = n_groups // 2
    return pl.pallas_call(
        functools.partial(_s5_kernel, bsz=bsz),
        grid=(n_pairs,),
        in_specs=[
            pl.BlockSpec((2, rows, lanes), lambda q: (q, 0, 0)),
            pl.BlockSpec((None, 2, lanes, lanes), lambda q: (layer, q, 0, 0)),
            pl.BlockSpec((None, None, 2 * lanes, 8 * S5_STATE), lambda q: (layer, q, 0, 0)),
            pl.BlockSpec((None, None, 8 * S5_STATE, 2 * lanes), lambda q: (layer, q, 0, 0)),
            pl.BlockSpec((None, None, 1, 8 * S5_STATE), lambda q: (layer, q, 0, 0)),
        ],
        out_specs=pl.BlockSpec((2, rows, lanes), lambda q: (q, 0, 0)),
        out_shape=jax.ShapeDtypeStruct((n_groups, rows, lanes), F32),
        scratch_shapes=[pltpu.VMEM((rows, 8 * S5_STATE), F32)],
        compiler_params=_compiler_params(("parallel",)),
        name="s5",
    )(u_blocks, m_mat, w_sum, w_out, a_blk)


def _s5_params(lam_re, lam_im, log_dt, b_re, b_im, c_re, c_im):
    n_groups = lam_re.shape[1]
    sb, p, c = S5_BLOCK, S5_STATE, S5_GROUP_CH
    l_re, l_im = lam_re.astype(F32), lam_im.astype(F32)
    dt = jnp.exp(log_dt.astype(F32))[..., None]
    mag = jnp.exp(l_re * dt)
    bar_re, bar_im = mag * jnp.cos(l_im * dt), mag * jnp.sin(l_im * dt)
    n_re, n_im = bar_re - 1.0, bar_im
    den = l_re * l_re + l_im * l_im
    k_re = ((n_re * l_re + n_im * l_im) / den)[..., None]
    k_im = ((n_im * l_re - n_re * l_im) / den)[..., None]
    bb_re = k_re * b_re.astype(F32) - k_im * b_im.astype(F32)
    bb_im = k_re * b_im.astype(F32) + k_im * b_re.astype(F32)
    cc_re, cc_im = c_re.astype(F32), c_im.astype(F32)

    def next_power(prev, _):
        nxt = (prev[0] * bar_re - prev[1] * bar_im, prev[0] * bar_im + prev[1] * bar_re)
        return nxt, prev
    last, powers = lax.scan(next_power, (jnp.ones_like(bar_re), jnp.zeros_like(bar_re)), None, length=sb)
    pw_re = jnp.concatenate([powers[0], last[0][None]], axis=0)
    pw_im = jnp.concatenate([powers[1], last[1][None]], axis=0)

    cp_re = cc_re[None] * pw_re[:sb, :, :, None, :] - cc_im[None] * pw_im[:sb, :, :, None, :]
    cp_im = cc_re[None] * pw_im[:sb, :, :, None, :] + cc_im[None] * pw_re[:sb, :, :, None, :]
    kern = jnp.einsum('ldgop,dgpi->ldgoi', cp_re, bb_re) - jnp.einsum('ldgop,dgpi->ldgoi', cp_im, bb_im)
    idx = jnp.arange(sb)
    lag = idx[None, :] - idx[:, None]
    k_f = jnp.where((lag >= 0)[:, :, None, None, None], kern[jnp.clip(lag, 0, sb - 1), 0], 0.0)
    k_b = jnp.where((lag <= 0)[:, :, None, None, None], kern[jnp.clip(-lag, 0, sb - 1), 1], 0.0)
    m_mat = jnp.transpose(k_f + k_b, (2, 0, 4, 1, 3)).reshape(n_groups, sb * c, sb * c)

    pw_f_re, pw_f_im = pw_re[sb - 1 - idx, 0], pw_im[sb - 1 - idx, 0]
    pw_b_re, pw_b_im = pw_re[idx, 1], pw_im[idx, 1]
    def summary(p_re, p_im, d):
        s_re = p_re[:, :, :, None] * bb_re[d][None] - p_im[:, :, :, None] * bb_im[d][None]
        s_im = p_re[:, :, :, None] * bb_im[d][None] + p_im[:, :, :, None] * bb_re[d][None]
        lay = lambda t: jnp.transpose(t, (1, 0, 3, 2)).reshape(n_groups, sb * c, p)
        return lay(s_re), lay(s_im)
    sums = summary(pw_f_re, pw_f_im, 0) + summary(pw_b_re, pw_b_im, 1)

    po_f_re, po_f_im = pw_re[idx + 1, 0], pw_im[idx + 1, 0]
    po_b_re, po_b_im = pw_re[sb - idx, 1], pw_im[sb - idx, 1]
    def readout(p_re, p_im, d):
        z_re = cc_re[d][None] * p_re[:, :, None, :] - cc_im[d][None] * p_im[:, :, None, :]
        z_im = cc_re[d][None] * p_im[:, :, None, :] + cc_im[d][None] * p_re[:, :, None, :]
        lay = lambda t: jnp.transpose(t, (1, 3, 0, 2)).reshape(n_groups, p, sb * c)
        return lay(z_re), lay(-z_im)
    outs = readout(po_f_re, po_f_im, 0) + readout(po_b_re, po_b_im, 1)

    def pair_sum(t):
        t2 = t.reshape(n_groups // 2, 2, sb * c, p)
        return jnp.einsum('qgrp,gh->qgrhp', t2, jnp.eye(2, dtype=F32)).reshape(n_groups // 2, 2 * sb * c, 2 * p)
    def pair_out(t):
        t2 = t.reshape(n_groups // 2, 2, p, sb * c)
        return jnp.einsum('qgpr,gh->qgphr', t2, jnp.eye(2, dtype=F32)).reshape(n_groups // 2, 2 * p, 2 * sb * c)
    w_sum = jnp.concatenate([pair_sum(t) for t in sums], axis=-1).astype(BF16)
    w_out = jnp.concatenate([pair_out(t) for t in outs], axis=-2).astype(BF16)
    a_parts = (pw_re[sb, 0], pw_im[sb, 0], pw_re[sb, 1], pw_im[sb, 1])
    a_blk = jnp.concatenate([t.reshape(n_groups // 2, 1, 2 * p) for t in a_parts], axis=-1)
    return m_mat.astype(BF16), w_sum, w_out, a_blk


def _out_kernel(oa_ref, za_ref, ys_ref, u_ref, zb_ref, ga_ref, gb_ref, x_ref,
                wpa_ref, dsk_ref, wglu_ref, bglu_ref, wpb_ref, bga_ref, bgb_ref, wout_ref, fg_ref,
                o_ref, *, final_norm):
    a = (oa_ref[...] * _silu(za_ref[...])).astype(BF16)
    y_a = jnp.dot(a, wpa_ref[...], preferred_element_type=F32)
    y_s = jax.nn.gelu(ys_ref[...] + u_ref[...] * dsk_ref[...])
    glu = jnp.dot(y_s.astype(BF16), wglu_ref[...], preferred_element_type=F32) + bglu_ref[...]
    y_s = y_s * jax.nn.sigmoid(glu)
    y_b = jnp.dot((y_s * _silu(zb_ref[...])).astype(BF16), wpb_ref[...], preferred_element_type=F32)
    merged = (jax.nn.sigmoid(ga_ref[...] + bga_ref[...]) * y_a
              + jax.nn.sigmoid(gb_ref[...] + bgb_ref[...]) * y_b)
    out = x_ref[...] + jnp.dot(merged.astype(BF16), wout_ref[...], preferred_element_type=F32)
    if final_norm:
        ms = jnp.mean(out * out, axis=-1, keepdims=True)
        out = out * lax.rsqrt(ms + RMS_EPS) * fg_ref[...]
    o_ref[...] = out


def _out_stage(o_a, proj, y_s5, x2, w_pa, d_skip, w_glu, b_glu, w_pb, b_gate, w_out, final_g, *,
               layer, za_block, u_block, zb_block, gate_block0, final_norm):
    t, d = x2.shape
    wa = o_a.shape[1]
    wb = y_s5.shape[1]
    tm = _largest_tile(t, 256, SUBLANE)
    row = lambda width, blk: pl.BlockSpec((tm, width), lambda i: (i, blk))
    const = lambda shape, blk=0: pl.BlockSpec((None,) + shape, lambda i: (layer, 0, blk),
                                              pipeline_mode=pl.Buffered(1))
    return pl.pallas_call(
        functools.partial(_out_kernel, final_norm=final_norm),
        grid=(t // tm,),
        in_specs=[
            row(wa, 0), row(wa, za_block), row(wb, 0), row(wb, u_block), row(wb, zb_block),
            row(d, gate_block0), row(d, gate_block0 + 1), row(d, 0),
            const((wa, d)), const((1, wb)), const((wb, wb)), const((1, wb)), const((wb, d)),
            const((1, d), 0), const((1, d), 1), const((d, d)), const((1, d)),
        ],
        out_specs=pl.BlockSpec((tm, d), lambda i: (i, 0)),
        out_shape=jax.ShapeDtypeStruct((t, d), F32),
        compiler_params=_compiler_params(("parallel",)),
        name="out_stage",
    )(o_a, proj, y_s5, proj, proj, proj, proj, x2, w_pa, d_skip, w_glu, b_glu, w_pb, b_gate, b_gate, w_out,
      final_g)


def _prepare_params(ln_g, w_in, conv_w, a_log, dt_bias, head_norm_g, lam_re, lam_im, log_dt, b_re, b_im,
                    c_re, c_im, d_skip, w_glu, b_glu, w_pa, w_pb, b_gate, w_out, final_g):
    depth, d, _ = w_in.shape
    n_heads = a_log.shape[2]
    wa = n_heads * HEAD_DIM
    wb = d_skip.shape[1]
    n_ba = 4 * n_heads
    o_beta = 4 * wa
    o_u = o_beta + n_ba
    w_cols = jnp.concatenate(
        [w_in[:, :, :o_beta], w_in[:, :, o_u:], w_in[:, :, o_beta:o_u],
         jnp.zeros((depth, d, LANE - n_ba), w_in.dtype)], axis=2).astype(BF16)
    pad_rows = lambda p: jnp.pad(p.astype(F32).reshape(depth, 1, n_ba // 2),
                                 ((0, 0), (0, 0), (n_ba // 2, LANE - n_ba)))
    row = lambda p: p.astype(F32).reshape(depth, 1, -1)
    return dict(
        ln_g=row(ln_g), w_cols=w_cols, conv_w=conv_w.astype(F32), alog=pad_rows(a_log), dtb=pad_rows(dt_bias),
        head_norm_g=row(head_norm_g),
        s5=jax.vmap(_s5_params)(lam_re, lam_im, log_dt, b_re, b_im, c_re, c_im),
        d_skip=row(d_skip), w_glu=w_glu.astype(BF16), b_glu=row(b_glu), w_pa=w_pa.astype(BF16),
        w_pb=w_pb.astype(BF16), b_gate=row(b_gate), w_out=w_out.astype(BF16),
        final_g=jnp.broadcast_to(final_g.astype(F32).reshape(1, 1, d), (depth, 1, d)),
        n_heads=n_heads, wa=wa, wb=wb)


def _layer(x2, bsz, seq, prm, layer, final_norm):
    t, d = x2.shape
    n_heads, wa, wb = prm["n_heads"], prm["wa"], prm["wb"]
    assert wa == wb and seq % (CHUNK * SUBLANE) == 0 and 4 * n_heads <= LANE
    c_qkv, c_za, c_beta, c_alpha, c_u, c_zb = 3 * wa, wa, 2 * n_heads, 2 * n_heads, wb, wb
    n_proj = prm["w_cols"].shape[2]
    ba_block = (n_proj - LANE) // LANE

    proj = _inproj(x2, prm["ln_g"], prm["w_cols"], layer)
    proj3 = proj.reshape(bsz, seq, n_proj)

    gates = _gates(proj, ba_block, prm["alog"], prm["dtb"], layer, n_heads)
    beta = gates[:, :c_beta].reshape(bsz, seq, 2, n_heads)
    gc = gates[:, c_beta:c_beta + c_alpha].reshape(bsz, seq, 2, n_heads)
    hpb = DELTA_HEADS_PER_PROGRAM
    cols = jnp.stack([gc[:, :, 0], gc[:, :, 1], beta[:, :, 0], beta[:, :, 1]], axis=-1)
    cols = jnp.transpose(cols.reshape(bsz, seq, n_heads // hpb, 4 * hpb), (0, 2, 1, 3))
    rows = jnp.transpose(gc.reshape(bsz, seq // CHUNK, CHUNK, 2, n_heads), (0, 4, 1, 3, 2))
    rows = rows.reshape(bsz, n_heads, seq // CHUNK, 1, 2 * CHUNK)

    o_a = _delta(proj3, prm["conv_w"], cols, rows, prm["head_norm_g"], layer, n_heads)
    o_a = o_a.reshape(t, wa)

    n_groups = wb // S5_GROUP_CH
    n_tb = seq // S5_BLOCK
    u_off = c_qkv + c_za
    u_blocks = proj[:, u_off:u_off + c_u].reshape(bsz, n_tb, S5_BLOCK, n_groups, S5_GROUP_CH)
    u_blocks = jnp.transpose(u_blocks, (3, 1, 0, 2, 4)).reshape(n_groups, n_tb * bsz, S5_BLOCK * S5_GROUP_CH)
    y_blocks = _s5(u_blocks.astype(BF16), *prm["s5"], layer, bsz)
    y_s5 = jnp.transpose(y_blocks.reshape(n_groups, n_tb, bsz, S5_BLOCK, S5_GROUP_CH), (2, 1, 3, 0, 4))
    y_s5 = y_s5.reshape(t, wb)

    return _out_stage(
        o_a, proj, y_s5, x2, prm["w_pa"], prm["d_skip"], prm["w_glu"], prm["b_glu"], prm["w_pb"],
        prm["b_gate"], prm["w_out"], prm["final_g"], layer=layer,
        za_block=c_qkv // wa, u_block=u_off // wb, zb_block=(u_off + c_u) // wb,
        gate_block0=(u_off + c_u + c_zb) // d, final_norm=final_norm)


def kernel(x, ln_g, w_in, conv_w, a_log, dt_bias, head_norm_g, lam_re, lam_im, log_dt, b_re, b_im, c_re, c_im, d_skip, w_glu, b_glu, w_pa, w_pb, b_gate, w_out, final_g):
    bsz, seq, d = x.shape
    depth = ln_g.shape[0]
    prm = _prepare_params(ln_g, w_in, conv_w, a_log, dt_bias, head_norm_g, lam_re, lam_im, log_dt, b_re, b_im,
                          c_re, c_im, d_skip, w_glu, b_glu, w_pa, w_pb, b_gate, w_out, final_g)
    x2 = x.reshape(bsz * seq, d)
    for layer in range(depth):
        x2 = _layer(x2, bsz, seq, prm, layer, final_norm=(layer == depth - 1))
    return x2.reshape(bsz, seq, d)
```

```python
import functools
import math

import jax
import jax.numpy as jnp
from jax import lax
from jax.experimental import pallas as pl
from jax.experimental.pallas import tpu as pltpu

F32 = jnp.float32
BF16 = jnp.bfloat16

LANE = 128
SUBLANE = 8
HEAD_DIM = 128
CHUNK = 64
CONV_K = 5
CONV_PAD = (CONV_K - 1) // 2
DELTA_HEADS_PER_PROGRAM = 2
LEVEL_GROUP_CHUNKS = 8
S5_GROUP_CH = 16
S5_STATE = 64
S5_BLOCK = 16
RMS_EPS = 1e-6
VMEM_LIMIT_BYTES = 56 * 1024 * 1024


def _compiler_params(semantics):
    return pltpu.CompilerParams(dimension_semantics=semantics, vmem_limit_bytes=VMEM_LIMIT_BYTES)


def _silu(x):
    return x * jax.nn.sigmoid(x)


def _largest_tile(n, cap, unit):
    best = unit
    t = unit
    while t <= min(n, cap):
        if n % t == 0:
            best = t
        t += unit
    return best


def _inproj_kernel(x_ref, g_ref, w_ref, o_ref, h_ref):
    @pl.when(pl.program_id(1) == 0)
    def _():
        x = x_ref[...]
        ms = jnp.mean(x * x, axis=-1, keepdims=True)
        h_ref[...] = (x * lax.rsqrt(ms + RMS_EPS) * g_ref[...]).astype(BF16)

    o_ref[...] = jnp.dot(h_ref[...], w_ref[...], preferred_element_type=F32)


def _inproj(x2, g_rows, w_bf16, layer):
    t, d = x2.shape
    n = w_bf16.shape[2]
    tm = _largest_tile(t, 1024, SUBLANE)
    tn = _largest_tile(n, 1280, LANE)
    return pl.pallas_call(
        _inproj_kernel,
        grid=(t // tm, n // tn),
        in_specs=[
            pl.BlockSpec((tm, d), lambda i, j: (i, 0)),
            pl.BlockSpec((None, 1, d), lambda i, j: (layer, 0, 0)),
            pl.BlockSpec((None, d, tn), lambda i, j: (layer, 0, j)),
        ],
        out_specs=pl.BlockSpec((tm, tn), lambda i, j: (i, j)),
        out_shape=jax.ShapeDtypeStruct((t, n), F32),
        scratch_shapes=[pltpu.VMEM((tm, d), BF16)],
        compiler_params=_compiler_params(("parallel", "arbitrary")),
        name="inproj",
    )(x2, g_rows, w_bf16)


def _gates_kernel(ba_ref, alog_ref, dtb_ref, o_ref, *, n_heads):
    x = ba_ref[...]
    tm = x.shape[0]
    lane = lax.broadcasted_iota(jnp.int32, x.shape, 1)
    beta = jax.nn.sigmoid(x)
    z = x + dtb_ref[...]
    softplus = jnp.maximum(z, 0.0) + jnp.log1p(jnp.exp(-jnp.abs(z)))
    g = -jnp.exp(alog_ref[...]) * softplus
    r = lax.broadcasted_iota(jnp.int32, (tm, tm), 0)
    c = lax.broadcasted_iota(jnp.int32, (tm, tm), 1)
    shift = int(math.log2(CHUNK))
    same_chunk = (r >> shift) == (c >> shift)
    m_prefix = jnp.where(same_chunk & (c <= r), 1.0, 0.0).astype(F32)
    m_suffix = jnp.where(same_chunk & (c >= r), 1.0, 0.0).astype(F32)
    g_prefix = jnp.dot(m_prefix, g, precision=lax.Precision.HIGHEST, preferred_element_type=F32)
    g_suffix = jnp.dot(m_suffix, g, precision=lax.Precision.HIGHEST, preferred_element_type=F32)
    gc = jnp.where(lane < 3 * n_heads, g_prefix, g_suffix)
    o_ref[...] = jnp.where(lane < 2 * n_heads, beta, gc)


def _gates(proj, ba_block, alog_rows, dtb_rows, layer, n_heads):
    t = proj.shape[0]
    tm = _largest_tile(t, 512, CHUNK)
    return pl.pallas_call(
        functools.partial(_gates_kernel, n_heads=n_heads),
        grid=(t // tm,),
        in_specs=[
            pl.BlockSpec((tm, LANE), lambda i: (i, ba_block)),
            pl.BlockSpec((None, 1, LANE), lambda i: (layer, 0, 0)),
            pl.BlockSpec((None, 1, LANE), lambda i: (layer, 0, 0)),
        ],
        out_specs=pl.BlockSpec((tm, LANE), lambda i: (i, 0)),
        out_shape=jax.ShapeDtypeStruct((t, LANE), F32),
        compiler_params=_compiler_params(("parallel",)),
        name="gates",
    )(proj, alog_rows, dtb_rows)


def _dot_nt(a, b):
    return lax.dot_general(a, b, (((1,), (1,)), ((), ())), preferred_element_type=F32)


def _delta_kernel(q_ref, k_ref, v_ref, wq_ref, wk_ref, wv_ref, cols_ref, rows_ref, hg_ref, o_ref,
                  pad_scr, r_scr, m_scr, wq_scr, kdt_scr, qk_scr, ob_scr, s_scr):
    seq = q_ref.shape[0]
    hpb = q_ref.shape[1] // HEAD_DIM
    n_chunks = seq // CHUNK
    halo = SUBLANE
    c2 = 2 * CHUNK
    operands = ((q_ref, wq_ref), (k_ref, wk_ref), (v_ref, wv_ref))
    for j in range(len(operands)):
        pad_scr[j, 0:halo, :] = jnp.zeros((halo, HEAD_DIM), F32)
        pad_scr[j, seq + halo:seq + 2 * halo, :] = jnp.zeros((halo, HEAD_DIM), F32)

    def conv_silu(j, w, r0):
        acc = None
        for i in range(CONV_K):
            term = pad_scr[j, pl.ds(r0 + halo + i - CONV_PAD, CHUNK), :] * w[i:i + 1, :]
            acc = term if acc is None else acc + term
        return _silu(acc)

    def l2norm(y):
        return y * lax.rsqrt(jnp.sum(y * y, axis=-1, keepdims=True) + RMS_EPS)

    ri = lax.broadcasted_iota(jnp.int32, (c2, c2), 0)
    ci = lax.broadcasted_iota(jnp.int32, (c2, c2), 1)
    fwd_block = (ri < CHUNK) & (ci < CHUNK)
    bwd_block = (ri >= CHUNK) & (ci >= CHUNK)
    strict2 = (fwd_block & (ri > ci)) | (bwd_block & (ri < ci))
    rw = lax.broadcasted_iota(jnp.int32, (CHUNK, c2), 0)
    cw = lax.broadcasted_iota(jnp.int32, (CHUNK, c2), 1)
    fwd_lanes = cw < CHUNK
    incl_wide = (fwd_lanes & (rw >= cw)) | (~fwd_lanes & (rw <= cw - CHUNK))
    top_rows = ri < CHUNK

    for p in range(hpb):
        lanes = slice(p * HEAD_DIM, (p + 1) * HEAD_DIM)

        def fill(n, carry, lanes=lanes):
            r0 = pl.multiple_of(n * CHUNK, CHUNK)
            for j, (src_ref, _) in enumerate(operands):
                pad_scr[j, pl.ds(r0 + halo, CHUNK), :] = src_ref[pl.ds(r0, CHUNK), lanes]
            return carry

        lax.fori_loop(0, n_chunks, fill, 0, unroll=2)

        def setup(n, carry, p=p, lanes=lanes):
            r0 = pl.multiple_of(n * CHUNK, CHUNK)
            q = l2norm(conv_silu(0, wq_ref[:, lanes], r0)) * (HEAD_DIM ** -0.5)
            k = l2norm(conv_silu(1, wk_ref[:, lanes], r0))
            v = conv_silu(2, wv_ref[:, lanes], r0)
            k16 = k.astype(BF16)
            k2_16 = jnp.concatenate([k16, k16], axis=0)
            kk2 = _dot_nt(k2_16, k2_16)
            qk_wide = _dot_nt(q.astype(BF16), k2_16)
            cols = cols_ref[pl.ds(r0, CHUNK), :]
            across = lambda c: jnp.broadcast_to(cols[:, 4 * p + c:4 * p + c + 1], (CHUNK, c2))
            gc2 = jnp.concatenate([across(0), across(1)], axis=0)
            be2 = jnp.concatenate([across(2), across(3)], axis=0)
            gc_row2 = rows_ref[p, n]
            k2 = jnp.concatenate([k, k], axis=0)
            v2 = jnp.concatenate([v, v], axis=0)
            q2 = jnp.concatenate([q, q], axis=0)
            decay2 = jnp.where(strict2, jnp.exp(gc2 - gc_row2), 0.0)
            m_scr[p, n] = (-(be2 * kk2 * decay2)).astype(BF16)
            eg2 = jnp.exp(gc2)
            r_scr[p, n] = jnp.concatenate([v2 * be2, k2 * (be2 * eg2)], axis=1)
            qg2 = (q2 * eg2).astype(BF16)
            wq_scr[p, 0, n, CHUNK:c2, :] = qg2[:CHUNK]
            wq_scr[p, 1, n, CHUNK:c2, :] = qg2[CHUNK:]
            g_last2 = jnp.where(top_rows, gc_row2[:, CHUNK - 1:CHUNK], gc_row2[:, CHUNK:CHUNK + 1])
            kdec2 = k2 * jnp.exp(g_last2 - gc2)
            kdt_scr[p, n] = kdec2.T.astype(BF16)
            gc_wide = jnp.where(fwd_lanes, gc2[:CHUNK], gc2[CHUNK:])
            qk_scr[p, n] = (qk_wide * jnp.where(incl_wide, jnp.exp(gc_wide - gc_row2), 0.0)).astype(BF16)
            return carry

        lax.fori_loop(0, n_chunks, setup, 0, unroll=2)

    group = _largest_tile(n_chunks, LEVEL_GROUP_CHUNKS, 1)

    def level(_, carry):
        def per_group(it, carry):
            tiles = [(p, it * group + g) for g in range(group) for p in range(hpb)]
            outs = []
            for p, n in tiles:
                m = m_scr[p, n]
                rhs = jnp.concatenate([r_scr[p, n].astype(BF16), m], axis=1)
                outs.append(jnp.dot(m, rhs, preferred_element_type=F32))
            for (p, n), out in zip(tiles, outs):
                r_scr[p, n] += out[:, :2 * HEAD_DIM]
                m_scr[p, n] = out[:, 2 * HEAD_DIM:].astype(BF16)
            return carry

        return lax.fori_loop(0, n_chunks // group, per_group, carry)

    lax.fori_loop(0, int(math.log2(CHUNK)), level, 0)

    def finish(n, carry):
        for p in range(hpb):
            w2 = r_scr[p, n, :, HEAD_DIM:2 * HEAD_DIM].astype(BF16)
            wq_scr[p, 0, n, 0:CHUNK, :] = w2[:CHUNK]
            wq_scr[p, 1, n, 0:CHUNK, :] = w2[CHUNK:]
        return carry

    lax.fori_loop(0, n_chunks, finish, 0, unroll=4)

    s_scr[...] = jnp.zeros(s_scr.shape, F32)
    zero_half = jnp.zeros((CHUNK, HEAD_DIM), BF16)
    chains = [(p, d) for p in range(hpb) for d in range(2)]

    def recurrence(i, carry):
        chunk_of = (i, n_chunks - 1 - i)
        states = [s_scr[p, d] for p, d in chains]
        ws = [jnp.dot(wq_scr[p, d, chunk_of[d]], s.astype(BF16), preferred_element_type=F32)
              for (p, d), s in zip(chains, states)]
        new_states = []
        for (p, d), s, w in zip(chains, states, ws):
            n = chunk_of[d]
            r0 = pl.multiple_of(n * CHUNK, CHUNK)
            v_new = (r_scr[p, n, d * CHUNK:(d + 1) * CHUNK, 0:HEAD_DIM] - w[:CHUNK]).astype(BF16)
            v_pad = jnp.concatenate([v_new, zero_half] if d == 0 else [zero_half, v_new], axis=0)
            o = w[CHUNK:] + jnp.dot(qk_scr[p, n], v_pad, preferred_element_type=F32)
            if d == 0:
                o_ref[pl.ds(r0, CHUNK), p * HEAD_DIM:(p + 1) * HEAD_DIM] = o
            else:
                ob_scr[p, pl.ds(r0, CHUNK), :] = o
            gc_row2 = rows_ref[p, n]
            g_last = gc_row2[:, CHUNK - 1:CHUNK] if d == 0 else gc_row2[:, CHUNK:CHUNK + 1]
            new_states.append(s * jnp.exp(g_last) + jnp.dot(kdt_scr[p, n], v_pad, preferred_element_type=F32))
        for (p, d), s in zip(chains, new_states):
            s_scr[p, d] = s
        return carry

    lax.fori_loop(0, n_chunks, recurrence, 0)

    def head_norm(n, carry):
        r0 = pl.multiple_of(n * CHUNK, CHUNK)
        for p in range(hpb):
            lanes = slice(p * HEAD_DIM, (p + 1) * HEAD_DIM)
            o = o_ref[pl.ds(r0, CHUNK), lanes] + ob_scr[p, pl.ds(r0, CHUNK), :]
            ms = jnp.mean(o * o, axis=-1, keepdims=True)
            o_ref[pl.ds(r0, CHUNK), lanes] = o * lax.rsqrt(ms + RMS_EPS) * hg_ref[...]
        return carry

    lax.fori_loop(0, n_chunks, head_norm, 0, unroll=2)


def _delta(proj3, conv_w, cols, rows, hg_rows, layer, n_heads):
    bsz, seq, _ = proj3.shape
    n_chunks = seq // CHUNK
    hpb = DELTA_HEADS_PER_PROGRAM
    width = hpb * HEAD_DIM
    n_blocks = n_heads // hpb
    tok = lambda off: pl.BlockSpec((None, seq, width), lambda b, h: (b, 0, off + h))
    cw = lambda off: pl.BlockSpec((None, CONV_K, width), lambda b, h: (layer, 0, off + h))
    return pl.pallas_call(
        _delta_kernel,
        grid=(bsz, n_blocks),
        in_specs=[
            tok(0), tok(n_blocks), tok(2 * n_blocks),
            cw(0), cw(n_blocks), cw(2 * n_blocks),
            pl.BlockSpec((None, None, seq, 4 * hpb), lambda b, h: (b, h, 0, 0)),
            pl.BlockSpec((None, hpb, n_chunks, 1, 2 * CHUNK), lambda b, h: (b, h, 0, 0, 0)),
            pl.BlockSpec((None, 1, HEAD_DIM), lambda b, h: (layer, 0, 0)),
        ],
        out_specs=pl.BlockSpec((None, seq, width), lambda b, h: (b, 0, h)),
        out_shape=jax.ShapeDtypeStruct((bsz, seq, n_heads * HEAD_DIM), F32),
        scratch_shapes=[
            pltpu.VMEM((3, seq + 2 * SUBLANE, HEAD_DIM), F32),
            pltpu.VMEM((hpb, n_chunks, 2 * CHUNK, 2 * HEAD_DIM), F32),
            pltpu.VMEM((hpb, n_chunks, 2 * CHUNK, 2 * CHUNK), BF16),
            pltpu.VMEM((hpb, 2, n_chunks, 2 * CHUNK, HEAD_DIM), BF16),
            pltpu.VMEM((hpb, n_chunks, HEAD_DIM, 2 * CHUNK), BF16),
            pltpu.VMEM((hpb, n_chunks, CHUNK, 2 * CHUNK), BF16),
            pltpu.VMEM((hpb, seq, HEAD_DIM), F32),
            pltpu.VMEM((hpb, 2, HEAD_DIM, HEAD_DIM), F32),
        ],
        compiler_params=_compiler_params(("parallel", "parallel")),
        name="delta",
    )(proj3, proj3, proj3, conv_w, conv_w, conv_w, cols, rows, hg_rows)


def _s5_kernel(u_ref, m_ref, wsum_ref, wout_ref, a_ref, y_ref, ut_scr, s_scr, yt_scr, *, bsz):
    sb, ch = S5_BLOCK, S5_GROUP_CH
    rows = u_ref.shape[0] // sb
    groups = LANE // ch
    w = 2 * S5_STATE
    blk = sb * ch

    for j in range(sb):
        t = u_ref[pl.ds(j, rows, stride=sb), :].T
        for g in range(groups):
            ut_scr[g, j * ch:(j + 1) * ch, :] = t[g * ch:(g + 1) * ch, :]

    tiles_per_seq = rows // bsz // SUBLANE
    row = lax.broadcasted_iota(jnp.int32, (SUBLANE, w), 0)
    in_slot = [row == k for k in range(SUBLANE)]

    def scan_tile(r0, lane0, a_re, a_im, x_re, x_im, forward):
        s_re, s_im = s_scr[pl.ds(r0, SUBLANE), lane0:lane0 + w], s_scr[pl.ds(r0, SUBLANE), lane0 + w:lane0 + 2 * w]
        shift = 1 if forward else SUBLANE - 1
        in_re, in_im = jnp.zeros((SUBLANE, w), F32), jnp.zeros((SUBLANE, w), F32)
        for k in range(SUBLANE):
            slot = in_slot[k] if forward else in_slot[SUBLANE - 1 - k]
            in_re, in_im = jnp.where(slot, x_re, in_re), jnp.where(slot, x_im, in_im)
            y_re = a_re * x_re - a_im * x_im + s_re
            y_im = a_re * x_im + a_im * x_re + s_im
            x_re, x_im = pltpu.roll(y_re, shift, axis=0), pltpu.roll(y_im, shift, axis=0)
        s_scr[pl.ds(r0, SUBLANE), lane0:lane0 + w] = in_re
        s_scr[pl.ds(r0, SUBLANE), lane0 + w:lane0 + 2 * w] = in_im
        return x_re, x_im

    for q in range(groups // 2):
        u = [ut_scr[2 * q + g].T.astype(BF16) for g in range(2)]
        s_scr[...] = jnp.dot(jnp.concatenate(u, axis=1), wsum_ref[q], preferred_element_type=F32)
        a = a_ref[q]
        coef = [jnp.broadcast_to(a[:, i * w:(i + 1) * w], (SUBLANE, w)) for i in range(4)]

        def step(i, xs, coef=coef):
            out = ()
            for b in range(bsz):
                rf = pl.multiple_of((b * tiles_per_seq + i) * SUBLANE, SUBLANE)
                rb = pl.multiple_of((b * tiles_per_seq + tiles_per_seq - 1 - i) * SUBLANE, SUBLANE)
                out += scan_tile(rf, 0, coef[0], coef[1], xs[4 * b], xs[4 * b + 1], True)
                out += scan_tile(rb, 2 * w, coef[2], coef[3], xs[4 * b + 2], xs[4 * b + 3], False)
            return out

        zero = jnp.zeros((SUBLANE, w), F32)
        lax.fori_loop(0, tiles_per_seq, step, (zero,) * (4 * bsz))

        carried = jnp.dot(s_scr[...].astype(BF16), wout_ref[q], preferred_element_type=F32)
        for g in range(2):
            y = jnp.dot(u[g], m_ref[2 * q + g], preferred_element_type=F32) + carried[:, g * blk:(g + 1) * blk]
            y_t = y.T
            for i in range(sb):
                yt_scr[i, (2 * q + g) * ch:(2 * q + g + 1) * ch, :] = y_t[i * ch:(i + 1) * ch, :]

    for i in range(sb):
        y_ref[pl.ds(i, rows, stride=sb), :] = yt_scr[i].T


def _s5(proj, u_block0, m_mat, w_sum, w_out, a_blk, layer, bsz):
    t = proj.shape[0]
    rows = t // S5_BLOCK
    blk = S5_BLOCK * S5_GROUP_CH
    groups = LANE // S5_GROUP_CH
    n_tiles = m_mat.shape[1] // groups
    return pl.pallas_call(
        functools.partial(_s5_kernel, bsz=bsz),
        grid=(n_tiles,),
        in_specs=[
            pl.BlockSpec((t, LANE), lambda k: (0, u_block0 + k)),
            pl.BlockSpec((None, groups, blk, blk), lambda k: (layer, k, 0, 0)),
            pl.BlockSpec((None, groups // 2, 2 * blk, 8 * S5_STATE), lambda k: (layer, k, 0, 0)),
            pl.BlockSpec((None, groups // 2, 8 * S5_STATE, 2 * blk), lambda k: (layer, k, 0, 0)),
            pl.BlockSpec((None, groups // 2, 1, 8 * S5_STATE), lambda k: (layer, k, 0, 0)),
        ],
        out_specs=pl.BlockSpec((t, LANE), lambda k: (0, k)),
        out_shape=jax.ShapeDtypeStruct((t, n_tiles * LANE), F32),
        scratch_shapes=[
            pltpu.VMEM((groups, blk, rows), F32),
            pltpu.VMEM((rows, 8 * S5_STATE), F32),
            pltpu.VMEM((S5_BLOCK, LANE, rows), F32),
        ],
        compiler_params=_compiler_params(("parallel",)),
        name="s5",
    )(proj, m_mat, w_sum, w_out, a_blk)


def _s5_params(lam_re, lam_im, log_dt, b_re, b_im, c_re, c_im):
    n_groups = lam_re.shape[1]
    sb, p, c = S5_BLOCK, S5_STATE, S5_GROUP_CH
    l_re, l_im = lam_re.astype(F32), lam_im.astype(F32)
    dt = jnp.exp(log_dt.astype(F32))[..., None]
    mag = jnp.exp(l_re * dt)
    bar_re, bar_im = mag * jnp.cos(l_im * dt), mag * jnp.sin(l_im * dt)
    n_re, n_im = bar_re - 1.0, bar_im
    den = l_re * l_re + l_im * l_im
    k_re = ((n_re * l_re + n_im * l_im) / den)[..., None]
    k_im = ((n_im * l_re - n_re * l_im) / den)[..., None]
    bb_re = k_re * b_re.astype(F32) - k_im * b_im.astype(F32)
    bb_im = k_re * b_im.astype(F32) + k_im * b_re.astype(F32)
    cc_re, cc_im = c_re.astype(F32), c_im.astype(F32)

    steps = jnp.arange(sb + 1, dtype=F32)[:, None, None, None]
    mag_l = jnp.exp(steps * (l_re * dt)[None])
    pw_re, pw_im = mag_l * jnp.cos(steps * (l_im * dt)[None]), mag_l * jnp.sin(steps * (l_im * dt)[None])

    cp_re = cc_re[None] * pw_re[:sb, :, :, None, :] - cc_im[None] * pw_im[:sb, :, :, None, :]
    cp_im = cc_re[None] * pw_im[:sb, :, :, None, :] + cc_im[None] * pw_re[:sb, :, :, None, :]
    kern = jnp.einsum('ldgop,dgpi->ldgoi', cp_re, bb_re) - jnp.einsum('ldgop,dgpi->ldgoi', cp_im, bb_im)
    kern_t = jnp.swapaxes(kern, 3, 4)
    idx = jnp.arange(sb)
    lag = idx[None, :] - idx[:, None]
    m5 = jnp.zeros((n_groups, sb, c, sb, c), F32)
    for l in range(sb):
        at_f = (lag == l).astype(F32)[None, :, None, :, None]
        at_b = (lag == -l).astype(F32)[None, :, None, :, None]
        m5 = m5 + at_f * kern_t[l, 0][:, None, :, None, :] + at_b * kern_t[l, 1][:, None, :, None, :]
    m_mat = m5.reshape(n_groups, sb * c, sb * c)

    def summary(p_re, p_im, d):
        s_re = p_re[:, :, :, None] * bb_re[d][None] - p_im[:, :, :, None] * bb_im[d][None]
        s_im = p_re[:, :, :, None] * bb_im[d][None] + p_im[:, :, :, None] * bb_re[d][None]
        lay = lambda t: jnp.transpose(t, (1, 0, 3, 2)).reshape(n_groups, sb * c, p)
        return lay(s_re), lay(s_im)
    sums = (summary(jnp.flip(pw_re[:sb, 0], 0), jnp.flip(pw_im[:sb, 0], 0), 0)
            + summary(pw_re[:sb, 1], pw_im[:sb, 1], 1))

    def readout(p_re, p_im, d):
        z_re = cc_re[d][None] * p_re[:, :, None, :] - cc_im[d][None] * p_im[:, :, None, :]
        z_im = cc_re[d][None] * p_im[:, :, None, :] + cc_im[d][None] * p_re[:, :, None, :]
        lay = lambda t: jnp.transpose(t, (1, 3, 0, 2)).reshape(n_groups, p, sb * c)
        return lay(z_re), lay(-z_im)
    outs = (readout(pw_re[1:, 0], pw_im[1:, 0], 0)
            + readout(jnp.flip(pw_re[1:, 1], 0), jnp.flip(pw_im[1:, 1], 0), 1))

    same = jnp.eye(2, dtype=F32)
    def pair_sum(t):
        t2 = t.reshape(n_groups // 2, 2, sb * c, 1, p) * same[None, :, None, :, None]
        return t2.reshape(n_groups // 2, 2 * sb * c, 2 * p)
    def pair_out(t):
        t2 = t.reshape(n_groups // 2, 2, p, 1, sb * c) * same[None, :, None, :, None]
        return t2.reshape(n_groups // 2, 2 * p, 2 * sb * c)
    w_sum = jnp.concatenate([pair_sum(t) for t in sums], axis=-1).astype(BF16)
    w_out = jnp.concatenate([pair_out(t) for t in outs], axis=-2).astype(BF16)
    a_parts = (pw_re[sb, 0], pw_im[sb, 0], pw_re[sb, 1], pw_im[sb, 1])
    a_blk = jnp.concatenate([t.reshape(n_groups // 2, 1, 2 * p) for t in a_parts], axis=-1)
    return m_mat.astype(BF16), w_sum, w_out, a_blk


def _out_kernel(oa_ref, za_ref, ys_ref, u_ref, zb_ref, ga_ref, gb_ref, x_ref,
                wpa_ref, dsk_ref, wglu_ref, bglu_ref, wpb_ref, bga_ref, bgb_ref, wout_ref, fg_ref,
                o_ref, *, final_norm):
    a = (oa_ref[...] * _silu(za_ref[...])).astype(BF16)
    y_a = jnp.dot(a, wpa_ref[...], preferred_element_type=F32)
    y_s = jax.nn.gelu(ys_ref[...] + u_ref[...] * dsk_ref[...])
    glu = jnp.dot(y_s.astype(BF16), wglu_ref[...], preferred_element_type=F32) + bglu_ref[...]
    y_s = y_s * jax.nn.sigmoid(glu)
    y_b = jnp.dot((y_s * _silu(zb_ref[...])).astype(BF16), wpb_ref[...], preferred_element_type=F32)
    merged = (jax.nn.sigmoid(ga_ref[...] + bga_ref[...]) * y_a
              + jax.nn.sigmoid(gb_ref[...] + bgb_ref[...]) * y_b)
    out = x_ref[...] + jnp.dot(merged.astype(BF16), wout_ref[...], preferred_element_type=F32)
    if final_norm:
        ms = jnp.mean(out * out, axis=-1, keepdims=True)
        out = out * lax.rsqrt(ms + RMS_EPS) * fg_ref[...]
    o_ref[...] = out


def _out_stage(o_a, proj, y_s5, x2, w_pa, d_skip, w_glu, b_glu, w_pb, b_gate, w_out, final_g, *,
               layer, za_block, u_block, zb_block, gate_block0, final_norm):
    t, d = x2.shape
    wa = o_a.shape[1]
    wb = y_s5.shape[1]
    tm = _largest_tile(t, 256, SUBLANE)
    row = lambda width, blk: pl.BlockSpec((tm, width), lambda i: (i, blk))
    const = lambda shape, blk=0: pl.BlockSpec((None,) + shape, lambda i: (layer, 0, blk),
                                              pipeline_mode=pl.Buffered(1))
    return pl.pallas_call(
        functools.partial(_out_kernel, final_norm=final_norm),
        grid=(t // tm,),
        in_specs=[
            row(wa, 0), row(wa, za_block), row(wb, 0), row(wb, u_block), row(wb, zb_block),
            row(d, gate_block0), row(d, gate_block0 + 1), row(d, 0),
            const((wa, d)), const((1, wb)), const((wb, wb)), const((1, wb)), const((wb, d)),
            const((1, d), 0), const((1, d), 1), const((d, d)), const((1, d)),
        ],
        out_specs=pl.BlockSpec((tm, d), lambda i: (i, 0)),
        out_shape=jax.ShapeDtypeStruct((t, d), F32),
        compiler_params=_compiler_params(("parallel",)),
        name="out_stage",
    )(o_a, proj, y_s5, proj, proj, proj, proj, x2, w_pa, d_skip, w_glu, b_glu, w_pb, b_gate, b_gate, w_out,
      final_g)


def _prepare_params(ln_g, w_in, conv_w, a_log, dt_bias, head_norm_g, lam_re, lam_im, log_dt, b_re, b_im,
                    c_re, c_im, d_skip, w_glu, b_glu, w_pa, w_pb, b_gate, w_out, final_g):
    depth, d, _ = w_in.shape
    n_heads = a_log.shape[2]
    wa = n_heads * HEAD_DIM
    wb = d_skip.shape[1]
    n_ba = 4 * n_heads
    o_beta = 4 * wa
    o_u = o_beta + n_ba
    w_cols = jnp.concatenate(
        [w_in[:, :, :o_beta], w_in[:, :, o_u:], w_in[:, :, o_beta:o_u],
         jnp.zeros((depth, d, LANE - n_ba), w_in.dtype)], axis=2).astype(BF16)
    pad_rows = lambda p: jnp.pad(p.astype(F32).reshape(depth, 1, n_ba // 2),
                                 ((0, 0), (0, 0), (n_ba // 2, LANE - n_ba)))
    row = lambda p: p.astype(F32).reshape(depth, 1, -1)
    return dict(
        ln_g=row(ln_g), w_cols=w_cols, conv_w=conv_w.astype(F32), alog=pad_rows(a_log), dtb=pad_rows(dt_bias),
        head_norm_g=row(head_norm_g),
        s5=jax.vmap(_s5_params)(lam_re, lam_im, log_dt, b_re, b_im, c_re, c_im),
        d_skip=row(d_skip), w_glu=w_glu.astype(BF16), b_glu=row(b_glu), w_pa=w_pa.astype(BF16),
        w_pb=w_pb.astype(BF16), b_gate=row(b_gate), w_out=w_out.astype(BF16),
        final_g=jnp.broadcast_to(final_g.astype(F32).reshape(1, 1, d), (depth, 1, d)),
        n_heads=n_heads, wa=wa, wb=wb)


def _layer(x2, bsz, seq, prm, layer, final_norm):
    t, d = x2.shape
    n_heads, wa, wb = prm["n_heads"], prm["wa"], prm["wb"]
    assert wa == wb and seq % (CHUNK * SUBLANE) == 0 and 4 * n_heads <= LANE
    c_qkv, c_za, c_beta, c_alpha, c_u, c_zb = 3 * wa, wa, 2 * n_heads, 2 * n_heads, wb, wb
    n_proj = prm["w_cols"].shape[2]
    ba_block = (n_proj - LANE) // LANE

    proj = _inproj(x2, prm["ln_g"], prm["w_cols"], layer)
    proj3 = proj.reshape(bsz, seq, n_proj)

    gates = _gates(proj, ba_block, prm["alog"], prm["dtb"], layer, n_heads)
    beta = gates[:, :c_beta].reshape(bsz, seq, 2, n_heads)
    gc = gates[:, c_beta:c_beta + c_alpha].reshape(bsz, seq, 2, n_heads)
    hpb = DELTA_HEADS_PER_PROGRAM
    cols = jnp.stack([gc[:, :, 0], gc[:, :, 1], beta[:, :, 0], beta[:, :, 1]], axis=-1)
    cols = jnp.transpose(cols.reshape(bsz, seq, n_heads // hpb, 4 * hpb), (0, 2, 1, 3))
    rows = jnp.transpose(gc.reshape(bsz, seq // CHUNK, CHUNK, 2, n_heads), (0, 4, 1, 3, 2))
    rows = rows.reshape(bsz, n_heads, seq // CHUNK, 1, 2 * CHUNK)

    o_a = _delta(proj3, prm["conv_w"], cols, rows, prm["head_norm_g"], layer, n_heads)
    o_a = o_a.reshape(t, wa)

    u_off = c_qkv + c_za
    y_s5 = _s5(proj, u_off // LANE, *prm["s5"], layer, bsz)

    return _out_stage(
        o_a, proj, y_s5, x2, prm["w_pa"], prm["d_skip"], prm["w_glu"], prm["b_glu"], prm["w_pb"],
        prm["b_gate"], prm["w_out"], prm["final_g"], layer=layer,
        za_block=c_qkv // wa, u_block=u_off // wb, zb_block=(u_off + c_u) // wb,
        gate_block0=(u_off + c_u + c_zb) // d, final_norm=final_norm)


def kernel(x, ln_g, w_in, conv_w, a_log, dt_bias, head_norm_g, lam_re, lam_im, log_dt, b_re, b_im, c_re, c_im, d_skip, w_glu, b_glu, w_pa, w_pb, b_gate, w_out, final_g):
    bsz, seq, d = x.shape
    depth = ln_g.shape[0]
    prm = _prepare_params(ln_g, w_in, conv_w, a_log, dt_bias, head_norm_g, lam_re, lam_im, log_dt, b_re, b_im,
                          c_re, c_im, d_skip, w_glu, b_glu, w_pa, w_pb, b_gate, w_out, final_g)
    x2 = x.reshape(bsz * seq, d)
    for layer in range(depth):
        x2 = _layer(x2, bsz, seq, prm, layer, final_norm=(layer == depth - 1))
    return x2.reshape(bsz, seq, d)
```

```python
import functools
import math

import jax
import jax.numpy as jnp
from jax import lax
from jax.experimental import pallas as pl
from jax.experimental.pallas import tpu as pltpu

F32 = jnp.float32
BF16 = jnp.bfloat16

LANE = 128
SUBLANE = 8
HEAD_DIM = 128
CHUNK = 64
CONV_K = 5
CONV_PAD = (CONV_K - 1) // 2
DELTA_HEADS_PER_PROGRAM = 2
LEVEL_GROUP_CHUNKS = 8
S5_GROUP_CH = 16
S5_STATE = 64
S5_BLOCK = 16
RMS_EPS = 1e-6
VMEM_LIMIT_BYTES = 56 * 1024 * 1024


def _compiler_params(semantics):
    return pltpu.CompilerParams(dimension_semantics=semantics, vmem_limit_bytes=VMEM_LIMIT_BYTES)


def _silu(x):
    return x * jax.nn.sigmoid(x)


def _largest_tile(n, cap, unit):
    best = unit
    t = unit
    while t <= min(n, cap):
        if n % t == 0:
            best = t
        t += unit
    return best


def _inproj_kernel(x_ref, g_ref, w_ref, o_ref, h_ref):
    @pl.when(pl.program_id(1) == 0)
    def _():
        x = x_ref[...]
        ms = jnp.mean(x * x, axis=-1, keepdims=True)
        h_ref[...] = (x * lax.rsqrt(ms + RMS_EPS) * g_ref[...]).astype(BF16)

    o_ref[...] = jnp.dot(h_ref[...], w_ref[...], preferred_element_type=F32)


def _inproj(x2, g_rows, w_bf16, layer):
    t, d = x2.shape
    n = w_bf16.shape[2]
    tm = _largest_tile(t, 1024, SUBLANE)
    tn = _largest_tile(n, 1280, LANE)
    return pl.pallas_call(
        _inproj_kernel,
        grid=(t // tm, n // tn),
        in_specs=[
            pl.BlockSpec((tm, d), lambda i, j: (i, 0)),
            pl.BlockSpec((None, 1, d), lambda i, j: (layer, 0, 0)),
            pl.BlockSpec((None, d, tn), lambda i, j: (layer, 0, j)),
        ],
        out_specs=pl.BlockSpec((tm, tn), lambda i, j: (i, j)),
        out_shape=jax.ShapeDtypeStruct((t, n), F32),
        scratch_shapes=[pltpu.VMEM((tm, d), BF16)],
        compiler_params=_compiler_params(("parallel", "arbitrary")),
        name="inproj",
    )(x2, g_rows, w_bf16)


def _gates_kernel(ba_ref, alog_ref, dtb_ref, o_ref, *, n_heads):
    x = ba_ref[...]
    tm = x.shape[0]
    lane = lax.broadcasted_iota(jnp.int32, x.shape, 1)
    beta = jax.nn.sigmoid(x)
    z = x + dtb_ref[...]
    softplus = jnp.maximum(z, 0.0) + jnp.log1p(jnp.exp(-jnp.abs(z)))
    g = -jnp.exp(alog_ref[...]) * softplus
    r = lax.broadcasted_iota(jnp.int32, (tm, tm), 0)
    c = lax.broadcasted_iota(jnp.int32, (tm, tm), 1)
    shift = int(math.log2(CHUNK))
    same_chunk = (r >> shift) == (c >> shift)
    m_prefix = jnp.where(same_chunk & (c <= r), 1.0, 0.0).astype(F32)
    m_suffix = jnp.where(same_chunk & (c >= r), 1.0, 0.0).astype(F32)
    g_prefix = jnp.dot(m_prefix, g, precision=lax.Precision.HIGHEST, preferred_element_type=F32)
    g_suffix = jnp.dot(m_suffix, g, precision=lax.Precision.HIGHEST, preferred_element_type=F32)
    gc = jnp.where(lane < 3 * n_heads, g_prefix, g_suffix)
    o_ref[...] = jnp.where(lane < 2 * n_heads, beta, gc)


def _gates(proj, ba_block, alog_rows, dtb_rows, layer, n_heads):
    t = proj.shape[0]
    tm = _largest_tile(t, 512, CHUNK)
    return pl.pallas_call(
        functools.partial(_gates_kernel, n_heads=n_heads),
        grid=(t // tm,),
        in_specs=[
            pl.BlockSpec((tm, LANE), lambda i: (i, ba_block)),
            pl.BlockSpec((None, 1, LANE), lambda i: (layer, 0, 0)),
            pl.BlockSpec((None, 1, LANE), lambda i: (layer, 0, 0)),
        ],
        out_specs=pl.BlockSpec((tm, LANE), lambda i: (i, 0)),
        out_shape=jax.ShapeDtypeStruct((t, LANE), F32),
        compiler_params=_compiler_params(("parallel",)),
        name="gates",
    )(proj, alog_rows, dtb_rows)


def _dot_nt(a, b):
    return lax.dot_general(a, b, (((1,), (1,)), ((), ())), preferred_element_type=F32)


def _delta_kernel(q_ref, k_ref, v_ref, wq_ref, wk_ref, wv_ref, cols_ref, rows_ref, hg_ref, o_ref,
                  pad_scr, r_scr, m_scr, wq_scr, kdt_scr, qk_scr, ob_scr, s_scr):
    seq = q_ref.shape[0]
    hpb = q_ref.shape[1] // HEAD_DIM
    n_chunks = seq // CHUNK
    halo = SUBLANE
    c2 = 2 * CHUNK
    operands = ((q_ref, wq_ref), (k_ref, wk_ref), (v_ref, wv_ref))
    for j in range(len(operands)):
        pad_scr[j, 0:halo, :] = jnp.zeros((halo, HEAD_DIM), F32)
        pad_scr[j, seq + halo:seq + 2 * halo, :] = jnp.zeros((halo, HEAD_DIM), F32)

    def conv_silu(j, w, r0):
        acc = None
        for i in range(CONV_K):
            term = pad_scr[j, pl.ds(r0 + halo + i - CONV_PAD, CHUNK), :] * w[i:i + 1, :]
            acc = term if acc is None else acc + term
        return _silu(acc)

    def l2norm(y):
        return y * lax.rsqrt(jnp.sum(y * y, axis=-1, keepdims=True) + RMS_EPS)

    ri = lax.broadcasted_iota(jnp.int32, (c2, c2), 0)
    ci = lax.broadcasted_iota(jnp.int32, (c2, c2), 1)
    fwd_block = (ri < CHUNK) & (ci < CHUNK)
    bwd_block = (ri >= CHUNK) & (ci >= CHUNK)
    strict2 = (fwd_block & (ri > ci)) | (bwd_block & (ri < ci))
    rw = lax.broadcasted_iota(jnp.int32, (CHUNK, c2), 0)
    cw = lax.broadcasted_iota(jnp.int32, (CHUNK, c2), 1)
    fwd_lanes = cw < CHUNK
    incl_wide = (fwd_lanes & (rw >= cw)) | (~fwd_lanes & (rw <= cw - CHUNK))
    top_rows = ri < CHUNK

    for p in range(hpb):
        lanes = slice(p * HEAD_DIM, (p + 1) * HEAD_DIM)

        def fill(n, carry, lanes=lanes):
            r0 = pl.multiple_of(n * CHUNK, CHUNK)
            for j, (src_ref, _) in enumerate(operands):
                pad_scr[j, pl.ds(r0 + halo, CHUNK), :] = src_ref[pl.ds(r0, CHUNK), lanes]
            return carry

        lax.fori_loop(0, n_chunks, fill, 0, unroll=2)

        def setup(n, carry, p=p, lanes=lanes):
            r0 = pl.multiple_of(n * CHUNK, CHUNK)
            q = l2norm(conv_silu(0, wq_ref[:, lanes], r0)) * (HEAD_DIM ** -0.5)
            k = l2norm(conv_silu(1, wk_ref[:, lanes], r0))
            v = conv_silu(2, wv_ref[:, lanes], r0)
            k16 = k.astype(BF16)
            k2_16 = jnp.concatenate([k16, k16], axis=0)
            kk2 = _dot_nt(k2_16, k2_16)
            qk_wide = _dot_nt(q.astype(BF16), k2_16)
            cols = cols_ref[pl.ds(r0, CHUNK), :]
            across = lambda c: jnp.broadcast_to(cols[:, 4 * p + c:4 * p + c + 1], (CHUNK, c2))
            gc2 = jnp.concatenate([across(0), across(1)], axis=0)
            be2 = jnp.concatenate([across(2), across(3)], axis=0)
            gc_row2 = rows_ref[p, n]
            k2 = jnp.concatenate([k, k], axis=0)
            v2 = jnp.concatenate([v, v], axis=0)
            q2 = jnp.concatenate([q, q], axis=0)
            decay2 = jnp.where(strict2, jnp.exp(gc2 - gc_row2), 0.0)
            m_scr[p, n] = (-(be2 * kk2 * decay2)).astype(BF16)
            eg2 = jnp.exp(gc2)
            r_scr[p, n] = jnp.concatenate([v2 * be2, k2 * (be2 * eg2)], axis=1)
            qg2 = (q2 * eg2).astype(BF16)
            wq_scr[p, 0, n, CHUNK:c2, :] = qg2[:CHUNK]
            wq_scr[p, 1, n, CHUNK:c2, :] = qg2[CHUNK:]
            g_last2 = jnp.where(top_rows, gc_row2[:, CHUNK - 1:CHUNK], gc_row2[:, CHUNK:CHUNK + 1])
            kdec2 = k2 * jnp.exp(g_last2 - gc2)
            kdt_scr[p, n] = kdec2.T.astype(BF16)
            gc_wide = jnp.where(fwd_lanes, gc2[:CHUNK], gc2[CHUNK:])
            qk_scr[p, n] = (qk_wide * jnp.where(incl_wide, jnp.exp(gc_wide - gc_row2), 0.0)).astype(BF16)
            return carry

        lax.fori_loop(0, n_chunks, setup, 0, unroll=2)

    group = _largest_tile(n_chunks, LEVEL_GROUP_CHUNKS, 1)

    def level(_, carry):
        def per_group(it, carry):
            tiles = [(p, it * group + g) for g in range(group) for p in range(hpb)]
            outs = []
            for p, n in tiles:
                m = m_scr[p, n]
                rhs = jnp.concatenate([r_scr[p, n].astype(BF16), m], axis=1)
                outs.append(jnp.dot(m, rhs, preferred_element_type=F32))
            for (p, n), out in zip(tiles, outs):
                r_scr[p, n] += out[:, :2 * HEAD_DIM]
                m_scr[p, n] = out[:, 2 * HEAD_DIM:].astype(BF16)
            return carry

        return lax.fori_loop(0, n_chunks // group, per_group, carry)

    lax.fori_loop(0, int(math.log2(CHUNK)), level, 0)

    def finish(n, carry):
        for p in range(hpb):
            w2 = r_scr[p, n, :, HEAD_DIM:2 * HEAD_DIM].astype(BF16)
            wq_scr[p, 0, n, 0:CHUNK, :] = w2[:CHUNK]
            wq_scr[p, 1, n, 0:CHUNK, :] = w2[CHUNK:]
        return carry

    lax.fori_loop(0, n_chunks, finish, 0, unroll=4)

    s_scr[...] = jnp.zeros(s_scr.shape, F32)
    zero_half = jnp.zeros((CHUNK, HEAD_DIM), BF16)
    chains = [(p, d) for p in range(hpb) for d in range(2)]

    def recurrence(i, carry):
        chunk_of = (i, n_chunks - 1 - i)
        states = [s_scr[p, d] for p, d in chains]
        ws = [jnp.dot(wq_scr[p, d, chunk_of[d]], s.astype(BF16), preferred_element_type=F32)
              for (p, d), s in zip(chains, states)]
        new_states = []
        for (p, d), s, w in zip(chains, states, ws):
            n = chunk_of[d]
            r0 = pl.multiple_of(n * CHUNK, CHUNK)
            v_new = (r_scr[p, n, d * CHUNK:(d + 1) * CHUNK, 0:HEAD_DIM] - w[:CHUNK]).astype(BF16)
            v_pad = jnp.concatenate([v_new, zero_half] if d == 0 else [zero_half, v_new], axis=0)
            o = w[CHUNK:] + jnp.dot(qk_scr[p, n], v_pad, preferred_element_type=F32)
            if d == 0:
                o_ref[pl.ds(r0, CHUNK), p * HEAD_DIM:(p + 1) * HEAD_DIM] = o
            else:
                ob_scr[p, pl.ds(r0, CHUNK), :] = o
            gc_row2 = rows_ref[p, n]
            g_last = gc_row2[:, CHUNK - 1:CHUNK] if d == 0 else gc_row2[:, CHUNK:CHUNK + 1]
            new_states.append(s * jnp.exp(g_last) + jnp.dot(kdt_scr[p, n], v_pad, preferred_element_type=F32))
        for (p, d), s in zip(chains, new_states):
            s_scr[p, d] = s
        return carry

    lax.fori_loop(0, n_chunks, recurrence, 0)

    def head_norm(n, carry):
        r0 = pl.multiple_of(n * CHUNK, CHUNK)
        for p in range(hpb):
            lanes = slice(p * HEAD_DIM, (p + 1) * HEAD_DIM)
            o = o_ref[pl.ds(r0, CHUNK), lanes] + ob_scr[p, pl.ds(r0, CHUNK), :]
            ms = jnp.mean(o * o, axis=-1, keepdims=True)
            o_ref[pl.ds(r0, CHUNK), lanes] = o * lax.rsqrt(ms + RMS_EPS) * hg_ref[...]
        return carry

    lax.fori_loop(0, n_chunks, head_norm, 0, unroll=2)


def _delta(proj3, conv_w, cols, rows, hg_rows, layer, n_heads):
    bsz, seq, _ = proj3.shape
    n_chunks = seq // CHUNK
    hpb = DELTA_HEADS_PER_PROGRAM
    width = hpb * HEAD_DIM
    n_blocks = n_heads // hpb
    tok = lambda off: pl.BlockSpec((None, seq, width), lambda b, h: (b, 0, off + h))
    cw = lambda off: pl.BlockSpec((None, CONV_K, width), lambda b, h: (layer, 0, off + h))
    return pl.pallas_call(
        _delta_kernel,
        grid=(bsz, n_blocks),
        in_specs=[
            tok(0), tok(n_blocks), tok(2 * n_blocks),
            cw(0), cw(n_blocks), cw(2 * n_blocks),
            pl.BlockSpec((None, None, seq, 4 * hpb), lambda b, h: (b, h, 0, 0)),
            pl.BlockSpec((None, hpb, n_chunks, 1, 2 * CHUNK), lambda b, h: (b, h, 0, 0, 0)),
            pl.BlockSpec((None, 1, HEAD_DIM), lambda b, h: (layer, 0, 0)),
        ],
        out_specs=pl.BlockSpec((None, seq, width), lambda b, h: (b, 0, h)),
        out_shape=jax.ShapeDtypeStruct((bsz, seq, n_heads * HEAD_DIM), F32),
        scratch_shapes=[
            pltpu.VMEM((3, seq + 2 * SUBLANE, HEAD_DIM), F32),
            pltpu.VMEM((hpb, n_chunks, 2 * CHUNK, 2 * HEAD_DIM), F32),
            pltpu.VMEM((hpb, n_chunks, 2 * CHUNK, 2 * CHUNK), BF16),
            pltpu.VMEM((hpb, 2, n_chunks, 2 * CHUNK, HEAD_DIM), BF16),
            pltpu.VMEM((hpb, n_chunks, HEAD_DIM, 2 * CHUNK), BF16),
            pltpu.VMEM((hpb, n_chunks, CHUNK, 2 * CHUNK), BF16),
            pltpu.VMEM((hpb, seq, HEAD_DIM), F32),
            pltpu.VMEM((hpb, 2, HEAD_DIM, HEAD_DIM), F32),
        ],
        compiler_params=_compiler_params(("parallel", "parallel")),
        name="delta",
    )(proj3, proj3, proj3, conv_w, conv_w, conv_w, cols, rows, hg_rows)


def _s5_kernel(u_ref, kf_ref, kb_ref, bt_ref, ct_ref, pw_ref, y_ref,
               ut_scr, s_scr, yt_scr, m_scr, wsum_scr, woutt_scr, *, bsz):
    sb, ch = S5_BLOCK, S5_GROUP_CH
    rows = u_ref.shape[0] // sb
    groups = LANE // ch
    w = 2 * S5_STATE
    blk = sb * ch

    lane = lax.broadcasted_iota(jnp.int32, (ch, blk), 1)
    for g in range(groups):
        k_f, k_b = kf_ref[g], kb_ref[g]
        for j in range(sb):
            fwd = jnp.where(lane >= j * ch, pltpu.roll(k_f, j * ch, axis=1), 0.0)
            bwd = jnp.where(lane < (j + 1) * ch, pltpu.roll(k_b, (blk - (sb - 1 - j) * ch) % blk, axis=1), 0.0)
            m_scr[g, j * ch:(j + 1) * ch, :] = (fwd + bwd).astype(BF16)

    def cmul(a_re, a_im, b_re, b_im):
        return a_re * b_re - a_im * b_im, a_re * b_im + a_im * b_re

    for j in range(sb):
        t = u_ref[pl.ds(j, rows, stride=sb), :].T
        for g in range(groups):
            ut_scr[g, j * ch:(j + 1) * ch, :] = t[g * ch:(g + 1) * ch, :]

    tiles_per_seq = rows // bsz // SUBLANE
    row = lax.broadcasted_iota(jnp.int32, (SUBLANE, w), 0)
    in_slot = [row == k for k in range(SUBLANE)]

    def scan_tile(r0, lane0, a_re, a_im, x_re, x_im, forward):
        s_re, s_im = s_scr[pl.ds(r0, SUBLANE), lane0:lane0 + w], s_scr[pl.ds(r0, SUBLANE), lane0 + w:lane0 + 2 * w]
        shift = 1 if forward else SUBLANE - 1
        in_re, in_im = jnp.zeros((SUBLANE, w), F32), jnp.zeros((SUBLANE, w), F32)
        for k in range(SUBLANE):
            slot = in_slot[k] if forward else in_slot[SUBLANE - 1 - k]
            in_re, in_im = jnp.where(slot, x_re, in_re), jnp.where(slot, x_im, in_im)
            y_re = a_re * x_re - a_im * x_im + s_re
            y_im = a_re * x_im + a_im * x_re + s_im
            x_re, x_im = pltpu.roll(y_re, shift, axis=0), pltpu.roll(y_im, shift, axis=0)
        s_scr[pl.ds(r0, SUBLANE), lane0:lane0 + w] = in_re
        s_scr[pl.ds(r0, SUBLANE), lane0 + w:lane0 + 2 * w] = in_im
        return x_re, x_im

    for q in range(groups // 2):
        pw = [[pw_ref[d, part, q] for part in range(2)] for d in range(2)]
        for h in range(2):
            bt = [[bt_ref[d, part, q, h] for part in range(2)] for d in range(2)]
            ct = [[ct_ref[d, part, q, h] for part in range(2)] for d in range(2)]
            for j in range(sb):
                r0 = h * blk + j * ch
                for d, (l_sum, l_out) in enumerate(((sb - 1 - j, j + 1), (j, sb - j))):
                    s_re, s_im = cmul(bt[d][0], bt[d][1], pw[d][0][l_sum:l_sum + 1], pw[d][1][l_sum:l_sum + 1])
                    z_re, z_im = cmul(ct[d][0], ct[d][1], pw[d][0][l_out:l_out + 1], pw[d][1][l_out:l_out + 1])
                    wsum_scr[r0:r0 + ch, 2 * d * w:(2 * d + 1) * w] = s_re.astype(BF16)
                    wsum_scr[r0:r0 + ch, (2 * d + 1) * w:(2 * d + 2) * w] = s_im.astype(BF16)
                    woutt_scr[r0:r0 + ch, 2 * d * w:(2 * d + 1) * w] = z_re.astype(BF16)
                    woutt_scr[r0:r0 + ch, (2 * d + 1) * w:(2 * d + 2) * w] = (-z_im).astype(BF16)

        u = [ut_scr[2 * q + g].T.astype(BF16) for g in range(2)]
        s_scr[...] = jnp.dot(jnp.concatenate(u, axis=1), wsum_scr[...], preferred_element_type=F32)
        coef = [jnp.broadcast_to(pw[d][part][sb:sb + 1], (SUBLANE, w)) for d in range(2) for part in range(2)]

        def step(i, xs, coef=coef):
            out = ()
            for b in range(bsz):
                rf = pl.multiple_of((b * tiles_per_seq + i) * SUBLANE, SUBLANE)
                rb = pl.multiple_of((b * tiles_per_seq + tiles_per_seq - 1 - i) * SUBLANE, SUBLANE)
                out += scan_tile(rf, 0, coef[0], coef[1], xs[4 * b], xs[4 * b + 1], True)
                out += scan_tile(rb, 2 * w, coef[2], coef[3], xs[4 * b + 2], xs[4 * b + 3], False)
            return out

        zero = jnp.zeros((SUBLANE, w), F32)
        lax.fori_loop(0, tiles_per_seq, step, (zero,) * (4 * bsz))

        carried = _dot_nt(s_scr[...].astype(BF16), woutt_scr[...])
        for g in range(2):
            y = jnp.dot(u[g], m_scr[2 * q + g], preferred_element_type=F32) + carried[:, g * blk:(g + 1) * blk]
            y_t = y.T
            for i in range(sb):
                yt_scr[i, (2 * q + g) * ch:(2 * q + g + 1) * ch, :] = y_t[i * ch:(i + 1) * ch, :]

    for i in range(sb):
        y_ref[pl.ds(i, rows, stride=sb), :] = yt_scr[i].T


def _s5(proj, u_block0, k_f, k_b, b_t, c_t, powers, layer, bsz):
    t = proj.shape[0]
    rows = t // S5_BLOCK
    blk = S5_BLOCK * S5_GROUP_CH
    groups = LANE // S5_GROUP_CH
    pairs = groups // 2
    w = 2 * S5_STATE
    n_tiles = k_f.shape[1] // groups
    strip = pl.BlockSpec((None, groups, S5_GROUP_CH, blk), lambda k: (layer, k, 0, 0))
    placed = pl.BlockSpec((None, 2, 2, pairs, 2, S5_GROUP_CH, w), lambda k: (layer, 0, 0, k, 0, 0, 0))
    return pl.pallas_call(
        functools.partial(_s5_kernel, bsz=bsz),
        grid=(n_tiles,),
        in_specs=[
            pl.BlockSpec((t, LANE), lambda k: (0, u_block0 + k)),
            strip, strip, placed, placed,
            pl.BlockSpec((None, 2, 2, pairs, S5_BLOCK + 1, w), lambda k: (layer, 0, 0, k, 0, 0)),
        ],
        out_specs=pl.BlockSpec((t, LANE), lambda k: (0, k)),
        out_shape=jax.ShapeDtypeStruct((t, n_tiles * LANE), F32),
        scratch_shapes=[
            pltpu.VMEM((groups, blk, rows), F32),
            pltpu.VMEM((rows, 4 * w), F32),
            pltpu.VMEM((S5_BLOCK, LANE, rows), F32),
            pltpu.VMEM((groups, blk, blk), BF16),
            pltpu.VMEM((2 * blk, 4 * w), BF16),
            pltpu.VMEM((2 * blk, 4 * w), BF16),
        ],
        compiler_params=_compiler_params(("parallel",)),
        name="s5",
    )(proj, k_f, k_b, b_t, c_t, powers)


def _s5_params(lam_re, lam_im, log_dt, b_re, b_im, c_re, c_im):
    n_groups = lam_re.shape[1]
    sb, p, c = S5_BLOCK, S5_STATE, S5_GROUP_CH
    l_re, l_im = lam_re.astype(F32), lam_im.astype(F32)
    dt = jnp.exp(log_dt.astype(F32))[..., None]
    mag = jnp.exp(l_re * dt)
    bar_re, bar_im = mag * jnp.cos(l_im * dt), mag * jnp.sin(l_im * dt)
    n_re, n_im = bar_re - 1.0, bar_im
    den = l_re * l_re + l_im * l_im
    k_re = ((n_re * l_re + n_im * l_im) / den)[..., None]
    k_im = ((n_im * l_re - n_re * l_im) / den)[..., None]
    bb_re = k_re * b_re.astype(F32) - k_im * b_im.astype(F32)
    bb_im = k_re * b_im.astype(F32) + k_im * b_re.astype(F32)
    cc_re, cc_im = c_re.astype(F32), c_im.astype(F32)

    steps = jnp.arange(sb + 1, dtype=F32)[:, None, None, None]
    mag_l = jnp.exp(steps * (l_re * dt)[None])
    pw_re, pw_im = mag_l * jnp.cos(steps * (l_im * dt)[None]), mag_l * jnp.sin(steps * (l_im * dt)[None])

    cp_re = cc_re[None] * pw_re[:sb, :, :, None, :] - cc_im[None] * pw_im[:sb, :, :, None, :]
    cp_im = cc_re[None] * pw_im[:sb, :, :, None, :] + cc_im[None] * pw_re[:sb, :, :, None, :]
    kern = jnp.einsum('ldgop,dgpi->ldgoi', cp_re, bb_re) - jnp.einsum('ldgop,dgpi->ldgoi', cp_im, bb_im)
    k_f = jnp.transpose(kern[:, 0], (1, 3, 0, 2)).reshape(n_groups, c, sb * c)
    k_b = jnp.transpose(jnp.flip(kern[:, 1], 0), (1, 3, 0, 2)).reshape(n_groups, c, sb * c)

    same = jnp.eye(2, dtype=F32)
    def place(t):
        t2 = t.reshape(2, n_groups // 2, 2, c, 1, p) * same[None, None, :, None, :, None]
        return t2.reshape(2, n_groups // 2, 2, c, 2 * p)
    b_t = jnp.stack([place(jnp.swapaxes(bb_re, 2, 3)), place(jnp.swapaxes(bb_im, 2, 3))], axis=1)
    c_t = jnp.stack([place(cc_re), place(cc_im)], axis=1)
    def side_by_side(t):
        t2 = jnp.transpose(t.reshape(sb + 1, 2, n_groups // 2, 2, p), (1, 2, 0, 3, 4))
        return t2.reshape(2, n_groups // 2, sb + 1, 2 * p)
    powers = jnp.stack([side_by_side(pw_re), side_by_side(pw_im)], axis=1)
    return k_f, k_b, b_t, c_t, powers


def _out_kernel(oa_ref, za_ref, ys_ref, u_ref, zb_ref, ga_ref, gb_ref, x_ref,
                wpa_ref, dsk_ref, wglu_ref, bglu_ref, wpb_ref, bga_ref, bgb_ref, wout_ref, fg_ref,
                o_ref, *, final_norm):
    a = (oa_ref[...] * _silu(za_ref[...])).astype(BF16)
    y_a = jnp.dot(a, wpa_ref[...], preferred_element_type=F32)
    y_s = jax.nn.gelu(ys_ref[...] + u_ref[...] * dsk_ref[...])
    glu = jnp.dot(y_s.astype(BF16), wglu_ref[...], preferred_element_type=F32) + bglu_ref[...]
    y_s = y_s * jax.nn.sigmoid(glu)
    y_b = jnp.dot((y_s * _silu(zb_ref[...])).astype(BF16), wpb_ref[...], preferred_element_type=F32)
    merged = (jax.nn.sigmoid(ga_ref[...] + bga_ref[...]) * y_a
              + jax.nn.sigmoid(gb_ref[...] + bgb_ref[...]) * y_b)
    out = x_ref[...] + jnp.dot(merged.astype(BF16), wout_ref[...], preferred_element_type=F32)
    if final_norm:
        ms = jnp.mean(out * out, axis=-1, keepdims=True)
        out = out * lax.rsqrt(ms + RMS_EPS) * fg_ref[...]
    o_ref[...] = out


def _out_stage(o_a, proj, y_s5, x2, w_pa, d_skip, w_glu, b_glu, w_pb, b_gate, w_out, final_g, *,
               layer, za_block, u_block, zb_block, gate_block0, final_norm):
    t, d = x2.shape
    wa = o_a.shape[1]
    wb = y_s5.shape[1]
    tm = _largest_tile(t, 256, SUBLANE)
    row = lambda width, blk: pl.BlockSpec((tm, width), lambda i: (i, blk))
    const = lambda shape, blk=0: pl.BlockSpec((None,) + shape, lambda i: (layer, 0, blk),
                                              pipeline_mode=pl.Buffered(1))
    return pl.pallas_call(
        functools.partial(_out_kernel, final_norm=final_norm),
        grid=(t // tm,),
        in_specs=[
            row(wa, 0), row(wa, za_block), row(wb, 0), row(wb, u_block), row(wb, zb_block),
            row(d, gate_block0), row(d, gate_block0 + 1), row(d, 0),
            const((wa, d)), const((1, wb)), const((wb, wb)), const((1, wb)), const((wb, d)),
            const((1, d), 0), const((1, d), 1), const((d, d)), const((1, d)),
        ],
        out_specs=pl.BlockSpec((tm, d), lambda i: (i, 0)),
        out_shape=jax.ShapeDtypeStruct((t, d), F32),
        compiler_params=_compiler_params(("parallel",)),
        name="out_stage",
    )(o_a, proj, y_s5, proj, proj, proj, proj, x2, w_pa, d_skip, w_glu, b_glu, w_pb, b_gate, b_gate, w_out,
      final_g)


def _prepare_params(ln_g, w_in, conv_w, a_log, dt_bias, head_norm_g, lam_re, lam_im, log_dt, b_re, b_im,
                    c_re, c_im, d_skip, w_glu, b_glu, w_pa, w_pb, b_gate, w_out, final_g):
    depth, d, _ = w_in.shape
    n_heads = a_log.shape[2]
    wa = n_heads * HEAD_DIM
    wb = d_skip.shape[1]
    n_ba = 4 * n_heads
    o_beta = 4 * wa
    o_u = o_beta + n_ba
    w_cols = jnp.concatenate(
        [w_in[:, :, :o_beta], w_in[:, :, o_u:], w_in[:, :, o_beta:o_u],
         jnp.zeros((depth, d, LANE - n_ba), w_in.dtype)], axis=2).astype(BF16)
    pad_rows = lambda p: jnp.pad(p.astype(F32).reshape(depth, 1, n_ba // 2),
                                 ((0, 0), (0, 0), (n_ba // 2, LANE - n_ba)))
    row = lambda p: p.astype(F32).reshape(depth, 1, -1)
    return dict(
        ln_g=row(ln_g), w_cols=w_cols, conv_w=conv_w.astype(F32), alog=pad_rows(a_log), dtb=pad_rows(dt_bias),
        head_norm_g=row(head_norm_g),
        s5=jax.vmap(_s5_params)(lam_re, lam_im, log_dt, b_re, b_im, c_re, c_im),
        d_skip=row(d_skip), w_glu=w_glu.astype(BF16), b_glu=row(b_glu), w_pa=w_pa.astype(BF16),
        w_pb=w_pb.astype(BF16), b_gate=row(b_gate), w_out=w_out.astype(BF16),
        final_g=jnp.broadcast_to(final_g.astype(F32).reshape(1, 1, d), (depth, 1, d)),
        n_heads=n_heads, wa=wa, wb=wb)


def _layer(x2, bsz, seq, prm, layer, final_norm):
    t, d = x2.shape
    n_heads, wa, wb = prm["n_heads"], prm["wa"], prm["wb"]
    assert wa == wb and seq % (CHUNK * SUBLANE) == 0 and 4 * n_heads <= LANE
    c_qkv, c_za, c_beta, c_alpha, c_u, c_zb = 3 * wa, wa, 2 * n_heads, 2 * n_heads, wb, wb
    n_proj = prm["w_cols"].shape[2]
    ba_block = (n_proj - LANE) // LANE

    proj = _inproj(x2, prm["ln_g"], prm["w_cols"], layer)
    proj3 = proj.reshape(bsz, seq, n_proj)

    gates = _gates(proj, ba_block, prm["alog"], prm["dtb"], layer, n_heads)
    beta = gates[:, :c_beta].reshape(bsz, seq, 2, n_heads)
    gc = gates[:, c_beta:c_beta + c_alpha].reshape(bsz, seq, 2, n_heads)
    hpb = DELTA_HEADS_PER_PROGRAM
    cols = jnp.stack([gc[:, :, 0], gc[:, :, 1], beta[:, :, 0], beta[:, :, 1]], axis=-1)
    cols = jnp.transpose(cols.reshape(bsz, seq, n_heads // hpb, 4 * hpb), (0, 2, 1, 3))
    rows = jnp.transpose(gc.reshape(bsz, seq // CHUNK, CHUNK, 2, n_heads), (0, 4, 1, 3, 2))
    rows = rows.reshape(bsz, n_heads, seq // CHUNK, 1, 2 * CHUNK)

    o_a = _delta(proj3, prm["conv_w"], cols, rows, prm["head_norm_g"], layer, n_heads)
    o_a = o_a.reshape(t, wa)

    u_off = c_qkv + c_za
    y_s5 = _s5(proj, u_off // LANE, *prm["s5"], layer, bsz)

    return _out_stage(
        o_a, proj, y_s5, x2, prm["w_pa"], prm["d_skip"], prm["w_glu"], prm["b_glu"], prm["w_pb"],
        prm["b_gate"], prm["w_out"], prm["final_g"], layer=layer,
        za_block=c_qkv // wa, u_block=u_off // wb, zb_block=(u_off + c_u) // wb,
        gate_block0=(u_off + c_u + c_zb) // d, final_norm=final_norm)


def kernel(x, ln_g, w_in, conv_w, a_log, dt_bias, head_norm_g, lam_re, lam_im, log_dt, b_re, b_im, c_re, c_im, d_skip, w_glu, b_glu, w_pa, w_pb, b_gate, w_out, final_g):
    bsz, seq, d = x.shape
    depth = ln_g.shape[0]
    prm = _prepare_params(ln_g, w_in, conv_w, a_log, dt_bias, head_norm_g, lam_re, lam_im, log_dt, b_re, b_im,
                          c_re, c_im, d_skip, w_glu, b_glu, w_pa, w_pb, b_gate, w_out, final_g)
    x2 = x.reshape(bsz * seq, d)
    for layer in range(depth):
        x2 = _layer(x2, bsz, seq, prm, layer, final_norm=(layer == depth - 1))
    return x2.reshape(bsz, seq, d)
```

```python
import functools
import math

import jax
import jax.numpy as jnp
from jax import lax
from jax.experimental import pallas as pl
from jax.experimental.pallas import tpu as pltpu

F32 = jnp.float32
BF16 = jnp.bfloat16

LANE = 128
SUBLANE = 8
HEAD_DIM = 128
CHUNK = 64
CONV_K = 5
CONV_PAD = (CONV_K - 1) // 2
DELTA_HEADS_PER_PROGRAM = 2
LEVEL_GROUP_CHUNKS = 8
S5_GROUP_CH = 16
S5_STATE = 64
S5_BLOCK = 16
RMS_EPS = 1e-6
VMEM_LIMIT_BYTES = 56 * 1024 * 1024


def _compiler_params(semantics):
    return pltpu.CompilerParams(dimension_semantics=semantics, vmem_limit_bytes=VMEM_LIMIT_BYTES)


def _silu(x):
    return x * jax.nn.sigmoid(x)


def _largest_tile(n, cap, unit):
    best = unit
    t = unit
    while t <= min(n, cap):
        if n % t == 0:
            best = t
        t += unit
    return best


def _inproj_kernel(x_ref, g_ref, w_ref, o_ref, h_ref):
    @pl.when(pl.program_id(1) == 0)
    def _():
        x = x_ref[...]
        ms = jnp.mean(x * x, axis=-1, keepdims=True)
        h_ref[...] = (x * lax.rsqrt(ms + RMS_EPS) * g_ref[...]).astype(BF16)

    o_ref[...] = jnp.dot(h_ref[...], w_ref[...], preferred_element_type=F32)


def _inproj(x2, g_rows, w_bf16, layer):
    t, d = x2.shape
    n = w_bf16.shape[2]
    tm = _largest_tile(t, 1024, SUBLANE)
    tn = _largest_tile(n, 1280, LANE)
    return pl.pallas_call(
        _inproj_kernel,
        grid=(t // tm, n // tn),
        in_specs=[
            pl.BlockSpec((tm, d), lambda i, j: (i, 0)),
            pl.BlockSpec((None, 1, d), lambda i, j: (layer, 0, 0)),
            pl.BlockSpec((None, d, tn), lambda i, j: (layer, 0, j)),
        ],
        out_specs=pl.BlockSpec((tm, tn), lambda i, j: (i, j)),
        out_shape=jax.ShapeDtypeStruct((t, n), F32),
        scratch_shapes=[pltpu.VMEM((tm, d), BF16)],
        compiler_params=_compiler_params(("parallel", "arbitrary")),
        name="inproj",
    )(x2, g_rows, w_bf16)


def _gates_kernel(ba_ref, alog_ref, dtb_ref, o_ref, *, n_heads):
    x = ba_ref[...]
    tm = x.shape[0]
    lane = lax.broadcasted_iota(jnp.int32, x.shape, 1)
    beta = jax.nn.sigmoid(x)
    z = x + dtb_ref[...]
    softplus = jnp.maximum(z, 0.0) + jnp.log1p(jnp.exp(-jnp.abs(z)))
    g = -jnp.exp(alog_ref[...]) * softplus
    r = lax.broadcasted_iota(jnp.int32, (tm, tm), 0)
    c = lax.broadcasted_iota(jnp.int32, (tm, tm), 1)
    shift = int(math.log2(CHUNK))
    same_chunk = (r >> shift) == (c >> shift)
    m_prefix = jnp.where(same_chunk & (c <= r), 1.0, 0.0).astype(F32)
    m_suffix = jnp.where(same_chunk & (c >= r), 1.0, 0.0).astype(F32)
    g_prefix = jnp.dot(m_prefix, g, precision=lax.Precision.HIGHEST, preferred_element_type=F32)
    g_suffix = jnp.dot(m_suffix, g, precision=lax.Precision.HIGHEST, preferred_element_type=F32)
    gc = jnp.where(lane < 3 * n_heads, g_prefix, g_suffix)
    o_ref[...] = jnp.where(lane < 2 * n_heads, beta, gc)


def _gates(proj, ba_block, alog_rows, dtb_rows, layer, n_heads):
    t = proj.shape[0]
    tm = _largest_tile(t, 512, CHUNK)
    return pl.pallas_call(
        functools.partial(_gates_kernel, n_heads=n_heads),
        grid=(t // tm,),
        in_specs=[
            pl.BlockSpec((tm, LANE), lambda i: (i, ba_block)),
            pl.BlockSpec((None, 1, LANE), lambda i: (layer, 0, 0)),
            pl.BlockSpec((None, 1, LANE), lambda i: (layer, 0, 0)),
        ],
        out_specs=pl.BlockSpec((tm, LANE), lambda i: (i, 0)),
        out_shape=jax.ShapeDtypeStruct((t, LANE), F32),
        compiler_params=_compiler_params(("parallel",)),
        name="gates",
    )(proj, alog_rows, dtb_rows)


def _dot_nt(a, b):
    return lax.dot_general(a, b, (((1,), (1,)), ((), ())), preferred_element_type=F32)


def _delta_kernel(q_ref, k_ref, v_ref, wq_ref, wk_ref, wv_ref, cols_ref, rows_ref, hg_ref, o_ref,
                  pad_scr, r_scr, m_scr, wq_scr, kdt_scr, qk_scr, ob_scr, s_scr):
    seq = q_ref.shape[0]
    hpb = q_ref.shape[1] // HEAD_DIM
    n_chunks = seq // CHUNK
    halo = SUBLANE
    c2 = 2 * CHUNK
    operands = ((q_ref, wq_ref), (k_ref, wk_ref), (v_ref, wv_ref))
    for j in range(len(operands)):
        pad_scr[j, 0:halo, :] = jnp.zeros((halo, HEAD_DIM), F32)
        pad_scr[j, seq + halo:seq + 2 * halo, :] = jnp.zeros((halo, HEAD_DIM), F32)

    def conv_silu(j, w, r0):
        acc = None
        for i in range(CONV_K):
            term = pad_scr[j, pl.ds(r0 + halo + i - CONV_PAD, CHUNK), :] * w[i:i + 1, :]
            acc = term if acc is None else acc + term
        return _silu(acc)

    def l2norm(y):
        return y * lax.rsqrt(jnp.sum(y * y, axis=-1, keepdims=True) + RMS_EPS)

    ri = lax.broadcasted_iota(jnp.int32, (c2, c2), 0)
    ci = lax.broadcasted_iota(jnp.int32, (c2, c2), 1)
    fwd_block = (ri < CHUNK) & (ci < CHUNK)
    bwd_block = (ri >= CHUNK) & (ci >= CHUNK)
    strict2 = (fwd_block & (ri > ci)) | (bwd_block & (ri < ci))
    rw = lax.broadcasted_iota(jnp.int32, (CHUNK, c2), 0)
    cw = lax.broadcasted_iota(jnp.int32, (CHUNK, c2), 1)
    fwd_lanes = cw < CHUNK
    incl_wide = (fwd_lanes & (rw >= cw)) | (~fwd_lanes & (rw <= cw - CHUNK))
    top_rows = ri < CHUNK

    for p in range(hpb):
        lanes = slice(p * HEAD_DIM, (p + 1) * HEAD_DIM)

        def fill(n, carry, lanes=lanes):
            r0 = pl.multiple_of(n * CHUNK, CHUNK)
            for j, (src_ref, _) in enumerate(operands):
                pad_scr[j, pl.ds(r0 + halo, CHUNK), :] = src_ref[pl.ds(r0, CHUNK), lanes]
            return carry

        lax.fori_loop(0, n_chunks, fill, 0, unroll=2)

        def setup(n, carry, p=p, lanes=lanes):
            r0 = pl.multiple_of(n * CHUNK, CHUNK)
            q = l2norm(conv_silu(0, wq_ref[:, lanes], r0)) * (HEAD_DIM ** -0.5)
            k = l2norm(conv_silu(1, wk_ref[:, lanes], r0))
            v = conv_silu(2, wv_ref[:, lanes], r0)
            k16 = k.astype(BF16)
            k2_16 = jnp.concatenate([k16, k16], axis=0)
            kk2 = _dot_nt(k2_16, k2_16)
            qk_wide = _dot_nt(q.astype(BF16), k2_16)
            cols = cols_ref[pl.ds(r0, CHUNK), :]
            across = lambda c: jnp.broadcast_to(cols[:, 4 * p + c:4 * p + c + 1], (CHUNK, c2))
            gc2 = jnp.concatenate([across(0), across(1)], axis=0)
            be2 = jnp.concatenate([across(2), across(3)], axis=0)
            gc_row2 = rows_ref[p, n]
            k2 = jnp.concatenate([k, k], axis=0)
            v2 = jnp.concatenate([v, v], axis=0)
            q2 = jnp.concatenate([q, q], axis=0)
            decay2 = jnp.where(strict2, jnp.exp(gc2 - gc_row2), 0.0)
            m_scr[p, n] = (-(be2 * kk2 * decay2)).astype(BF16)
            eg2 = jnp.exp(gc2)
            r_scr[p, n] = jnp.concatenate([v2 * be2, k2 * (be2 * eg2)], axis=1)
            qg2 = (q2 * eg2).astype(BF16)
            wq_scr[p, 0, n, CHUNK:c2, :] = qg2[:CHUNK]
            wq_scr[p, 1, n, CHUNK:c2, :] = qg2[CHUNK:]
            g_last2 = jnp.where(top_rows, gc_row2[:, CHUNK - 1:CHUNK], gc_row2[:, CHUNK:CHUNK + 1])
            kdec2 = k2 * jnp.exp(g_last2 - gc2)
            kdt_scr[p, n] = kdec2.T.astype(BF16)
            gc_wide = jnp.where(fwd_lanes, gc2[:CHUNK], gc2[CHUNK:])
            qk_scr[p, n] = (qk_wide * jnp.where(incl_wide, jnp.exp(gc_wide - gc_row2), 0.0)).astype(BF16)
            return carry

        lax.fori_loop(0, n_chunks, setup, 0, unroll=2)

    group = _largest_tile(n_chunks, LEVEL_GROUP_CHUNKS, 1)

    def level(_, carry):
        def per_group(it, carry):
            tiles = [(p, it * group + g) for g in range(group) for p in range(hpb)]
            outs = []
            for p, n in tiles:
                m = m_scr[p, n]
                rhs = jnp.concatenate([r_scr[p, n].astype(BF16), m], axis=1)
                outs.append(jnp.dot(m, rhs, preferred_element_type=F32))
            for (p, n), out in zip(tiles, outs):
                r_scr[p, n] += out[:, :2 * HEAD_DIM]
                m_scr[p, n] = out[:, 2 * HEAD_DIM:].astype(BF16)
            return carry

        return lax.fori_loop(0, n_chunks // group, per_group, carry)

    lax.fori_loop(0, int(math.log2(CHUNK)), level, 0)

    def finish(n, carry):
        for p in range(hpb):
            w2 = r_scr[p, n, :, HEAD_DIM:2 * HEAD_DIM].astype(BF16)
            wq_scr[p, 0, n, 0:CHUNK, :] = w2[:CHUNK]
            wq_scr[p, 1, n, 0:CHUNK, :] = w2[CHUNK:]
        return carry

    lax.fori_loop(0, n_chunks, finish, 0, unroll=4)

    s_scr[...] = jnp.zeros(s_scr.shape, F32)
    zero_half = jnp.zeros((CHUNK, HEAD_DIM), BF16)
    chains = [(p, d) for p in range(hpb) for d in range(2)]

    def recurrence(i, carry):
        chunk_of = (i, n_chunks - 1 - i)
        states = [s_scr[p, d] for p, d in chains]
        ws = [jnp.dot(wq_scr[p, d, chunk_of[d]], s.astype(BF16), preferred_element_type=F32)
              for (p, d), s in zip(chains, states)]
        new_states = []
        for (p, d), s, w in zip(chains, states, ws):
            n = chunk_of[d]
            r0 = pl.multiple_of(n * CHUNK, CHUNK)
            v_new = (r_scr[p, n, d * CHUNK:(d + 1) * CHUNK, 0:HEAD_DIM] - w[:CHUNK]).astype(BF16)
            v_pad = jnp.concatenate([v_new, zero_half] if d == 0 else [zero_half, v_new], axis=0)
            o = w[CHUNK:] + jnp.dot(qk_scr[p, n], v_pad, preferred_element_type=F32)
            if d == 0:
                o_ref[pl.ds(r0, CHUNK), p * HEAD_DIM:(p + 1) * HEAD_DIM] = o
            else:
                ob_scr[p, pl.ds(r0, CHUNK), :] = o
            gc_row2 = rows_ref[p, n]
            g_last = gc_row2[:, CHUNK - 1:CHUNK] if d == 0 else gc_row2[:, CHUNK:CHUNK + 1]
            new_states.append(s * jnp.exp(g_last) + jnp.dot(kdt_scr[p, n], v_pad, preferred_element_type=F32))
        for (p, d), s in zip(chains, new_states):
            s_scr[p, d] = s
        return carry

    lax.fori_loop(0, n_chunks, recurrence, 0)

    def head_norm(n, carry):
        r0 = pl.multiple_of(n * CHUNK, CHUNK)
        for p in range(hpb):
            lanes = slice(p * HEAD_DIM, (p + 1) * HEAD_DIM)
            o = o_ref[pl.ds(r0, CHUNK), lanes] + ob_scr[p, pl.ds(r0, CHUNK), :]
            ms = jnp.mean(o * o, axis=-1, keepdims=True)
            o_ref[pl.ds(r0, CHUNK), lanes] = o * lax.rsqrt(ms + RMS_EPS) * hg_ref[...]
        return carry

    lax.fori_loop(0, n_chunks, head_norm, 0, unroll=2)


def _delta(proj3, conv_w, cols, rows, hg_rows, layer, n_heads):
    bsz, seq, _ = proj3.shape
    n_chunks = seq // CHUNK
    hpb = DELTA_HEADS_PER_PROGRAM
    width = hpb * HEAD_DIM
    n_blocks = n_heads // hpb
    tok = lambda off: pl.BlockSpec((None, seq, width), lambda b, h: (b, 0, off + h))
    cw = lambda off: pl.BlockSpec((None, CONV_K, width), lambda b, h: (layer, 0, off + h))
    return pl.pallas_call(
        _delta_kernel,
        grid=(bsz, n_blocks),
        in_specs=[
            tok(0), tok(n_blocks), tok(2 * n_blocks),
            cw(0), cw(n_blocks), cw(2 * n_blocks),
            pl.BlockSpec((None, None, seq, 4 * hpb), lambda b, h: (b, h, 0, 0)),
            pl.BlockSpec((None, hpb, n_chunks, 1, 2 * CHUNK), lambda b, h: (b, h, 0, 0, 0)),
            pl.BlockSpec((None, 1, HEAD_DIM), lambda b, h: (layer, 0, 0)),
        ],
        out_specs=pl.BlockSpec((None, seq, width), lambda b, h: (b, 0, h)),
        out_shape=jax.ShapeDtypeStruct((bsz, seq, n_heads * HEAD_DIM), F32),
        scratch_shapes=[
            pltpu.VMEM((3, seq + 2 * SUBLANE, HEAD_DIM), F32),
            pltpu.VMEM((hpb, n_chunks, 2 * CHUNK, 2 * HEAD_DIM), F32),
            pltpu.VMEM((hpb, n_chunks, 2 * CHUNK, 2 * CHUNK), BF16),
            pltpu.VMEM((hpb, 2, n_chunks, 2 * CHUNK, HEAD_DIM), BF16),
            pltpu.VMEM((hpb, n_chunks, HEAD_DIM, 2 * CHUNK), BF16),
            pltpu.VMEM((hpb, n_chunks, CHUNK, 2 * CHUNK), BF16),
            pltpu.VMEM((hpb, seq, HEAD_DIM), F32),
            pltpu.VMEM((hpb, 2, HEAD_DIM, HEAD_DIM), F32),
        ],
        compiler_params=_compiler_params(("parallel", "parallel")),
        name="delta",
    )(proj3, proj3, proj3, conv_w, conv_w, conv_w, cols, rows, hg_rows)


def _s5_kernel(u_ref, kf_ref, kb_ref, bt_ref, ct_ref, pw_ref, y_ref,
               ut_scr, u_scr, s_scr, yt_scr, m_scr, wsum_scr, woutt_scr, *, bsz):
    sb, ch = S5_BLOCK, S5_GROUP_CH
    rows = u_ref.shape[0] // sb
    groups = LANE // ch
    w = 2 * S5_STATE
    blk = sb * ch

    lane = lax.broadcasted_iota(jnp.int32, (ch, blk), 1)
    for g in range(groups):
        k_f, k_b = kf_ref[g], kb_ref[g]
        for j in range(sb):
            fwd = jnp.where(lane >= j * ch, pltpu.roll(k_f, j * ch, axis=1), 0.0)
            bwd = jnp.where(lane < (j + 1) * ch, pltpu.roll(k_b, (blk - (sb - 1 - j) * ch) % blk, axis=1), 0.0)
            m_scr[g, j * ch:(j + 1) * ch, :] = (fwd + bwd).astype(BF16)

    def cmul(a_re, a_im, b_re, b_im):
        return a_re * b_re - a_im * b_im, a_re * b_im + a_im * b_re

    for j in range(sb):
        t = u_ref[pl.ds(j, rows, stride=sb), :].T
        for g in range(groups):
            ut_scr[g, j * ch:(j + 1) * ch, :] = t[g * ch:(g + 1) * ch, :]

    n_blocks = rows // bsz
    pairs = groups // 2
    for q in range(pairs):
        pw = [[pw_ref[d, part, q] for part in range(2)] for d in range(2)]
        for h in range(2):
            bt = [[bt_ref[d, part, q, h] for part in range(2)] for d in range(2)]
            ct = [[ct_ref[d, part, q, h] for part in range(2)] for d in range(2)]
            for j in range(sb):
                r0 = h * blk + j * ch
                for d, (l_sum, l_out) in enumerate(((sb - 1 - j, j + 1), (j, sb - j))):
                    s_re, s_im = cmul(bt[d][0], bt[d][1], pw[d][0][l_sum:l_sum + 1], pw[d][1][l_sum:l_sum + 1])
                    z_re, z_im = cmul(ct[d][0], ct[d][1], pw[d][0][l_out:l_out + 1], pw[d][1][l_out:l_out + 1])
                    wsum_scr[r0:r0 + ch, 2 * d * w:(2 * d + 1) * w] = s_re.astype(BF16)
                    wsum_scr[r0:r0 + ch, (2 * d + 1) * w:(2 * d + 2) * w] = s_im.astype(BF16)
                    woutt_scr[q, r0:r0 + ch, 2 * d * w:(2 * d + 1) * w] = z_re.astype(BF16)
                    woutt_scr[q, r0:r0 + ch, (2 * d + 1) * w:(2 * d + 2) * w] = (-z_im).astype(BF16)

        for g in range(2):
            u_scr[2 * q + g] = ut_scr[2 * q + g].T.astype(BF16)
        u2 = jnp.concatenate([u_scr[2 * q], u_scr[2 * q + 1]], axis=1)
        summaries = jnp.dot(u2, wsum_scr[...], preferred_element_type=F32)
        for part in range(4):
            s_scr[part, q * rows:(q + 1) * rows, :] = summaries[:, part * w:(part + 1) * w]

    chains = pairs * bsz
    decay = [jnp.concatenate([jnp.broadcast_to(pw_ref[d, part, q][sb:sb + 1], (bsz, w)) for q in range(pairs)], axis=0)
             for d in range(2) for part in range(2)]

    def step(k, xs):
        xf_re, xf_im, xb_re, xb_im = xs
        rf = pl.ds(k, chains, stride=n_blocks)
        rb = pl.ds(n_blocks - 1 - k, chains, stride=n_blocks)
        sf_re, sf_im = s_scr[0, rf, :], s_scr[1, rf, :]
        sb_re, sb_im = s_scr[2, rb, :], s_scr[3, rb, :]
        s_scr[0, rf, :] = xf_re
        s_scr[1, rf, :] = xf_im
        s_scr[2, rb, :] = xb_re
        s_scr[3, rb, :] = xb_im
        nf_re, nf_im = cmul(decay[0], decay[1], xf_re, xf_im)
        nb_re, nb_im = cmul(decay[2], decay[3], xb_re, xb_im)
        return nf_re + sf_re, nf_im + sf_im, nb_re + sb_re, nb_im + sb_im

    zero = jnp.zeros((chains, w), F32)
    lax.fori_loop(0, n_blocks, step, (zero, zero, zero, zero), unroll=2)

    for q in range(pairs):
        entering = jnp.concatenate([s_scr[part, q * rows:(q + 1) * rows, :] for part in range(4)], axis=1)
        carried = _dot_nt(entering.astype(BF16), woutt_scr[q])
        for g in range(2):
            y = (jnp.dot(u_scr[2 * q + g], m_scr[2 * q + g], preferred_element_type=F32)
                 + carried[:, g * blk:(g + 1) * blk])
            y_t = y.T
            for i in range(sb):
                yt_scr[i, (2 * q + g) * ch:(2 * q + g + 1) * ch, :] = y_t[i * ch:(i + 1) * ch, :]

    for i in range(sb):
        y_ref[pl.ds(i, rows, stride=sb), :] = yt_scr[i].T


def _s5(proj, u_block0, k_f, k_b, b_t, c_t, powers, layer, bsz):
    t = proj.shape[0]
    rows = t // S5_BLOCK
    blk = S5_BLOCK * S5_GROUP_CH
    groups = LANE // S5_GROUP_CH
    pairs = groups // 2
    w = 2 * S5_STATE
    n_tiles = k_f.shape[1] // groups
    strip = pl.BlockSpec((None, groups, S5_GROUP_CH, blk), lambda k: (layer, k, 0, 0))
    placed = pl.BlockSpec((None, 2, 2, pairs, 2, S5_GROUP_CH, w), lambda k: (layer, 0, 0, k, 0, 0, 0))
    return pl.pallas_call(
        functools.partial(_s5_kernel, bsz=bsz),
        grid=(n_tiles,),
        in_specs=[
            pl.BlockSpec((t, LANE), lambda k: (0, u_block0 + k)),
            strip, strip, placed, placed,
            pl.BlockSpec((None, 2, 2, pairs, S5_BLOCK + 1, w), lambda k: (layer, 0, 0, k, 0, 0)),
        ],
        out_specs=pl.BlockSpec((t, LANE), lambda k: (0, k)),
        out_shape=jax.ShapeDtypeStruct((t, n_tiles * LANE), F32),
        scratch_shapes=[
            pltpu.VMEM((groups, blk, rows), F32),
            pltpu.VMEM((groups, rows, blk), BF16),
            pltpu.VMEM((4, pairs * rows, w), F32),
            pltpu.VMEM((S5_BLOCK, LANE, rows), F32),
            pltpu.VMEM((groups, blk, blk), BF16),
            pltpu.VMEM((2 * blk, 4 * w), BF16),
            pltpu.VMEM((pairs, 2 * blk, 4 * w), BF16),
        ],
        compiler_params=_compiler_params(("parallel",)),
        name="s5",
    )(proj, k_f, k_b, b_t, c_t, powers)


def _s5_params(lam_re, lam_im, log_dt, b_re, b_im, c_re, c_im):
    n_groups = lam_re.shape[1]
    sb, p, c = S5_BLOCK, S5_STATE, S5_GROUP_CH
    l_re, l_im = lam_re.astype(F32), lam_im.astype(F32)
    dt = jnp.exp(log_dt.astype(F32))[..., None]
    mag = jnp.exp(l_re * dt)
    bar_re, bar_im = mag * jnp.cos(l_im * dt), mag * jnp.sin(l_im * dt)
    n_re, n_im = bar_re - 1.0, bar_im
    den = l_re * l_re + l_im * l_im
    k_re = ((n_re * l_re + n_im * l_im) / den)[..., None]
    k_im = ((n_im * l_re - n_re * l_im) / den)[..., None]
    bb_re = k_re * b_re.astype(F32) - k_im * b_im.astype(F32)
    bb_im = k_re * b_im.astype(F32) + k_im * b_re.astype(F32)
    cc_re, cc_im = c_re.astype(F32), c_im.astype(F32)

    steps = jnp.arange(sb + 1, dtype=F32)[:, None, None, None]
    mag_l = jnp.exp(steps * (l_re * dt)[None])
    pw_re, pw_im = mag_l * jnp.cos(steps * (l_im * dt)[None]), mag_l * jnp.sin(steps * (l_im * dt)[None])

    bt_re, bt_im = jnp.swapaxes(bb_re, 2, 3)[:, :, None], jnp.swapaxes(bb_im, 2, 3)[:, :, None]
    cb_re = cc_re[:, :, :, None] * bt_re - cc_im[:, :, :, None] * bt_im
    cb_im = cc_re[:, :, :, None] * bt_im + cc_im[:, :, :, None] * bt_re
    kern = jnp.sum(cb_re[None] * pw_re[:sb, :, :, None, None, :] - cb_im[None] * pw_im[:sb, :, :, None, None, :],
                   axis=-1)
    k_f = jnp.transpose(kern[:, 0], (1, 3, 0, 2)).reshape(n_groups, c, sb * c)
    k_b = jnp.transpose(jnp.flip(kern[:, 1], 0), (1, 3, 0, 2)).reshape(n_groups, c, sb * c)

    same = jnp.eye(2, dtype=F32)
    def place(t):
        t2 = t.reshape(2, n_groups // 2, 2, c, 1, p) * same[None, None, :, None, :, None]
        return t2.reshape(2, n_groups // 2, 2, c, 2 * p)
    b_t = jnp.stack([place(jnp.swapaxes(bb_re, 2, 3)), place(jnp.swapaxes(bb_im, 2, 3))], axis=1)
    c_t = jnp.stack([place(cc_re), place(cc_im)], axis=1)
    def side_by_side(t):
        t2 = jnp.transpose(t.reshape(sb + 1, 2, n_groups // 2, 2, p), (1, 2, 0, 3, 4))
        return t2.reshape(2, n_groups // 2, sb + 1, 2 * p)
    powers = jnp.stack([side_by_side(pw_re), side_by_side(pw_im)], axis=1)
    return k_f, k_b, b_t, c_t, powers


def _out_kernel(oa_ref, za_ref, ys_ref, u_ref, zb_ref, ga_ref, gb_ref, x_ref,
                wpa_ref, dsk_ref, wglu_ref, bglu_ref, wpb_ref, bga_ref, bgb_ref, wout_ref, fg_ref,
                o_ref, *, final_norm):
    a = (oa_ref[...] * _silu(za_ref[...])).astype(BF16)
    y_a = jnp.dot(a, wpa_ref[...], preferred_element_type=F32)
    y_s = jax.nn.gelu(ys_ref[...] + u_ref[...] * dsk_ref[...])
    glu = jnp.dot(y_s.astype(BF16), wglu_ref[...], preferred_element_type=F32) + bglu_ref[...]
    y_s = y_s * jax.nn.sigmoid(glu)
    y_b = jnp.dot((y_s * _silu(zb_ref[...])).astype(BF16), wpb_ref[...], preferred_element_type=F32)
    merged = (jax.nn.sigmoid(ga_ref[...] + bga_ref[...]) * y_a
              + jax.nn.sigmoid(gb_ref[...] + bgb_ref[...]) * y_b)
    out = x_ref[...] + jnp.dot(merged.astype(BF16), wout_ref[...], preferred_element_type=F32)
    if final_norm:
        ms = jnp.mean(out * out, axis=-1, keepdims=True)
        out = out * lax.rsqrt(ms + RMS_EPS) * fg_ref[...]
    o_ref[...] = out


def _out_stage(o_a, proj, y_s5, x2, w_pa, d_skip, w_glu, b_glu, w_pb, b_gate, w_out, final_g, *,
               layer, za_block, u_block, zb_block, gate_block0, final_norm):
    t, d = x2.shape
    wa = o_a.shape[1]
    wb = y_s5.shape[1]
    tm = _largest_tile(t, 256, SUBLANE)
    row = lambda width, blk: pl.BlockSpec((tm, width), lambda i: (i, blk))
    const = lambda shape, blk=0: pl.BlockSpec((None,) + shape, lambda i: (layer, 0, blk),
                                              pipeline_mode=pl.Buffered(1))
    return pl.pallas_call(
        functools.partial(_out_kernel, final_norm=final_norm),
        grid=(t // tm,),
        in_specs=[
            row(wa, 0), row(wa, za_block), row(wb, 0), row(wb, u_block), row(wb, zb_block),
            row(d, gate_block0), row(d, gate_block0 + 1), row(d, 0),
            const((wa, d)), const((1, wb)), const((wb, wb)), const((1, wb)), const((wb, d)),
            const((1, d), 0), const((1, d), 1), const((d, d)), const((1, d)),
        ],
        out_specs=pl.BlockSpec((tm, d), lambda i: (i, 0)),
        out_shape=jax.ShapeDtypeStruct((t, d), F32),
        compiler_params=_compiler_params(("parallel",)),
        name="out_stage",
    )(o_a, proj, y_s5, proj, proj, proj, proj, x2, w_pa, d_skip, w_glu, b_glu, w_pb, b_gate, b_gate, w_out,
      final_g)


def _prepare_params(ln_g, w_in, conv_w, a_log, dt_bias, head_norm_g, lam_re, lam_im, log_dt, b_re, b_im,
                    c_re, c_im, d_skip, w_glu, b_glu, w_pa, w_pb, b_gate, w_out, final_g):
    depth, d, _ = w_in.shape
    n_heads = a_log.shape[2]
    wa = n_heads * HEAD_DIM
    wb = d_skip.shape[1]
    n_ba = 4 * n_heads
    o_beta = 4 * wa
    o_u = o_beta + n_ba
    w_cols = jnp.concatenate(
        [w_in[:, :, :o_beta], w_in[:, :, o_u:], w_in[:, :, o_beta:o_u],
         jnp.zeros((depth, d, LANE - n_ba), w_in.dtype)], axis=2).astype(BF16)
    pad_rows = lambda p: jnp.pad(p.astype(F32).reshape(depth, 1, n_ba // 2),
                                 ((0, 0), (0, 0), (n_ba // 2, LANE - n_ba)))
    row = lambda p: p.astype(F32).reshape(depth, 1, -1)
    return dict(
        ln_g=row(ln_g), w_cols=w_cols, conv_w=conv_w.astype(F32), alog=pad_rows(a_log), dtb=pad_rows(dt_bias),
        head_norm_g=row(head_norm_g),
        s5=jax.vmap(_s5_params)(lam_re, lam_im, log_dt, b_re, b_im, c_re, c_im),
        d_skip=row(d_skip), w_glu=w_glu.astype(BF16), b_glu=row(b_glu), w_pa=w_pa.astype(BF16),
        w_pb=w_pb.astype(BF16), b_gate=row(b_gate), w_out=w_out.astype(BF16),
        final_g=jnp.broadcast_to(final_g.astype(F32).reshape(1, 1, d), (depth, 1, d)),
        n_heads=n_heads, wa=wa, wb=wb)


def _layer(x2, bsz, seq, prm, layer, final_norm):
    t, d = x2.shape
    n_heads, wa, wb = prm["n_heads"], prm["wa"], prm["wb"]
    assert wa == wb and seq % (CHUNK * SUBLANE) == 0 and 4 * n_heads <= LANE
    c_qkv, c_za, c_beta, c_alpha, c_u, c_zb = 3 * wa, wa, 2 * n_heads, 2 * n_heads, wb, wb
    n_proj = prm["w_cols"].shape[2]
    ba_block = (n_proj - LANE) // LANE

    proj = _inproj(x2, prm["ln_g"], prm["w_cols"], layer)
    proj3 = proj.reshape(bsz, seq, n_proj)

    gates = _gates(proj, ba_block, prm["alog"], prm["dtb"], layer, n_heads)
    beta = gates[:, :c_beta].reshape(bsz, seq, 2, n_heads)
    gc = gates[:, c_beta:c_beta + c_alpha].reshape(bsz, seq, 2, n_heads)
    hpb = DELTA_HEADS_PER_PROGRAM
    cols = jnp.stack([gc[:, :, 0], gc[:, :, 1], beta[:, :, 0], beta[:, :, 1]], axis=-1)
    cols = jnp.transpose(cols.reshape(bsz, seq, n_heads // hpb, 4 * hpb), (0, 2, 1, 3))
    rows = jnp.transpose(gc.reshape(bsz, seq // CHUNK, CHUNK, 2, n_heads), (0, 4, 1, 3, 2))
    rows = rows.reshape(bsz, n_heads, seq // CHUNK, 1, 2 * CHUNK)

    o_a = _delta(proj3, prm["conv_w"], cols, rows, prm["head_norm_g"], layer, n_heads)
    o_a = o_a.reshape(t, wa)

    u_off = c_qkv + c_za
    y_s5 = _s5(proj, u_off // LANE, *prm["s5"], layer, bsz)

    return _out_stage(
        o_a, proj, y_s5, x2, prm["w_pa"], prm["d_skip"], prm["w_glu"], prm["b_glu"], prm["w_pb"],
        prm["b_gate"], prm["w_out"], prm["final_g"], layer=layer,
        za_block=c_qkv // wa, u_block=u_off // wb, zb_block=(u_off + c_u) // wb,
        gate_block0=(u_off + c_u + c_zb) // d, final_norm=final_norm)


def kernel(x, ln_g, w_in, conv_w, a_log, dt_bias, head_norm_g, lam_re, lam_im, log_dt, b_re, b_im, c_re, c_im, d_skip, w_glu, b_glu, w_pa, w_pb, b_gate, w_out, final_g):
    bsz, seq, d = x.shape
    depth = ln_g.shape[0]
    prm = _prepare_params(ln_g, w_in, conv_w, a_log, dt_bias, head_norm_g, lam_re, lam_im, log_dt, b_re, b_im,
                          c_re, c_im, d_skip, w_glu, b_glu, w_pa, w_pb, b_gate, w_out, final_g)
    x2 = x.reshape(bsz * seq, d)
    for layer in range(depth):
        x2 = _layer(x2, bsz, seq, prm, layer, final_norm=(layer == depth - 1))
    return x2.reshape(bsz, seq, d)
```

```python
import functools
import math

import jax
import jax.numpy as jnp
from jax import lax
from jax.experimental import pallas as pl
from jax.experimental.pallas import tpu as pltpu

F32 = jnp.float32
BF16 = jnp.bfloat16

LANE = 128
SUBLANE = 8
HEAD_DIM = 128
CHUNK = 64
CONV_K = 5
CONV_PAD = (CONV_K - 1) // 2
DELTA_HEADS_PER_PROGRAM = 2
LEVEL_GROUP_CHUNKS = 8
S5_GROUP_CH = 16
S5_STATE = 64
S5_BLOCK = 16
RMS_EPS = 1e-6
VMEM_LIMIT_BYTES = 56 * 1024 * 1024


def _compiler_params(semantics):
    return pltpu.CompilerParams(dimension_semantics=semantics, vmem_limit_bytes=VMEM_LIMIT_BYTES)


def _silu(x):
    return x * jax.nn.sigmoid(x)


def _largest_tile(n, cap, unit):
    best = unit
    t = unit
    while t <= min(n, cap):
        if n % t == 0:
            best = t
        t += unit
    return best


def _regroup_kernel(a_ref, b_ref, o_ref, *, n_plain, shift):
    @pl.when(pl.program_id(2) < n_plain)
    def _():
        o_ref[...] = a_ref[...].astype(BF16)

    @pl.when(pl.program_id(2) >= n_plain)
    def _():
        both = jnp.concatenate([a_ref[...], b_ref[...]], axis=1)
        o_ref[...] = both[:, shift:shift + o_ref.shape[1]].astype(BF16)


def _regroup_weights(w_in, o_skip, n_skip):
    depth, d, n = w_in.shape
    n_out = n - n_skip
    tn = _largest_tile(math.gcd(o_skip, n_out - o_skip), 1024, LANE)
    tr = _largest_tile(d, 512, SUBLANE)
    return pl.pallas_call(
        functools.partial(_regroup_kernel, n_plain=o_skip // tn, shift=n_skip),
        grid=(depth, d // tr, n_out // tn),
        in_specs=[
            pl.BlockSpec((None, tr, tn), lambda l, r, j: (l, r, j)),
            pl.BlockSpec((None, tr, LANE), lambda l, r, j: (l, r, (j + 1) * (tn // LANE))),
        ],
        out_specs=pl.BlockSpec((None, tr, tn), lambda l, r, j: (l, r, j)),
        out_shape=jax.ShapeDtypeStruct((depth, d, n_out), BF16),
        compiler_params=_compiler_params(("parallel", "parallel", "parallel")),
        name="regroup",
    )(w_in, w_in)


def _inproj_kernel(x_ref, g_ref, w_ref, wba_ref, o_ref, ba_ref, h_ref):
    @pl.when(pl.program_id(1) == 0)
    def _():
        x = x_ref[...]
        ms = jnp.mean(x * x, axis=-1, keepdims=True)
        h_ref[...] = (x * lax.rsqrt(ms + RMS_EPS) * g_ref[...]).astype(BF16)
        ba_ref[...] = jnp.dot(h_ref[...], wba_ref[...], preferred_element_type=F32)

    o_ref[...] = jnp.dot(h_ref[...], w_ref[...], preferred_element_type=F32)


def _inproj(x2, g_rows, w_bf16, w_ba, layer):
    t, d = x2.shape
    n = w_bf16.shape[2]
    tm = _largest_tile(t, 1024, SUBLANE)
    tn = _largest_tile(n, 1280, LANE)
    return pl.pallas_call(
        _inproj_kernel,
        grid=(t // tm, n // tn),
        in_specs=[
            pl.BlockSpec((tm, d), lambda i, j: (i, 0)),
            pl.BlockSpec((None, 1, d), lambda i, j: (layer, 0, 0)),
            pl.BlockSpec((None, d, tn), lambda i, j: (layer, 0, j)),
            pl.BlockSpec((None, d, LANE), lambda i, j: (layer, 0, 0)),
        ],
        out_specs=[pl.BlockSpec((tm, tn), lambda i, j: (i, j)),
                   pl.BlockSpec((tm, LANE), lambda i, j: (i, 0))],
        out_shape=[jax.ShapeDtypeStruct((t, n), F32), jax.ShapeDtypeStruct((t, LANE), F32)],
        scratch_shapes=[pltpu.VMEM((tm, d), BF16)],
        compiler_params=_compiler_params(("parallel", "arbitrary")),
        name="inproj",
    )(x2, g_rows, w_bf16, w_ba)


def _gates_kernel(ba_ref, alog_ref, dtb_ref, o_ref, *, n_heads):
    x = ba_ref[...]
    tm = x.shape[0]
    lane = lax.broadcasted_iota(jnp.int32, x.shape, 1)
    beta = jax.nn.sigmoid(x)
    z = x + dtb_ref[...]
    softplus = jnp.maximum(z, 0.0) + jnp.log1p(jnp.exp(-jnp.abs(z)))
    g = -jnp.exp(alog_ref[...]) * softplus
    r = lax.broadcasted_iota(jnp.int32, (tm, tm), 0)
    c = lax.broadcasted_iota(jnp.int32, (tm, tm), 1)
    shift = int(math.log2(CHUNK))
    same_chunk = (r >> shift) == (c >> shift)
    m_prefix = jnp.where(same_chunk & (c <= r), 1.0, 0.0).astype(F32)
    m_suffix = jnp.where(same_chunk & (c >= r), 1.0, 0.0).astype(F32)
    g_prefix = jnp.dot(m_prefix, g, precision=lax.Precision.HIGHEST, preferred_element_type=F32)
    g_suffix = jnp.dot(m_suffix, g, precision=lax.Precision.HIGHEST, preferred_element_type=F32)
    gc = jnp.where(lane < 3 * n_heads, g_prefix, g_suffix)
    o_ref[...] = jnp.where(lane < 2 * n_heads, beta, gc)


def _gates(ba_logits, alog_rows, dtb_rows, layer, n_heads):
    t = ba_logits.shape[0]
    tm = _largest_tile(t, 512, CHUNK)
    return pl.pallas_call(
        functools.partial(_gates_kernel, n_heads=n_heads),
        grid=(t // tm,),
        in_specs=[
            pl.BlockSpec((tm, LANE), lambda i: (i, 0)),
            pl.BlockSpec((None, 1, LANE), lambda i: (layer, 0, 0)),
            pl.BlockSpec((None, 1, LANE), lambda i: (layer, 0, 0)),
        ],
        out_specs=pl.BlockSpec((tm, LANE), lambda i: (i, 0)),
        out_shape=jax.ShapeDtypeStruct((t, LANE), F32),
        compiler_params=_compiler_params(("parallel",)),
        name="gates",
    )(ba_logits, alog_rows, dtb_rows)


def _dot_nt(a, b):
    return lax.dot_general(a, b, (((1,), (1,)), ((), ())), preferred_element_type=F32)


def _delta_kernel(q_ref, k_ref, v_ref, wq_ref, wk_ref, wv_ref, cols_ref, rows_ref, hg_ref, o_ref,
                  pad_scr, r_scr, m_scr, wq_scr, kdt_scr, qk_scr, ob_scr, s_scr):
    seq = q_ref.shape[0]
    hpb = q_ref.shape[1] // HEAD_DIM
    n_chunks = seq // CHUNK
    halo = SUBLANE
    c2 = 2 * CHUNK
    operands = ((q_ref, wq_ref), (k_ref, wk_ref), (v_ref, wv_ref))
    for j in range(len(operands)):
        pad_scr[j, 0:halo, :] = jnp.zeros((halo, HEAD_DIM), F32)
        pad_scr[j, seq + halo:seq + 2 * halo, :] = jnp.zeros((halo, HEAD_DIM), F32)

    def conv_silu(j, w, r0):
        acc = None
        for i in range(CONV_K):
            term = pad_scr[j, pl.ds(r0 + halo + i - CONV_PAD, CHUNK), :] * w[i:i + 1, :]
            acc = term if acc is None else acc + term
        return _silu(acc)

    def l2norm(y):
        return y * lax.rsqrt(jnp.sum(y * y, axis=-1, keepdims=True) + RMS_EPS)

    ri = lax.broadcasted_iota(jnp.int32, (c2, c2), 0)
    ci = lax.broadcasted_iota(jnp.int32, (c2, c2), 1)
    fwd_block = (ri < CHUNK) & (ci < CHUNK)
    bwd_block = (ri >= CHUNK) & (ci >= CHUNK)
    strict2 = (fwd_block & (ri > ci)) | (bwd_block & (ri < ci))
    rw = lax.broadcasted_iota(jnp.int32, (CHUNK, c2), 0)
    cw = lax.broadcasted_iota(jnp.int32, (CHUNK, c2), 1)
    fwd_lanes = cw < CHUNK
    incl_wide = (fwd_lanes & (rw >= cw)) | (~fwd_lanes & (rw <= cw - CHUNK))
    top_rows = ri < CHUNK

    for p in range(hpb):
        lanes = slice(p * HEAD_DIM, (p + 1) * HEAD_DIM)

        def fill(n, carry, lanes=lanes):
            r0 = pl.multiple_of(n * CHUNK, CHUNK)
            for j, (src_ref, _) in enumerate(operands):
                pad_scr[j, pl.ds(r0 + halo, CHUNK), :] = src_ref[pl.ds(r0, CHUNK), lanes]
            return carry

        lax.fori_loop(0, n_chunks, fill, 0, unroll=2)

        def setup(n, carry, p=p, lanes=lanes):
            r0 = pl.multiple_of(n * CHUNK, CHUNK)
            q = l2norm(conv_silu(0, wq_ref[:, lanes], r0)) * (HEAD_DIM ** -0.5)
            k = l2norm(conv_silu(1, wk_ref[:, lanes], r0))
            v = conv_silu(2, wv_ref[:, lanes], r0)
            k16 = k.astype(BF16)
            k2_16 = jnp.concatenate([k16, k16], axis=0)
            kk2 = _dot_nt(k2_16, k2_16)
            qk_wide = _dot_nt(q.astype(BF16), k2_16)
            cols = cols_ref[pl.ds(r0, CHUNK), :]
            across = lambda c: jnp.broadcast_to(cols[:, 4 * p + c:4 * p + c + 1], (CHUNK, c2))
            gc2 = jnp.concatenate([across(0), across(1)], axis=0)
            be2 = jnp.concatenate([across(2), across(3)], axis=0)
            gc_row2 = rows_ref[p, n]
            k2 = jnp.concatenate([k, k], axis=0)
            v2 = jnp.concatenate([v, v], axis=0)
            q2 = jnp.concatenate([q, q], axis=0)
            decay2 = jnp.where(strict2, jnp.exp(gc2 - gc_row2), 0.0)
            m_scr[p, n] = (-(be2 * kk2 * decay2)).astype(BF16)
            eg2 = jnp.exp(gc2)
            r_scr[p, n] = jnp.concatenate([v2 * be2, k2 * (be2 * eg2)], axis=1)
            qg2 = (q2 * eg2).astype(BF16)
            wq_scr[p, 0, n, CHUNK:c2, :] = qg2[:CHUNK]
            wq_scr[p, 1, n, CHUNK:c2, :] = qg2[CHUNK:]
            g_last2 = jnp.where(top_rows, gc_row2[:, CHUNK - 1:CHUNK], gc_row2[:, CHUNK:CHUNK + 1])
            kdec2 = k2 * jnp.exp(g_last2 - gc2)
            kdt_scr[p, n] = kdec2.T.astype(BF16)
            gc_wide = jnp.where(fwd_lanes, gc2[:CHUNK], gc2[CHUNK:])
            qk_scr[p, n] = (qk_wide * jnp.where(incl_wide, jnp.exp(gc_wide - gc_row2), 0.0)).astype(BF16)
            return carry

        lax.fori_loop(0, n_chunks, setup, 0, unroll=2)

    group = _largest_tile(n_chunks, LEVEL_GROUP_CHUNKS, 1)

    def level(_, carry):
        def per_group(it, carry):
            tiles = [(p, it * group + g) for g in range(group) for p in range(hpb)]
            outs = []
            for p, n in tiles:
                m = m_scr[p, n]
                rhs = jnp.concatenate([r_scr[p, n].astype(BF16), m], axis=1)
                outs.append(jnp.dot(m, rhs, preferred_element_type=F32))
            for (p, n), out in zip(tiles, outs):
                r_scr[p, n] += out[:, :2 * HEAD_DIM]
                m_scr[p, n] = out[:, 2 * HEAD_DIM:].astype(BF16)
            return carry

        return lax.fori_loop(0, n_chunks // group, per_group, carry)

    lax.fori_loop(0, int(math.log2(CHUNK)), level, 0)

    def finish(n, carry):
        for p in range(hpb):
            w2 = r_scr[p, n, :, HEAD_DIM:2 * HEAD_DIM].astype(BF16)
            wq_scr[p, 0, n, 0:CHUNK, :] = w2[:CHUNK]
            wq_scr[p, 1, n, 0:CHUNK, :] = w2[CHUNK:]
        return carry

    lax.fori_loop(0, n_chunks, finish, 0, unroll=4)

    s_scr[...] = jnp.zeros(s_scr.shape, F32)
    zero_half = jnp.zeros((CHUNK, HEAD_DIM), BF16)
    chains = [(p, d) for p in range(hpb) for d in range(2)]

    def recurrence(i, carry):
        chunk_of = (i, n_chunks - 1 - i)
        states = [s_scr[p, d] for p, d in chains]
        ws = [jnp.dot(wq_scr[p, d, chunk_of[d]], s.astype(BF16), preferred_element_type=F32)
              for (p, d), s in zip(chains, states)]
        new_states = []
        for (p, d), s, w in zip(chains, states, ws):
            n = chunk_of[d]
            r0 = pl.multiple_of(n * CHUNK, CHUNK)
            v_new = (r_scr[p, n, d * CHUNK:(d + 1) * CHUNK, 0:HEAD_DIM] - w[:CHUNK]).astype(BF16)
            v_pad = jnp.concatenate([v_new, zero_half] if d == 0 else [zero_half, v_new], axis=0)
            o = w[CHUNK:] + jnp.dot(qk_scr[p, n], v_pad, preferred_element_type=F32)
            if d == 0:
                o_ref[pl.ds(r0, CHUNK), p * HEAD_DIM:(p + 1) * HEAD_DIM] = o
            else:
                ob_scr[p, pl.ds(r0, CHUNK), :] = o
            gc_row2 = rows_ref[p, n]
            g_last = gc_row2[:, CHUNK - 1:CHUNK] if d == 0 else gc_row2[:, CHUNK:CHUNK + 1]
            new_states.append(s * jnp.exp(g_last) + jnp.dot(kdt_scr[p, n], v_pad, preferred_element_type=F32))
        for (p, d), s in zip(chains, new_states):
            s_scr[p, d] = s
        return carry

    lax.fori_loop(0, n_chunks, recurrence, 0)

    def head_norm(n, carry):
        r0 = pl.multiple_of(n * CHUNK, CHUNK)
        for p in range(hpb):
            lanes = slice(p * HEAD_DIM, (p + 1) * HEAD_DIM)
            o = o_ref[pl.ds(r0, CHUNK), lanes] + ob_scr[p, pl.ds(r0, CHUNK), :]
            ms = jnp.mean(o * o, axis=-1, keepdims=True)
            o_ref[pl.ds(r0, CHUNK), lanes] = o * lax.rsqrt(ms + RMS_EPS) * hg_ref[...]
        return carry

    lax.fori_loop(0, n_chunks, head_norm, 0, unroll=2)


def _delta(proj3, conv_w, cols, rows, hg_rows, layer, n_heads):
    bsz, seq, _ = proj3.shape
    n_chunks = seq // CHUNK
    hpb = DELTA_HEADS_PER_PROGRAM
    width = hpb * HEAD_DIM
    n_blocks = n_heads // hpb
    tok = lambda off: pl.BlockSpec((None, seq, width), lambda b, h: (b, 0, off + h))
    cw = lambda off: pl.BlockSpec((None, CONV_K, width), lambda b, h: (layer, 0, off + h))
    return pl.pallas_call(
        _delta_kernel,
        grid=(bsz, n_blocks),
        in_specs=[
            tok(0), tok(n_blocks), tok(2 * n_blocks),
            cw(0), cw(n_blocks), cw(2 * n_blocks),
            pl.BlockSpec((None, None, seq, 4 * hpb), lambda b, h: (b, h, 0, 0)),
            pl.BlockSpec((None, hpb, n_chunks, 1, 2 * CHUNK), lambda b, h: (b, h, 0, 0, 0)),
            pl.BlockSpec((None, 1, HEAD_DIM), lambda b, h: (layer, 0, 0)),
        ],
        out_specs=pl.BlockSpec((None, seq, width), lambda b, h: (b, 0, h)),
        out_shape=jax.ShapeDtypeStruct((bsz, seq, n_heads * HEAD_DIM), F32),
        scratch_shapes=[
            pltpu.VMEM((3, seq + 2 * SUBLANE, HEAD_DIM), F32),
            pltpu.VMEM((hpb, n_chunks, 2 * CHUNK, 2 * HEAD_DIM), F32),
            pltpu.VMEM((hpb, n_chunks, 2 * CHUNK, 2 * CHUNK), BF16),
            pltpu.VMEM((hpb, 2, n_chunks, 2 * CHUNK, HEAD_DIM), BF16),
            pltpu.VMEM((hpb, n_chunks, HEAD_DIM, 2 * CHUNK), BF16),
            pltpu.VMEM((hpb, n_chunks, CHUNK, 2 * CHUNK), BF16),
            pltpu.VMEM((hpb, seq, HEAD_DIM), F32),
            pltpu.VMEM((hpb, 2, HEAD_DIM, HEAD_DIM), F32),
        ],
        compiler_params=_compiler_params(("parallel", "parallel")),
        name="delta",
    )(proj3, proj3, proj3, conv_w, conv_w, conv_w, cols, rows, hg_rows)


def _s5_kernel(u_ref, bt_ref, ct_ref, pw_ref, y_ref,
               ut_scr, u_scr, s_scr, yt_scr, m_scr, wsum_scr, woutt_scr, *, bsz):
    sb, ch = S5_BLOCK, S5_GROUP_CH
    rows = u_ref.shape[0] // sb
    groups = LANE // ch
    w = 2 * S5_STATE
    blk = sb * ch

    def cmul(a_re, a_im, b_re, b_im):
        return a_re * b_re - a_im * b_im, a_re * b_im + a_im * b_re

    def lag_strip(d, q, h, lags):
        z = [cmul(ct_ref[d, 0, q, h], ct_ref[d, 1, q, h], pw_ref[d, 0, q][l:l + 1], pw_ref[d, 1, q][l:l + 1])
             for l in lags]
        z_re = jnp.concatenate([t[0] for t in z], axis=0)
        z_im = jnp.concatenate([t[1] for t in z], axis=0)
        over_states = functools.partial(lax.dot_general, dimension_numbers=(((1,), (1,)), ((), ())),
                                        precision=lax.Precision.HIGHEST, preferred_element_type=F32)
        return over_states(bt_ref[d, 0, q, h], z_re) - over_states(bt_ref[d, 1, q, h], z_im)

    lane = lax.broadcasted_iota(jnp.int32, (ch, blk), 1)
    for g in range(groups):
        k_f = lag_strip(0, g // 2, g % 2, range(sb))
        k_b = lag_strip(1, g // 2, g % 2, range(sb - 1, -1, -1))
        for j in range(sb):
            fwd = jnp.where(lane >= j * ch, pltpu.roll(k_f, j * ch, axis=1), 0.0)
            bwd = jnp.where(lane < (j + 1) * ch, pltpu.roll(k_b, (blk - (sb - 1 - j) * ch) % blk, axis=1), 0.0)
            m_scr[g, j * ch:(j + 1) * ch, :] = (fwd + bwd).astype(BF16)

    for j in range(sb):
        t = u_ref[pl.ds(j, rows, stride=sb), :].T
        for g in range(groups):
            ut_scr[g, j * ch:(j + 1) * ch, :] = t[g * ch:(g + 1) * ch, :]

    n_blocks = rows // bsz
    pairs = groups // 2
    for q in range(pairs):
        pw = [[pw_ref[d, part, q] for part in range(2)] for d in range(2)]
        for h in range(2):
            bt = [[bt_ref[d, part, q, h] for part in range(2)] for d in range(2)]
            ct = [[ct_ref[d, part, q, h] for part in range(2)] for d in range(2)]
            for j in range(sb):
                r0 = h * blk + j * ch
                for d, (l_sum, l_out) in enumerate(((sb - 1 - j, j + 1), (j, sb - j))):
                    s_re, s_im = cmul(bt[d][0], bt[d][1], pw[d][0][l_sum:l_sum + 1], pw[d][1][l_sum:l_sum + 1])
                    z_re, z_im = cmul(ct[d][0], ct[d][1], pw[d][0][l_out:l_out + 1], pw[d][1][l_out:l_out + 1])
                    wsum_scr[r0:r0 + ch, 2 * d * w:(2 * d + 1) * w] = s_re.astype(BF16)
                    wsum_scr[r0:r0 + ch, (2 * d + 1) * w:(2 * d + 2) * w] = s_im.astype(BF16)
                    woutt_scr[q, r0:r0 + ch, 2 * d * w:(2 * d + 1) * w] = z_re.astype(BF16)
                    woutt_scr[q, r0:r0 + ch, (2 * d + 1) * w:(2 * d + 2) * w] = (-z_im).astype(BF16)

        for g in range(2):
            u_scr[2 * q + g] = ut_scr[2 * q + g].T.astype(BF16)
        u2 = jnp.concatenate([u_scr[2 * q], u_scr[2 * q + 1]], axis=1)
        summaries = jnp.dot(u2, wsum_scr[...], preferred_element_type=F32)
        for part in range(4):
            s_scr[part, q * rows:(q + 1) * rows, :] = summaries[:, part * w:(part + 1) * w]

    chains = pairs * bsz
    decay = [jnp.concatenate([jnp.broadcast_to(pw_ref[d, part, q][sb:sb + 1], (bsz, w)) for q in range(pairs)], axis=0)
             for d in range(2) for part in range(2)]

    def step(k, xs):
        xf_re, xf_im, xb_re, xb_im = xs
        rf = pl.ds(k, chains, stride=n_blocks)
        rb = pl.ds(n_blocks - 1 - k, chains, stride=n_blocks)
        sf_re, sf_im = s_scr[0, rf, :], s_scr[1, rf, :]
        sb_re, sb_im = s_scr[2, rb, :], s_scr[3, rb, :]
        s_scr[0, rf, :] = xf_re
        s_scr[1, rf, :] = xf_im
        s_scr[2, rb, :] = xb_re
        s_scr[3, rb, :] = xb_im
        nf_re, nf_im = cmul(decay[0], decay[1], xf_re, xf_im)
        nb_re, nb_im = cmul(decay[2], decay[3], xb_re, xb_im)
        return nf_re + sf_re, nf_im + sf_im, nb_re + sb_re, nb_im + sb_im

    zero = jnp.zeros((chains, w), F32)
    lax.fori_loop(0, n_blocks, step, (zero, zero, zero, zero), unroll=2)

    for q in range(pairs):
        entering = jnp.concatenate([s_scr[part, q * rows:(q + 1) * rows, :] for part in range(4)], axis=1)
        carried = _dot_nt(entering.astype(BF16), woutt_scr[q])
        for g in range(2):
            y = (jnp.dot(u_scr[2 * q + g], m_scr[2 * q + g], preferred_element_type=F32)
                 + carried[:, g * blk:(g + 1) * blk])
            y_t = y.T
            for i in range(sb):
                yt_scr[i, (2 * q + g) * ch:(2 * q + g + 1) * ch, :] = y_t[i * ch:(i + 1) * ch, :]

    for i in range(sb):
        y_ref[pl.ds(i, rows, stride=sb), :] = yt_scr[i].T


def _s5(proj, u_block0, b_t, c_t, powers, layer, bsz):
    t = proj.shape[0]
    rows = t // S5_BLOCK
    blk = S5_BLOCK * S5_GROUP_CH
    groups = LANE // S5_GROUP_CH
    pairs = groups // 2
    w = 2 * S5_STATE
    n_tiles = powers.shape[3] // pairs
    placed = pl.BlockSpec((None, 2, 2, pairs, 2, S5_GROUP_CH, w), lambda k: (layer, 0, 0, k, 0, 0, 0))
    return pl.pallas_call(
        functools.partial(_s5_kernel, bsz=bsz),
        grid=(n_tiles,),
        in_specs=[
            pl.BlockSpec((t, LANE), lambda k: (0, u_block0 + k)),
            placed, placed,
            pl.BlockSpec((None, 2, 2, pairs, S5_BLOCK + 1, w), lambda k: (layer, 0, 0, k, 0, 0)),
        ],
        out_specs=pl.BlockSpec((t, LANE), lambda k: (0, k)),
        out_shape=jax.ShapeDtypeStruct((t, n_tiles * LANE), F32),
        scratch_shapes=[
            pltpu.VMEM((groups, blk, rows), F32),
            pltpu.VMEM((groups, rows, blk), BF16),
            pltpu.VMEM((4, pairs * rows, w), F32),
            pltpu.VMEM((S5_BLOCK, LANE, rows), F32),
            pltpu.VMEM((groups, blk, blk), BF16),
            pltpu.VMEM((2 * blk, 4 * w), BF16),
            pltpu.VMEM((pairs, 2 * blk, 4 * w), BF16),
        ],
        compiler_params=_compiler_params(("parallel",)),
        name="s5",
    )(proj, b_t, c_t, powers)


def _s5_params(lam_re, lam_im, log_dt, b_re, b_im, c_re, c_im):
    n_groups = lam_re.shape[1]
    sb, p, c = S5_BLOCK, S5_STATE, S5_GROUP_CH
    l_re, l_im = lam_re.astype(F32), lam_im.astype(F32)
    dt = jnp.exp(log_dt.astype(F32))[..., None]
    mag = jnp.exp(l_re * dt)
    bar_re, bar_im = mag * jnp.cos(l_im * dt), mag * jnp.sin(l_im * dt)
    n_re, n_im = bar_re - 1.0, bar_im
    den = l_re * l_re + l_im * l_im
    k_re = ((n_re * l_re + n_im * l_im) / den)[..., None]
    k_im = ((n_im * l_re - n_re * l_im) / den)[..., None]
    bb_re = k_re * b_re.astype(F32) - k_im * b_im.astype(F32)
    bb_im = k_re * b_im.astype(F32) + k_im * b_re.astype(F32)
    cc_re, cc_im = c_re.astype(F32), c_im.astype(F32)

    steps = jnp.arange(sb + 1, dtype=F32)[:, None, None, None]
    mag_l = jnp.exp(steps * (l_re * dt)[None])
    pw_re, pw_im = mag_l * jnp.cos(steps * (l_im * dt)[None]), mag_l * jnp.sin(steps * (l_im * dt)[None])

    same = jnp.eye(2, dtype=F32)
    def place(t):
        t2 = t.reshape(2, n_groups // 2, 2, c, 1, p) * same[None, None, :, None, :, None]
        return t2.reshape(2, n_groups // 2, 2, c, 2 * p)
    b_t = jnp.stack([place(jnp.swapaxes(bb_re, 2, 3)), place(jnp.swapaxes(bb_im, 2, 3))], axis=1)
    c_t = jnp.stack([place(cc_re), place(cc_im)], axis=1)
    def side_by_side(t):
        t2 = jnp.transpose(t.reshape(sb + 1, 2, n_groups // 2, 2, p), (1, 2, 0, 3, 4))
        return t2.reshape(2, n_groups // 2, sb + 1, 2 * p)
    powers = jnp.stack([side_by_side(pw_re), side_by_side(pw_im)], axis=1)
    return b_t, c_t, powers


def _out_kernel(oa_ref, za_ref, ys_ref, u_ref, zb_ref, ga_ref, gb_ref, x_ref,
                wpa_ref, dsk_ref, wglu_ref, bglu_ref, wpb_ref, bga_ref, bgb_ref, wout_ref, fg_ref,
                o_ref, *, final_norm):
    a = (oa_ref[...] * _silu(za_ref[...])).astype(BF16)
    y_a = jnp.dot(a, wpa_ref[...], preferred_element_type=F32)
    y_s = jax.nn.gelu(ys_ref[...] + u_ref[...] * dsk_ref[...])
    glu = jnp.dot(y_s.astype(BF16), wglu_ref[...], preferred_element_type=F32) + bglu_ref[...]
    y_s = y_s * jax.nn.sigmoid(glu)
    y_b = jnp.dot((y_s * _silu(zb_ref[...])).astype(BF16), wpb_ref[...], preferred_element_type=F32)
    merged = (jax.nn.sigmoid(ga_ref[...] + bga_ref[...]) * y_a
              + jax.nn.sigmoid(gb_ref[...] + bgb_ref[...]) * y_b)
    out = x_ref[...] + jnp.dot(merged.astype(BF16), wout_ref[...], preferred_element_type=F32)
    if final_norm:
        ms = jnp.mean(out * out, axis=-1, keepdims=True)
        out = out * lax.rsqrt(ms + RMS_EPS) * fg_ref[...]
    o_ref[...] = out


def _out_stage(o_a, proj, y_s5, x2, w_pa, d_skip, w_glu, b_glu, w_pb, b_gate, w_out, final_g, *,
               layer, za_block, u_block, zb_block, gate_block0, final_norm):
    t, d = x2.shape
    wa = o_a.shape[1]
    wb = y_s5.shape[1]
    tm = _largest_tile(t, 256, SUBLANE)
    row = lambda width, blk: pl.BlockSpec((tm, width), lambda i: (i, blk))
    const = lambda shape, blk=0: pl.BlockSpec((None,) + shape, lambda i: (layer, 0, blk),
                                              pipeline_mode=pl.Buffered(1))
    return pl.pallas_call(
        functools.partial(_out_kernel, final_norm=final_norm),
        grid=(t // tm,),
        in_specs=[
            row(wa, 0), row(wa, za_block), row(wb, 0), row(wb, u_block), row(wb, zb_block),
            row(d, gate_block0), row(d, gate_block0 + 1), row(d, 0),
            const((wa, d)), const((1, wb)), const((wb, wb)), const((1, wb)), const((wb, d)),
            const((1, d), 0), const((1, d), 1), const((d, d)), const((1, d)),
        ],
        out_specs=pl.BlockSpec((tm, d), lambda i: (i, 0)),
        out_shape=jax.ShapeDtypeStruct((t, d), F32),
        compiler_params=_compiler_params(("parallel",)),
        name="out_stage",
    )(o_a, proj, y_s5, proj, proj, proj, proj, x2, w_pa, d_skip, w_glu, b_glu, w_pb, b_gate, b_gate, w_out,
      final_g)


def _prepare_params(ln_g, w_in, conv_w, a_log, dt_bias, head_norm_g, lam_re, lam_im, log_dt, b_re, b_im,
                    c_re, c_im, d_skip, w_glu, b_glu, w_pa, w_pb, b_gate, w_out, final_g):
    depth, d, _ = w_in.shape
    n_heads = a_log.shape[2]
    wa = n_heads * HEAD_DIM
    wb = d_skip.shape[1]
    n_ba = 4 * n_heads
    o_beta = 4 * wa
    w_cols = _regroup_weights(w_in, o_beta, n_ba)
    w_ba = w_in[:, :, o_beta:o_beta + LANE].astype(BF16)
    pad_rows = lambda p: jnp.pad(p.astype(F32).reshape(depth, 1, n_ba // 2),
                                 ((0, 0), (0, 0), (n_ba // 2, LANE - n_ba)))
    row = lambda p: p.astype(F32).reshape(depth, 1, -1)
    return dict(
        ln_g=row(ln_g), w_cols=w_cols, w_ba=w_ba, conv_w=conv_w.astype(F32), alog=pad_rows(a_log), dtb=pad_rows(dt_bias),
        head_norm_g=row(head_norm_g),
        s5=jax.vmap(_s5_params)(lam_re, lam_im, log_dt, b_re, b_im, c_re, c_im),
        d_skip=row(d_skip), w_glu=w_glu.astype(BF16), b_glu=row(b_glu), w_pa=w_pa.astype(BF16),
        w_pb=w_pb.astype(BF16), b_gate=row(b_gate), w_out=w_out.astype(BF16),
        final_g=jnp.broadcast_to(final_g.astype(F32).reshape(1, 1, d), (depth, 1, d)),
        n_heads=n_heads, wa=wa, wb=wb)


def _layer(x2, bsz, seq, prm, layer, final_norm):
    t, d = x2.shape
    n_heads, wa, wb = prm["n_heads"], prm["wa"], prm["wb"]
    assert wa == wb and seq % (CHUNK * SUBLANE) == 0 and 4 * n_heads <= LANE
    c_qkv, c_za, c_beta, c_alpha, c_u, c_zb = 3 * wa, wa, 2 * n_heads, 2 * n_heads, wb, wb
    n_proj = prm["w_cols"].shape[2]

    proj, ba_logits = _inproj(x2, prm["ln_g"], prm["w_cols"], prm["w_ba"], layer)
    proj3 = proj.reshape(bsz, seq, n_proj)

    gates = _gates(ba_logits, prm["alog"], prm["dtb"], layer, n_heads)
    beta = gates[:, :c_beta].reshape(bsz, seq, 2, n_heads)
    gc = gates[:, c_beta:c_beta + c_alpha].reshape(bsz, seq, 2, n_heads)
    hpb = DELTA_HEADS_PER_PROGRAM
    cols = jnp.stack([gc[:, :, 0], gc[:, :, 1], beta[:, :, 0], beta[:, :, 1]], axis=-1)
    cols = jnp.transpose(cols.reshape(bsz, seq, n_heads // hpb, 4 * hpb), (0, 2, 1, 3))
    rows = jnp.transpose(gc.reshape(bsz, seq // CHUNK, CHUNK, 2, n_heads), (0, 4, 1, 3, 2))
    rows = rows.reshape(bsz, n_heads, seq // CHUNK, 1, 2 * CHUNK)

    o_a = _delta(proj3, prm["conv_w"], cols, rows, prm["head_norm_g"], layer, n_heads)
    o_a = o_a.reshape(t, wa)

    u_off = c_qkv + c_za
    y_s5 = _s5(proj, u_off // LANE, *prm["s5"], layer, bsz)

    return _out_stage(
        o_a, proj, y_s5, x2, prm["w_pa"], prm["d_skip"], prm["w_glu"], prm["b_glu"], prm["w_pb"],
        prm["b_gate"], prm["w_out"], prm["final_g"], layer=layer,
        za_block=c_qkv // wa, u_block=u_off // wb, zb_block=(u_off + c_u) // wb,
        gate_block0=(u_off + c_u + c_zb) // d, final_norm=final_norm)


def kernel(x, ln_g, w_in, conv_w, a_log, dt_bias, head_norm_g, lam_re, lam_im, log_dt, b_re, b_im, c_re, c_im, d_skip, w_glu, b_glu, w_pa, w_pb, b_gate, w_out, final_g):
    bsz, seq, d = x.shape
    depth = ln_g.shape[0]
    prm = _prepare_params(ln_g, w_in, conv_w, a_log, dt_bias, head_norm_g, lam_re, lam_im, log_dt, b_re, b_im,
                          c_re, c_im, d_skip, w_glu, b_glu, w_pa, w_pb, b_gate, w_out, final_g)
    x2 = x.reshape(bsz * seq, d)
    for layer in range(depth):
        x2 = _layer(x2, bsz, seq, prm, layer, final_norm=(layer == depth - 1))
    return x2.reshape(bsz, seq, d)
```

```python
import functools
import math

import jax
import jax.numpy as jnp
from jax import lax
from jax.experimental import pallas as pl
from jax.experimental.pallas import tpu as pltpu

F32 = jnp.float32
BF16 = jnp.bfloat16

LANE = 128
SUBLANE = 8
HEAD_DIM = 128
CHUNK = 64
CONV_K = 5
CONV_PAD = (CONV_K - 1) // 2
DELTA_HEADS_PER_PROGRAM = 2
LEVEL_GROUP_CHUNKS = 8
S5_GROUP_CH = 16
S5_STATE = 64
S5_BLOCK = 16
RMS_EPS = 1e-6
VMEM_LIMIT_BYTES = 56 * 1024 * 1024


def _compiler_params(semantics):
    return pltpu.CompilerParams(dimension_semantics=semantics, vmem_limit_bytes=VMEM_LIMIT_BYTES)


def _silu(x):
    return x * jax.nn.sigmoid(x)


def _largest_tile(n, cap, unit):
    best = unit
    t = unit
    while t <= min(n, cap):
        if n % t == 0:
            best = t
        t += unit
    return best


def _regroup_kernel(a_ref, b_ref, o_ref, *, n_plain, shift):
    @pl.when(pl.program_id(2) < n_plain)
    def _():
        o_ref[...] = a_ref[...].astype(BF16)

    @pl.when(pl.program_id(2) >= n_plain)
    def _():
        both = jnp.concatenate([a_ref[...], b_ref[...]], axis=1)
        o_ref[...] = both[:, shift:shift + o_ref.shape[1]].astype(BF16)


def _regroup_weights(w_in, o_skip, n_skip):
    depth, d, n = w_in.shape
    n_out = n - n_skip
    tn = _largest_tile(math.gcd(o_skip, n_out - o_skip), 1024, LANE)
    tr = _largest_tile(d, 512, SUBLANE)
    return pl.pallas_call(
        functools.partial(_regroup_kernel, n_plain=o_skip // tn, shift=n_skip),
        grid=(depth, d // tr, n_out // tn),
        in_specs=[
            pl.BlockSpec((None, tr, tn), lambda l, r, j: (l, r, j)),
            pl.BlockSpec((None, tr, LANE), lambda l, r, j: (l, r, (j + 1) * (tn // LANE))),
        ],
        out_specs=pl.BlockSpec((None, tr, tn), lambda l, r, j: (l, r, j)),
        out_shape=jax.ShapeDtypeStruct((depth, d, n_out), BF16),
        compiler_params=_compiler_params(("parallel", "parallel", "parallel")),
        name="regroup",
    )(w_in, w_in)


def _inproj_kernel(x_ref, g_ref, w_ref, wba_ref, o_ref, ba_ref, h_ref):
    @pl.when(pl.program_id(1) == 0)
    def _():
        x = x_ref[...]
        ms = jnp.mean(x * x, axis=-1, keepdims=True)
        h_ref[...] = (x * lax.rsqrt(ms + RMS_EPS) * g_ref[...]).astype(BF16)
        ba_ref[...] = jnp.dot(h_ref[...], wba_ref[...], preferred_element_type=F32)

    o_ref[...] = jnp.dot(h_ref[...], w_ref[...], preferred_element_type=F32)


def _inproj(x2, g_rows, w_bf16, w_ba, layer):
    t, d = x2.shape
    n = w_bf16.shape[2]
    tm = _largest_tile(t, 1024, SUBLANE)
    tn = _largest_tile(n, 1280, LANE)
    return pl.pallas_call(
        _inproj_kernel,
        grid=(t // tm, n // tn),
        in_specs=[
            pl.BlockSpec((tm, d), lambda i, j: (i, 0)),
            pl.BlockSpec((None, 1, d), lambda i, j: (layer, 0, 0)),
            pl.BlockSpec((None, d, tn), lambda i, j: (layer, 0, j)),
            pl.BlockSpec((None, d, LANE), lambda i, j: (layer, 0, 0)),
        ],
        out_specs=[pl.BlockSpec((tm, tn), lambda i, j: (i, j)),
                   pl.BlockSpec((tm, LANE), lambda i, j: (i, 0))],
        out_shape=[jax.ShapeDtypeStruct((t, n), F32), jax.ShapeDtypeStruct((t, LANE), F32)],
        scratch_shapes=[pltpu.VMEM((tm, d), BF16)],
        compiler_params=_compiler_params(("parallel", "arbitrary")),
        name="inproj",
    )(x2, g_rows, w_bf16, w_ba)


def _gates_kernel(ba_ref, alog_ref, dtb_ref, o_ref, *, n_heads):
    x = ba_ref[...]
    tm = x.shape[0]
    lane = lax.broadcasted_iota(jnp.int32, x.shape, 1)
    beta = jax.nn.sigmoid(x)
    z = x + dtb_ref[...]
    softplus = jnp.maximum(z, 0.0) + jnp.log1p(jnp.exp(-jnp.abs(z)))
    g = -jnp.exp(alog_ref[...]) * softplus
    r = lax.broadcasted_iota(jnp.int32, (tm, tm), 0)
    c = lax.broadcasted_iota(jnp.int32, (tm, tm), 1)
    shift = int(math.log2(CHUNK))
    same_chunk = (r >> shift) == (c >> shift)
    m_prefix = jnp.where(same_chunk & (c <= r), 1.0, 0.0).astype(F32)
    m_suffix = jnp.where(same_chunk & (c >= r), 1.0, 0.0).astype(F32)
    g_prefix = jnp.dot(m_prefix, g, precision=lax.Precision.HIGHEST, preferred_element_type=F32)
    g_suffix = jnp.dot(m_suffix, g, precision=lax.Precision.HIGHEST, preferred_element_type=F32)
    gc = jnp.where(lane < 3 * n_heads, g_prefix, g_suffix)
    o_ref[...] = jnp.where(lane < 2 * n_heads, beta, gc)


def _gates(ba_logits, alog_rows, dtb_rows, layer, n_heads):
    t = ba_logits.shape[0]
    tm = _largest_tile(t, 512, CHUNK)
    return pl.pallas_call(
        functools.partial(_gates_kernel, n_heads=n_heads),
        grid=(t // tm,),
        in_specs=[
            pl.BlockSpec((tm, LANE), lambda i: (i, 0)),
            pl.BlockSpec((None, 1, LANE), lambda i: (layer, 0, 0)),
            pl.BlockSpec((None, 1, LANE), lambda i: (layer, 0, 0)),
        ],
        out_specs=pl.BlockSpec((tm, LANE), lambda i: (i, 0)),
        out_shape=jax.ShapeDtypeStruct((t, LANE), F32),
        compiler_params=_compiler_params(("parallel",)),
        name="gates",
    )(ba_logits, alog_rows, dtb_rows)


def _dot_nt(a, b):
    return lax.dot_general(a, b, (((1,), (1,)), ((), ())), preferred_element_type=F32)


def _delta_kernel(q_ref, k_ref, v_ref, wq_ref, wk_ref, wv_ref, cols_ref, rows_ref, hg_ref, o_ref,
                  pad_scr, r_scr, mp_scr, wq_scr, kdt_scr, qk_scr, ob_scr, s_scr):
    seq = q_ref.shape[0]
    hpb = q_ref.shape[1] // HEAD_DIM
    n_chunks = seq // CHUNK
    halo = SUBLANE
    c2 = 2 * CHUNK
    operands = ((q_ref, wq_ref), (k_ref, wk_ref), (v_ref, wv_ref))
    for j in range(len(operands)):
        pad_scr[j, 0:halo, :] = jnp.zeros((halo, HEAD_DIM), F32)
        pad_scr[j, seq + halo:seq + 2 * halo, :] = jnp.zeros((halo, HEAD_DIM), F32)

    def conv_silu(j, w, r0):
        acc = None
        for i in range(CONV_K):
            term = pad_scr[j, pl.ds(r0 + halo + i - CONV_PAD, CHUNK), :] * w[i:i + 1, :]
            acc = term if acc is None else acc + term
        return _silu(acc)

    def l2norm(y):
        return y * lax.rsqrt(jnp.sum(y * y, axis=-1, keepdims=True) + RMS_EPS)

    top_rows = lax.broadcasted_iota(jnp.int32, (c2, c2), 0) < CHUNK
    rw = lax.broadcasted_iota(jnp.int32, (CHUNK, c2), 0)
    cw = lax.broadcasted_iota(jnp.int32, (CHUNK, c2), 1)
    fwd_lanes = cw < CHUNK
    cs = jnp.where(fwd_lanes, cw, cw - CHUNK)
    incl_wide = (fwd_lanes & (rw >= cs)) | (~fwd_lanes & (rw <= cs))
    strict_wide = (fwd_lanes & (rw > cs)) | (~fwd_lanes & (rw < cs))
    eye_wide = jnp.where(rw == cs, 1.0, 0.0).astype(F32)

    def block_diag(cat):
        return jnp.concatenate([jnp.where(fwd_lanes, cat, 0.0), jnp.where(fwd_lanes, 0.0, cat)], axis=0)

    for p in range(hpb):
        lanes = slice(p * HEAD_DIM, (p + 1) * HEAD_DIM)

        def fill(n, carry, lanes=lanes):
            r0 = pl.multiple_of(n * CHUNK, CHUNK)
            for j, (src_ref, _) in enumerate(operands):
                pad_scr[j, pl.ds(r0 + halo, CHUNK), :] = src_ref[pl.ds(r0, CHUNK), lanes]
            return carry

        lax.fori_loop(0, n_chunks, fill, 0, unroll=2)

        def setup(n, carry, p=p, lanes=lanes):
            r0 = pl.multiple_of(n * CHUNK, CHUNK)
            q = l2norm(conv_silu(0, wq_ref[:, lanes], r0)) * (HEAD_DIM ** -0.5)
            k = l2norm(conv_silu(1, wk_ref[:, lanes], r0))
            v = conv_silu(2, wv_ref[:, lanes], r0)
            k16 = k.astype(BF16)
            k2_16 = jnp.concatenate([k16, k16], axis=0)
            gram = _dot_nt(jnp.concatenate([q.astype(BF16), k16], axis=0), k2_16)
            qk_wide, kk_wide = gram[:CHUNK], gram[CHUNK:]
            cols = cols_ref[pl.ds(r0, CHUNK), :]
            across = lambda c: jnp.broadcast_to(cols[:, 4 * p + c:4 * p + c + 1], (CHUNK, c2))
            gc2 = jnp.concatenate([across(0), across(1)], axis=0)
            be2 = jnp.concatenate([across(2), across(3)], axis=0)
            gc_row2 = rows_ref[p, n]
            k2 = jnp.concatenate([k, k], axis=0)
            v2 = jnp.concatenate([v, v], axis=0)
            q2 = jnp.concatenate([q, q], axis=0)
            gc_wide = jnp.where(fwd_lanes, gc2[:CHUNK], gc2[CHUNK:])
            be_wide = jnp.where(fwd_lanes, be2[:CHUNK], be2[CHUNK:])
            decay_wide = jnp.exp(gc_wide - gc_row2)
            mp_scr[p, n, 0:CHUNK, :] = -(be_wide * kk_wide * jnp.where(strict_wide, decay_wide, 0.0))
            mp_scr[p, n, CHUNK:c2, :] = eye_wide
            eg2 = jnp.exp(gc2)
            r_scr[p, n] = jnp.concatenate([v2 * be2, k2 * (be2 * eg2)], axis=1)
            qg2 = (q2 * eg2).astype(BF16)
            wq_scr[p, 0, n, CHUNK:c2, :] = qg2[:CHUNK]
            wq_scr[p, 1, n, CHUNK:c2, :] = qg2[CHUNK:]
            g_last2 = jnp.where(top_rows, gc_row2[:, CHUNK - 1:CHUNK], gc_row2[:, CHUNK:CHUNK + 1])
            kdec2 = k2 * jnp.exp(g_last2 - gc2)
            kdt_scr[p, n] = kdec2.T.astype(BF16)
            qk_scr[p, n] = (qk_wide * jnp.where(incl_wide, decay_wide, 0.0)).astype(BF16)
            return carry

        lax.fori_loop(0, n_chunks, setup, 0, unroll=2)

    group = _largest_tile(n_chunks, LEVEL_GROUP_CHUNKS, 1)

    def group_tiles(it):
        return [(p, it * group + g) for g in range(group) for p in range(hpb)]

    def level(_, carry):
        def per_group(it, carry):
            tiles = group_tiles(it)
            outs = []
            for p, n in tiles:
                mp = mp_scr[p, n]
                outs.append(jnp.dot(mp.astype(BF16), block_diag(mp[:CHUNK]).astype(BF16),
                                    preferred_element_type=F32))
            for (p, n), out in zip(tiles, outs):
                mp_scr[p, n, 0:CHUNK, :] = out[:CHUNK]
                mp_scr[p, n, CHUNK:c2, :] += out[CHUNK:]
            return carry

        return lax.fori_loop(0, n_chunks // group, per_group, carry)

    lax.fori_loop(0, int(math.log2(CHUNK)), level, 0)

    def apply_inverse(it, carry):
        tiles = group_tiles(it)
        outs = [jnp.dot(block_diag(mp_scr[p, n, CHUNK:c2, :]).astype(BF16), r_scr[p, n].astype(BF16),
                        preferred_element_type=F32) for p, n in tiles]
        for (p, n), uw in zip(tiles, outs):
            r_scr[p, n] = uw
            w2 = uw[:, HEAD_DIM:].astype(BF16)
            wq_scr[p, 0, n, 0:CHUNK, :] = w2[:CHUNK]
            wq_scr[p, 1, n, 0:CHUNK, :] = w2[CHUNK:]
        return carry

    lax.fori_loop(0, n_chunks // group, apply_inverse, 0)

    s_scr[...] = jnp.zeros(s_scr.shape, F32)
    zero_half = jnp.zeros((CHUNK, HEAD_DIM), BF16)
    chains = [(p, d) for p in range(hpb) for d in range(2)]

    def recurrence(i, carry):
        chunk_of = (i, n_chunks - 1 - i)
        states = [s_scr[p, d] for p, d in chains]
        ws = [jnp.dot(wq_scr[p, d, chunk_of[d]], s.astype(BF16), preferred_element_type=F32)
              for (p, d), s in zip(chains, states)]
        new_states = []
        for (p, d), s, w in zip(chains, states, ws):
            n = chunk_of[d]
            r0 = pl.multiple_of(n * CHUNK, CHUNK)
            v_new = (r_scr[p, n, d * CHUNK:(d + 1) * CHUNK, 0:HEAD_DIM] - w[:CHUNK]).astype(BF16)
            v_pad = jnp.concatenate([v_new, zero_half] if d == 0 else [zero_half, v_new], axis=0)
            o = w[CHUNK:] + jnp.dot(qk_scr[p, n], v_pad, preferred_element_type=F32)
            if d == 0:
                o_ref[pl.ds(r0, CHUNK), p * HEAD_DIM:(p + 1) * HEAD_DIM] = o
            else:
                ob_scr[p, pl.ds(r0, CHUNK), :] = o
            gc_row2 = rows_ref[p, n]
            g_last = gc_row2[:, CHUNK - 1:CHUNK] if d == 0 else gc_row2[:, CHUNK:CHUNK + 1]
            new_states.append(s * jnp.exp(g_last) + jnp.dot(kdt_scr[p, n], v_pad, preferred_element_type=F32))
        for (p, d), s in zip(chains, new_states):
            s_scr[p, d] = s
        return carry

    lax.fori_loop(0, n_chunks, recurrence, 0)

    def head_norm(n, carry):
        r0 = pl.multiple_of(n * CHUNK, CHUNK)
        for p in range(hpb):
            lanes = slice(p * HEAD_DIM, (p + 1) * HEAD_DIM)
            o = o_ref[pl.ds(r0, CHUNK), lanes] + ob_scr[p, pl.ds(r0, CHUNK), :]
            ms = jnp.mean(o * o, axis=-1, keepdims=True)
            o_ref[pl.ds(r0, CHUNK), lanes] = o * lax.rsqrt(ms + RMS_EPS) * hg_ref[...]
        return carry

    lax.fori_loop(0, n_chunks, head_norm, 0, unroll=2)


def _delta(proj3, conv_w, cols, rows, hg_rows, layer, n_heads):
    bsz, seq, _ = proj3.shape
    n_chunks = seq // CHUNK
    hpb = DELTA_HEADS_PER_PROGRAM
    width = hpb * HEAD_DIM
    n_blocks = n_heads // hpb
    tok = lambda off: pl.BlockSpec((None, seq, width), lambda b, h: (b, 0, off + h))
    cw = lambda off: pl.BlockSpec((None, CONV_K, width), lambda b, h: (layer, 0, off + h))
    return pl.pallas_call(
        _delta_kernel,
        grid=(bsz, n_blocks),
        in_specs=[
            tok(0), tok(n_blocks), tok(2 * n_blocks),
            cw(0), cw(n_blocks), cw(2 * n_blocks),
            pl.BlockSpec((None, None, seq, 4 * hpb), lambda b, h: (b, h, 0, 0)),
            pl.BlockSpec((None, hpb, n_chunks, 1, 2 * CHUNK), lambda b, h: (b, h, 0, 0, 0)),
            pl.BlockSpec((None, 1, HEAD_DIM), lambda b, h: (layer, 0, 0)),
        ],
        out_specs=pl.BlockSpec((None, seq, width), lambda b, h: (b, 0, h)),
        out_shape=jax.ShapeDtypeStruct((bsz, seq, n_heads * HEAD_DIM), F32),
        scratch_shapes=[
            pltpu.VMEM((3, seq + 2 * SUBLANE, HEAD_DIM), F32),
            pltpu.VMEM((hpb, n_chunks, 2 * CHUNK, 2 * HEAD_DIM), F32),
            pltpu.VMEM((hpb, n_chunks, 2 * CHUNK, 2 * CHUNK), F32),
            pltpu.VMEM((hpb, 2, n_chunks, 2 * CHUNK, HEAD_DIM), BF16),
            pltpu.VMEM((hpb, n_chunks, HEAD_DIM, 2 * CHUNK), BF16),
            pltpu.VMEM((hpb, n_chunks, CHUNK, 2 * CHUNK), BF16),
            pltpu.VMEM((hpb, seq, HEAD_DIM), F32),
            pltpu.VMEM((hpb, 2, HEAD_DIM, HEAD_DIM), F32),
        ],
        compiler_params=_compiler_params(("parallel", "parallel")),
        name="delta",
    )(proj3, proj3, proj3, conv_w, conv_w, conv_w, cols, rows, hg_rows)


def _s5_kernel(u_ref, bt_ref, ct_ref, pw_ref, y_ref,
               ut_scr, u_scr, s_scr, yt_scr, m_scr, wsum_scr, woutt_scr, *, bsz):
    sb, ch = S5_BLOCK, S5_GROUP_CH
    rows = u_ref.shape[0] // sb
    groups = LANE // ch
    w = 2 * S5_STATE
    blk = sb * ch

    def cmul(a_re, a_im, b_re, b_im):
        return a_re * b_re - a_im * b_im, a_re * b_im + a_im * b_re

    def lag_strip(d, q, h, lags):
        z = [cmul(ct_ref[d, 0, q, h], ct_ref[d, 1, q, h], pw_ref[d, 0, q][l:l + 1], pw_ref[d, 1, q][l:l + 1])
             for l in lags]
        z_re = jnp.concatenate([t[0] for t in z], axis=0)
        z_im = jnp.concatenate([t[1] for t in z], axis=0)
        over_states = functools.partial(lax.dot_general, dimension_numbers=(((1,), (1,)), ((), ())),
                                        precision=lax.Precision.HIGHEST, preferred_element_type=F32)
        return over_states(bt_ref[d, 0, q, h], z_re) - over_states(bt_ref[d, 1, q, h], z_im)

    lane = lax.broadcasted_iota(jnp.int32, (ch, blk), 1)
    for g in range(groups):
        k_f = lag_strip(0, g // 2, g % 2, range(sb))
        k_b = lag_strip(1, g // 2, g % 2, range(sb - 1, -1, -1))
        for j in range(sb):
            fwd = jnp.where(lane >= j * ch, pltpu.roll(k_f, j * ch, axis=1), 0.0)
            bwd = jnp.where(lane < (j + 1) * ch, pltpu.roll(k_b, (blk - (sb - 1 - j) * ch) % blk, axis=1), 0.0)
            m_scr[g, j * ch:(j + 1) * ch, :] = (fwd + bwd).astype(BF16)

    for j in range(sb):
        t = u_ref[pl.ds(j, rows, stride=sb), :].T
        for g in range(groups):
            ut_scr[g, j * ch:(j + 1) * ch, :] = t[g * ch:(g + 1) * ch, :]

    n_blocks = rows // bsz
    pairs = groups // 2
    for q in range(pairs):
        pw = [[pw_ref[d, part, q] for part in range(2)] for d in range(2)]
        for h in range(2):
            bt = [[bt_ref[d, part, q, h] for part in range(2)] for d in range(2)]
            ct = [[ct_ref[d, part, q, h] for part in range(2)] for d in range(2)]
            for j in range(sb):
                r0 = h * blk + j * ch
                for d, (l_sum, l_out) in enumerate(((sb - 1 - j, j + 1), (j, sb - j))):
                    s_re, s_im = cmul(bt[d][0], bt[d][1], pw[d][0][l_sum:l_sum + 1], pw[d][1][l_sum:l_sum + 1])
                    z_re, z_im = cmul(ct[d][0], ct[d][1], pw[d][0][l_out:l_out + 1], pw[d][1][l_out:l_out + 1])
                    wsum_scr[r0:r0 + ch, 2 * d * w:(2 * d + 1) * w] = s_re.astype(BF16)
                    wsum_scr[r0:r0 + ch, (2 * d + 1) * w:(2 * d + 2) * w] = s_im.astype(BF16)
                    woutt_scr[q, r0:r0 + ch, 2 * d * w:(2 * d + 1) * w] = z_re.astype(BF16)
                    woutt_scr[q, r0:r0 + ch, (2 * d + 1) * w:(2 * d + 2) * w] = (-z_im).astype(BF16)

        for g in range(2):
            u_scr[2 * q + g] = ut_scr[2 * q + g].T.astype(BF16)
        u2 = jnp.concatenate([u_scr[2 * q], u_scr[2 * q + 1]], axis=1)
        summaries = jnp.dot(u2, wsum_scr[...], preferred_element_type=F32)
        for part in range(4):
            s_scr[part, q * rows:(q + 1) * rows, :] = summaries[:, part * w:(part + 1) * w]

    chains = pairs * bsz
    decay = [jnp.concatenate([jnp.broadcast_to(pw_ref[d, part, q][sb:sb + 1], (bsz, w)) for q in range(pairs)], axis=0)
             for d in range(2) for part in range(2)]

    def step(k, xs):
        xf_re, xf_im, xb_re, xb_im = xs
        rf = pl.ds(k, chains, stride=n_blocks)
        rb = pl.ds(n_blocks - 1 - k, chains, stride=n_blocks)
        sf_re, sf_im = s_scr[0, rf, :], s_scr[1, rf, :]
        sb_re, sb_im = s_scr[2, rb, :], s_scr[3, rb, :]
        s_scr[0, rf, :] = xf_re
        s_scr[1, rf, :] = xf_im
        s_scr[2, rb, :] = xb_re
        s_scr[3, rb, :] = xb_im
        nf_re, nf_im = cmul(decay[0], decay[1], xf_re, xf_im)
        nb_re, nb_im = cmul(decay[2], decay[3], xb_re, xb_im)
        return nf_re + sf_re, nf_im + sf_im, nb_re + sb_re, nb_im + sb_im

    zero = jnp.zeros((chains, w), F32)
    lax.fori_loop(0, n_blocks, step, (zero, zero, zero, zero), unroll=2)

    for q in range(pairs):
        entering = jnp.concatenate([s_scr[part, q * rows:(q + 1) * rows, :] for part in range(4)], axis=1)
        carried = _dot_nt(entering.astype(BF16), woutt_scr[q])
        for g in range(2):
            y = (jnp.dot(u_scr[2 * q + g], m_scr[2 * q + g], preferred_element_type=F32)
                 + carried[:, g * blk:(g + 1) * blk])
            y_t = y.T
            for i in range(sb):
                yt_scr[i, (2 * q + g) * ch:(2 * q + g + 1) * ch, :] = y_t[i * ch:(i + 1) * ch, :]

    for i in range(sb):
        y_ref[pl.ds(i, rows, stride=sb), :] = yt_scr[i].T


def _s5(proj, u_block0, b_t, c_t, powers, layer, bsz):
    t = proj.shape[0]
    rows = t // S5_BLOCK
    blk = S5_BLOCK * S5_GROUP_CH
    groups = LANE // S5_GROUP_CH
    pairs = groups // 2
    w = 2 * S5_STATE
    n_tiles = powers.shape[3] // pairs
    placed = pl.BlockSpec((None, 2, 2, pairs, 2, S5_GROUP_CH, w), lambda k: (layer, 0, 0, k, 0, 0, 0))
    return pl.pallas_call(
        functools.partial(_s5_kernel, bsz=bsz),
        grid=(n_tiles,),
        in_specs=[
            pl.BlockSpec((t, LANE), lambda k: (0, u_block0 + k)),
            placed, placed,
            pl.BlockSpec((None, 2, 2, pairs, S5_BLOCK + 1, w), lambda k: (layer, 0, 0, k, 0, 0)),
        ],
        out_specs=pl.BlockSpec((t, LANE), lambda k: (0, k)),
        out_shape=jax.ShapeDtypeStruct((t, n_tiles * LANE), F32),
        scratch_shapes=[
            pltpu.VMEM((groups, blk, rows), F32),
            pltpu.VMEM((groups, rows, blk), BF16),
            pltpu.VMEM((4, pairs * rows, w), F32),
            pltpu.VMEM((S5_BLOCK, LANE, rows), F32),
            pltpu.VMEM((groups, blk, blk), BF16),
            pltpu.VMEM((2 * blk, 4 * w), BF16),
            pltpu.VMEM((pairs, 2 * blk, 4 * w), BF16),
        ],
        compiler_params=_compiler_params(("parallel",)),
        name="s5",
    )(proj, b_t, c_t, powers)


def _s5_params(lam_re, lam_im, log_dt, b_re, b_im, c_re, c_im):
    n_groups = lam_re.shape[1]
    sb, p, c = S5_BLOCK, S5_STATE, S5_GROUP_CH
    l_re, l_im = lam_re.astype(F32), lam_im.astype(F32)
    dt = jnp.exp(log_dt.astype(F32))[..., None]
    mag = jnp.exp(l_re * dt)
    bar_re, bar_im = mag * jnp.cos(l_im * dt), mag * jnp.sin(l_im * dt)
    n_re, n_im = bar_re - 1.0, bar_im
    den = l_re * l_re + l_im * l_im
    k_re = ((n_re * l_re + n_im * l_im) / den)[..., None]
    k_im = ((n_im * l_re - n_re * l_im) / den)[..., None]
    bb_re = k_re * b_re.astype(F32) - k_im * b_im.astype(F32)
    bb_im = k_re * b_im.astype(F32) + k_im * b_re.astype(F32)
    cc_re, cc_im = c_re.astype(F32), c_im.astype(F32)

    steps = jnp.arange(sb + 1, dtype=F32)[:, None, None, None]
    mag_l = jnp.exp(steps * (l_re * dt)[None])
    pw_re, pw_im = mag_l * jnp.cos(steps * (l_im * dt)[None]), mag_l * jnp.sin(steps * (l_im * dt)[None])

    same = jnp.eye(2, dtype=F32)
    def place(t):
        t2 = t.reshape(2, n_groups // 2, 2, c, 1, p) * same[None, None, :, None, :, None]
        return t2.reshape(2, n_groups // 2, 2, c, 2 * p)
    b_t = jnp.stack([place(jnp.swapaxes(bb_re, 2, 3)), place(jnp.swapaxes(bb_im, 2, 3))], axis=1)
    c_t = jnp.stack([place(cc_re), place(cc_im)], axis=1)
    def side_by_side(t):
        t2 = jnp.transpose(t.reshape(sb + 1, 2, n_groups // 2, 2, p), (1, 2, 0, 3, 4))
        return t2.reshape(2, n_groups // 2, sb + 1, 2 * p)
    powers = jnp.stack([side_by_side(pw_re), side_by_side(pw_im)], axis=1)
    return b_t, c_t, powers


def _out_kernel(oa_ref, za_ref, ys_ref, u_ref, zb_ref, ga_ref, gb_ref, x_ref,
                wpa_ref, dsk_ref, wglu_ref, bglu_ref, wpb_ref, bga_ref, bgb_ref, wout_ref, fg_ref,
                o_ref, *, final_norm):
    a = (oa_ref[...] * _silu(za_ref[...])).astype(BF16)
    y_a = jnp.dot(a, wpa_ref[...], preferred_element_type=F32)
    y_s = jax.nn.gelu(ys_ref[...] + u_ref[...] * dsk_ref[...])
    glu = jnp.dot(y_s.astype(BF16), wglu_ref[...], preferred_element_type=F32) + bglu_ref[...]
    y_s = y_s * jax.nn.sigmoid(glu)
    y_b = jnp.dot((y_s * _silu(zb_ref[...])).astype(BF16), wpb_ref[...], preferred_element_type=F32)
    merged = (jax.nn.sigmoid(ga_ref[...] + bga_ref[...]) * y_a
              + jax.nn.sigmoid(gb_ref[...] + bgb_ref[...]) * y_b)
    out = x_ref[...] + jnp.dot(merged.astype(BF16), wout_ref[...], preferred_element_type=F32)
    if final_norm:
        ms = jnp.mean(out * out, axis=-1, keepdims=True)
        out = out * lax.rsqrt(ms + RMS_EPS) * fg_ref[...]
    o_ref[...] = out


def _out_stage(o_a, proj, y_s5, x2, w_pa, d_skip, w_glu, b_glu, w_pb, b_gate, w_out, final_g, *,
               layer, za_block, u_block, zb_block, gate_block0, final_norm):
    t, d = x2.shape
    wa = o_a.shape[1]
    wb = y_s5.shape[1]
    tm = _largest_tile(t, 256, SUBLANE)
    row = lambda width, blk: pl.BlockSpec((tm, width), lambda i: (i, blk))
    const = lambda shape, blk=0: pl.BlockSpec((None,) + shape, lambda i: (layer, 0, blk),
                                              pipeline_mode=pl.Buffered(1))
    return pl.pallas_call(
        functools.partial(_out_kernel, final_norm=final_norm),
        grid=(t // tm,),
        in_specs=[
            row(wa, 0), row(wa, za_block), row(wb, 0), row(wb, u_block), row(wb, zb_block),
            row(d, gate_block0), row(d, gate_block0 + 1), row(d, 0),
            const((wa, d)), const((1, wb)), const((wb, wb)), const((1, wb)), const((wb, d)),
            const((1, d), 0), const((1, d), 1), const((d, d)), const((1, d)),
        ],
        out_specs=pl.BlockSpec((tm, d), lambda i: (i, 0)),
        out_shape=jax.ShapeDtypeStruct((t, d), F32),
        compiler_params=_compiler_params(("parallel",)),
        name="out_stage",
    )(o_a, proj, y_s5, proj, proj, proj, proj, x2, w_pa, d_skip, w_glu, b_glu, w_pb, b_gate, b_gate, w_out,
      final_g)


def _prepare_params(ln_g, w_in, conv_w, a_log, dt_bias, head_norm_g, lam_re, lam_im, log_dt, b_re, b_im,
                    c_re, c_im, d_skip, w_glu, b_glu, w_pa, w_pb, b_gate, w_out, final_g):
    depth, d, _ = w_in.shape
    n_heads = a_log.shape[2]
    wa = n_heads * HEAD_DIM
    wb = d_skip.shape[1]
    n_ba = 4 * n_heads
    o_beta = 4 * wa
    w_in16 = w_in.astype(BF16)
    w_cols = _regroup_weights(w_in16, o_beta, n_ba)
    w_ba = w_in16[:, :, o_beta:o_beta + LANE]
    pad_rows = lambda p: jnp.pad(p.astype(F32).reshape(depth, 1, n_ba // 2),
                                 ((0, 0), (0, 0), (n_ba // 2, LANE - n_ba)))
    row = lambda p: p.astype(F32).reshape(depth, 1, -1)
    return dict(
        ln_g=row(ln_g), w_cols=w_cols, w_ba=w_ba, conv_w=conv_w.astype(F32), alog=pad_rows(a_log), dtb=pad_rows(dt_bias),
        head_norm_g=row(head_norm_g),
        s5=jax.vmap(_s5_params)(lam_re, lam_im, log_dt, b_re, b_im, c_re, c_im),
        d_skip=row(d_skip), w_glu=w_glu.astype(BF16), b_glu=row(b_glu), w_pa=w_pa.astype(BF16),
        w_pb=w_pb.astype(BF16), b_gate=row(b_gate), w_out=w_out.astype(BF16),
        final_g=jnp.broadcast_to(final_g.astype(F32).reshape(1, 1, d), (depth, 1, d)),
        n_heads=n_heads, wa=wa, wb=wb)


def _layer(x2, bsz, seq, prm, layer, final_norm):
    t, d = x2.shape
    n_heads, wa, wb = prm["n_heads"], prm["wa"], prm["wb"]
    assert wa == wb and seq % (CHUNK * SUBLANE) == 0 and 4 * n_heads <= LANE
    c_qkv, c_za, c_beta, c_alpha, c_u, c_zb = 3 * wa, wa, 2 * n_heads, 2 * n_heads, wb, wb
    n_proj = prm["w_cols"].shape[2]

    proj, ba_logits = _inproj(x2, prm["ln_g"], prm["w_cols"], prm["w_ba"], layer)
    proj3 = proj.reshape(bsz, seq, n_proj)

    gates = _gates(ba_logits, prm["alog"], prm["dtb"], layer, n_heads)
    beta = gates[:, :c_beta].reshape(bsz, seq, 2, n_heads)
    gc = gates[:, c_beta:c_beta + c_alpha].reshape(bsz, seq, 2, n_heads)
    hpb = DELTA_HEADS_PER_PROGRAM
    cols = jnp.stack([gc[:, :, 0], gc[:, :, 1], beta[:, :, 0], beta[:, :, 1]], axis=-1)
    cols = jnp.transpose(cols.reshape(bsz, seq, n_heads // hpb, 4 * hpb), (0, 2, 1, 3))
    rows = jnp.transpose(gc.reshape(bsz, seq // CHUNK, CHUNK, 2, n_heads), (0, 4, 1, 3, 2))
    rows = rows.reshape(bsz, n_heads, seq // CHUNK, 1, 2 * CHUNK)

    o_a = _delta(proj3, prm["conv_w"], cols, rows, prm["head_norm_g"], layer, n_heads)
    o_a = o_a.reshape(t, wa)

    u_off = c_qkv + c_za
    y_s5 = _s5(proj, u_off // LANE, *prm["s5"], layer, bsz)

    return _out_stage(
        o_a, proj, y_s5, x2, prm["w_pa"], prm["d_skip"], prm["w_glu"], prm["b_glu"], prm["w_pb"],
        prm["b_gate"], prm["w_out"], prm["final_g"], layer=layer,
        za_block=c_qkv // wa, u_block=u_off // wb, zb_block=(u_off + c_u) // wb,
        gate_block0=(u_off + c_u + c_zb) // d, final_norm=final_norm)


def kernel(x, ln_g, w_in, conv_w, a_log, dt_bias, head_norm_g, lam_re, lam_im, log_dt, b_re, b_im, c_re, c_im, d_skip, w_glu, b_glu, w_pa, w_pb, b_gate, w_out, final_g):
    bsz, seq, d = x.shape
    depth = ln_g.shape[0]
    prm = _prepare_params(ln_g, w_in, conv_w, a_log, dt_bias, head_norm_g, lam_re, lam_im, log_dt, b_re, b_im,
                          c_re, c_im, d_skip, w_glu, b_glu, w_pa, w_pb, b_gate, w_out, final_g)
    x2 = x.reshape(bsz * seq, d)
    for layer in range(depth):
        x2 = _layer(x2, bsz, seq, prm, layer, final_norm=(layer == depth - 1))
    return x2.reshape(bsz, seq, d)
```

```python
import functools
import math

import jax
import jax.numpy as jnp
from jax import lax
from jax.experimental import pallas as pl
from jax.experimental.pallas import tpu as pltpu

F32 = jnp.float32
BF16 = jnp.bfloat16

LANE = 128
SUBLANE = 8
HEAD_DIM = 128
CHUNK = 64
CONV_K = 5
CONV_PAD = (CONV_K - 1) // 2
DELTA_HEADS_PER_PROGRAM = 2
LEVEL_GROUP_CHUNKS = 8
S5_GROUP_CH = 16
S5_STATE = 64
S5_BLOCK = 16
RMS_EPS = 1e-6
VMEM_LIMIT_BYTES = 56 * 1024 * 1024


def _compiler_params(semantics):
    return pltpu.CompilerParams(dimension_semantics=semantics, vmem_limit_bytes=VMEM_LIMIT_BYTES)


def _silu(x):
    return x * jax.nn.sigmoid(x)


def _largest_tile(n, cap, unit):
    best = unit
    t = unit
    while t <= min(n, cap):
        if n % t == 0:
            best = t
        t += unit
    return best


def _inproj_tile(o_skip, n_shifted):
    return _largest_tile(math.gcd(o_skip, n_shifted), 1280, LANE)


def _shift_kernel(a_ref, b_ref, o_ref, *, shift):
    both = jnp.concatenate([a_ref[...], b_ref[...]], axis=1)
    o_ref[...] = both[:, shift:shift + o_ref.shape[1]]


def _shift_weights(w, o_skip, n_skip):
    depth, d, n = w.shape
    n_out = n - o_skip - n_skip
    tn = _inproj_tile(o_skip, n_out)
    tr = _largest_tile(d, 512, 2 * SUBLANE)
    first = o_skip // tn
    return pl.pallas_call(
        functools.partial(_shift_kernel, shift=n_skip),
        grid=(depth, d // tr, n_out // tn),
        in_specs=[
            pl.BlockSpec((None, tr, tn), lambda l, r, j: (l, r, first + j)),
            pl.BlockSpec((None, tr, LANE), lambda l, r, j: (l, r, (first + j + 1) * (tn // LANE))),
        ],
        out_specs=pl.BlockSpec((None, tr, tn), lambda l, r, j: (l, r, j)),
        out_shape=jax.ShapeDtypeStruct((depth, d, n_out), w.dtype),
        compiler_params=_compiler_params(("parallel", "parallel", "parallel")),
        name="shift_weights",
    )(w, w)


def _inproj_kernel(x_ref, g_ref, wa_ref, wb_ref, wba_ref, o_ref, ba_ref, h_ref, *, n_plain):
    @pl.when(pl.program_id(1) == 0)
    def _():
        x = x_ref[...]
        ms = jnp.mean(x * x, axis=-1, keepdims=True)
        h_ref[...] = (x * lax.rsqrt(ms + RMS_EPS) * g_ref[...]).astype(BF16)
        ba_ref[...] = jnp.dot(h_ref[...], wba_ref[...], preferred_element_type=F32)

    @pl.when(pl.program_id(1) < n_plain)
    def _():
        o_ref[...] = jnp.dot(h_ref[...], wa_ref[...], preferred_element_type=F32)

    @pl.when(pl.program_id(1) >= n_plain)
    def _():
        o_ref[...] = jnp.dot(h_ref[...], wb_ref[...], preferred_element_type=F32)


def _inproj(x2, g_rows, w_all, w_shifted, w_ba, o_skip, layer):
    t, d = x2.shape
    n_s = w_shifted.shape[2]
    tm = _largest_tile(t, 1024, SUBLANE)
    tn = _inproj_tile(o_skip, n_s)
    n_plain = o_skip // tn
    return pl.pallas_call(
        functools.partial(_inproj_kernel, n_plain=n_plain),
        grid=(t // tm, n_plain + n_s // tn),
        in_specs=[
            pl.BlockSpec((tm, d), lambda i, j: (i, 0)),
            pl.BlockSpec((None, 1, d), lambda i, j: (layer, 0, 0)),
            pl.BlockSpec((None, d, tn), lambda i, j: (layer, 0, jnp.minimum(j, n_plain - 1))),
            pl.BlockSpec((None, d, tn), lambda i, j: (layer, 0, jnp.maximum(j - n_plain, 0))),
            pl.BlockSpec((None, d, LANE), lambda i, j: (layer, 0, 0)),
        ],
        out_specs=[pl.BlockSpec((tm, tn), lambda i, j: (i, j)),
                   pl.BlockSpec((tm, LANE), lambda i, j: (i, 0))],
        out_shape=[jax.ShapeDtypeStruct((t, o_skip + n_s), F32), jax.ShapeDtypeStruct((t, LANE), F32)],
        scratch_shapes=[pltpu.VMEM((tm, d), BF16)],
        compiler_params=_compiler_params(("parallel", "arbitrary")),
        name="inproj",
    )(x2, g_rows, w_all, w_shifted, w_ba)


def _gates_kernel(ba_ref, alog_ref, dtb_ref, o_ref, *, n_heads):
    lane = lax.broadcasted_iota(jnp.int32, (CHUNK, LANE), 1)
    r = lax.broadcasted_iota(jnp.int32, (CHUNK, CHUNK), 0)
    c = lax.broadcasted_iota(jnp.int32, (CHUNK, CHUNK), 1)
    m_prefix = jnp.where(c <= r, 1.0, 0.0).astype(F32)
    m_suffix = jnp.where(c >= r, 1.0, 0.0).astype(F32)
    within_chunk = functools.partial(jnp.dot, precision=lax.Precision.HIGHEST, preferred_element_type=F32)
    neg_rate = -jnp.exp(alog_ref[...])
    for r0 in range(0, ba_ref.shape[0], CHUNK):
        x = ba_ref[r0:r0 + CHUNK, :]
        z = x + dtb_ref[...]
        g = neg_rate * (jnp.maximum(z, 0.0) + jnp.log1p(jnp.exp(-jnp.abs(z))))
        gc = jnp.where(lane < 3 * n_heads, within_chunk(m_prefix, g), within_chunk(m_suffix, g))
        o_ref[r0:r0 + CHUNK, :] = jnp.where(lane < 2 * n_heads, jax.nn.sigmoid(x), gc)


def _gates(ba_logits, alog_rows, dtb_rows, layer, n_heads):
    t = ba_logits.shape[0]
    tm = _largest_tile(t, 512, CHUNK)
    return pl.pallas_call(
        functools.partial(_gates_kernel, n_heads=n_heads),
        grid=(t // tm,),
        in_specs=[
            pl.BlockSpec((tm, LANE), lambda i: (i, 0)),
            pl.BlockSpec((None, 1, LANE), lambda i: (layer, 0, 0)),
            pl.BlockSpec((None, 1, LANE), lambda i: (layer, 0, 0)),
        ],
        out_specs=pl.BlockSpec((tm, LANE), lambda i: (i, 0)),
        out_shape=jax.ShapeDtypeStruct((t, LANE), F32),
        compiler_params=_compiler_params(("parallel",)),
        name="gates",
    )(ba_logits, alog_rows, dtb_rows)


def _dot_nt(a, b):
    return lax.dot_general(a, b, (((1,), (1,)), ((), ())), preferred_element_type=F32)


def _delta_kernel(q_ref, k_ref, v_ref, wq_ref, wk_ref, wv_ref, cols_ref, rows_ref, hg_ref, o_ref,
                  pad_scr, r_scr, mp_scr, wq_scr, kdt_scr, qk_scr, ob_scr, s_scr):
    seq = q_ref.shape[0]
    hpb = q_ref.shape[1] // HEAD_DIM
    n_chunks = seq // CHUNK
    halo = SUBLANE
    c2 = 2 * CHUNK
    operands = ((q_ref, wq_ref), (k_ref, wk_ref), (v_ref, wv_ref))
    for j in range(len(operands)):
        pad_scr[j, 0:halo, :] = jnp.zeros((halo, HEAD_DIM), F32)
        pad_scr[j, seq + halo:seq + 2 * halo, :] = jnp.zeros((halo, HEAD_DIM), F32)

    def conv_silu(j, w, r0):
        acc = None
        for i in range(CONV_K):
            term = pad_scr[j, pl.ds(r0 + halo + i - CONV_PAD, CHUNK), :] * w[i:i + 1, :]
            acc = term if acc is None else acc + term
        return _silu(acc)

    def l2norm(y):
        return y * lax.rsqrt(jnp.sum(y * y, axis=-1, keepdims=True) + RMS_EPS)

    top_rows = lax.broadcasted_iota(jnp.int32, (c2, c2), 0) < CHUNK
    rw = lax.broadcasted_iota(jnp.int32, (CHUNK, c2), 0)
    cw = lax.broadcasted_iota(jnp.int32, (CHUNK, c2), 1)
    fwd_lanes = cw < CHUNK
    cs = jnp.where(fwd_lanes, cw, cw - CHUNK)
    incl_wide = (fwd_lanes & (rw >= cs)) | (~fwd_lanes & (rw <= cs))
    strict_wide = (fwd_lanes & (rw > cs)) | (~fwd_lanes & (rw < cs))
    eye_wide = jnp.where(rw == cs, 1.0, 0.0).astype(F32)

    def block_diag(cat):
        return jnp.concatenate([jnp.where(fwd_lanes, cat, 0.0), jnp.where(fwd_lanes, 0.0, cat)], axis=0)

    for p in range(hpb):
        lanes = slice(p * HEAD_DIM, (p + 1) * HEAD_DIM)

        def fill(n, carry, lanes=lanes):
            r0 = pl.multiple_of(n * CHUNK, CHUNK)
            for j, (src_ref, _) in enumerate(operands):
                pad_scr[j, pl.ds(r0 + halo, CHUNK), :] = src_ref[pl.ds(r0, CHUNK), lanes]
            return carry

        lax.fori_loop(0, n_chunks, fill, 0, unroll=2)

        def setup(n, carry, p=p, lanes=lanes):
            r0 = pl.multiple_of(n * CHUNK, CHUNK)
            q = l2norm(conv_silu(0, wq_ref[:, lanes], r0)) * (HEAD_DIM ** -0.5)
            k = l2norm(conv_silu(1, wk_ref[:, lanes], r0))
            v = conv_silu(2, wv_ref[:, lanes], r0)
            k16 = k.astype(BF16)
            k2_16 = jnp.concatenate([k16, k16], axis=0)
            gram = _dot_nt(jnp.concatenate([q.astype(BF16), k16], axis=0), k2_16)
            qk_wide, kk_wide = gram[:CHUNK], gram[CHUNK:]
            cols = cols_ref[pl.ds(r0, CHUNK), :]
            across = lambda c: jnp.broadcast_to(cols[:, 4 * p + c:4 * p + c + 1], (CHUNK, c2))
            gc2 = jnp.concatenate([across(0), across(1)], axis=0)
            be2 = jnp.concatenate([across(2), across(3)], axis=0)
            gc_row2 = rows_ref[p, n]
            k2 = jnp.concatenate([k, k], axis=0)
            v2 = jnp.concatenate([v, v], axis=0)
            q2 = jnp.concatenate([q, q], axis=0)
            gc_wide = jnp.where(fwd_lanes, gc2[:CHUNK], gc2[CHUNK:])
            be_wide = jnp.where(fwd_lanes, be2[:CHUNK], be2[CHUNK:])
            decay_wide = jnp.exp(gc_wide - gc_row2)
            mp_scr[p, n, 0:CHUNK, :] = -(be_wide * kk_wide * jnp.where(strict_wide, decay_wide, 0.0))
            mp_scr[p, n, CHUNK:c2, :] = eye_wide
            eg2 = jnp.exp(gc2)
            r_scr[p, n] = jnp.concatenate([v2 * be2, k2 * (be2 * eg2)], axis=1)
            qg2 = (q2 * eg2).astype(BF16)
            wq_scr[p, 0, n, CHUNK:c2, :] = qg2[:CHUNK]
            wq_scr[p, 1, n, CHUNK:c2, :] = qg2[CHUNK:]
            g_last2 = jnp.where(top_rows, gc_row2[:, CHUNK - 1:CHUNK], gc_row2[:, CHUNK:CHUNK + 1])
            kdec2 = k2 * jnp.exp(g_last2 - gc2)
            kdt_scr[p, n] = kdec2.T.astype(BF16)
            qk_scr[p, n] = (qk_wide * jnp.where(incl_wide, decay_wide, 0.0)).astype(BF16)
            return carry

        lax.fori_loop(0, n_chunks, setup, 0, unroll=2)

    group = _largest_tile(n_chunks, LEVEL_GROUP_CHUNKS, 1)

    def group_tiles(it):
        return [(p, it * group + g) for g in range(group) for p in range(hpb)]

    def level(_, carry):
        def per_group(it, carry):
            tiles = group_tiles(it)
            outs = []
            for p, n in tiles:
                mp = mp_scr[p, n]
                outs.append(jnp.dot(mp.astype(BF16), block_diag(mp[:CHUNK]).astype(BF16),
                                    preferred_element_type=F32))
            for (p, n), out in zip(tiles, outs):
                mp_scr[p, n, 0:CHUNK, :] = out[:CHUNK]
                mp_scr[p, n, CHUNK:c2, :] += out[CHUNK:]
            return carry

        return lax.fori_loop(0, n_chunks // group, per_group, carry)

    lax.fori_loop(0, int(math.log2(CHUNK)), level, 0)

    def apply_inverse(it, carry):
        tiles = group_tiles(it)
        outs = [jnp.dot(block_diag(mp_scr[p, n, CHUNK:c2, :]).astype(BF16), r_scr[p, n].astype(BF16),
                        preferred_element_type=F32) for p, n in tiles]
        for (p, n), uw in zip(tiles, outs):
            r_scr[p, n] = uw
            w2 = uw[:, HEAD_DIM:].astype(BF16)
            wq_scr[p, 0, n, 0:CHUNK, :] = w2[:CHUNK]
            wq_scr[p, 1, n, 0:CHUNK, :] = w2[CHUNK:]
        return carry

    lax.fori_loop(0, n_chunks // group, apply_inverse, 0)

    s_scr[...] = jnp.zeros(s_scr.shape, F32)
    zero_half = jnp.zeros((CHUNK, HEAD_DIM), BF16)
    chains = [(p, d) for p in range(hpb) for d in range(2)]

    def recurrence(i, carry):
        chunk_of = (i, n_chunks - 1 - i)
        states = [s_scr[p, d] for p, d in chains]
        ws = [jnp.dot(wq_scr[p, d, chunk_of[d]], s.astype(BF16), preferred_element_type=F32)
              for (p, d), s in zip(chains, states)]
        new_states = []
        for (p, d), s, w in zip(chains, states, ws):
            n = chunk_of[d]
            r0 = pl.multiple_of(n * CHUNK, CHUNK)
            v_new = (r_scr[p, n, d * CHUNK:(d + 1) * CHUNK, 0:HEAD_DIM] - w[:CHUNK]).astype(BF16)
            v_pad = jnp.concatenate([v_new, zero_half] if d == 0 else [zero_half, v_new], axis=0)
            o = w[CHUNK:] + jnp.dot(qk_scr[p, n], v_pad, preferred_element_type=F32)
            if d == 0:
                o_ref[pl.ds(r0, CHUNK), p * HEAD_DIM:(p + 1) * HEAD_DIM] = o
            else:
                ob_scr[p, pl.ds(r0, CHUNK), :] = o
            gc_row2 = rows_ref[p, n]
            g_last = gc_row2[:, CHUNK - 1:CHUNK] if d == 0 else gc_row2[:, CHUNK:CHUNK + 1]
            new_states.append(s * jnp.exp(g_last) + jnp.dot(kdt_scr[p, n], v_pad, preferred_element_type=F32))
        for (p, d), s in zip(chains, new_states):
            s_scr[p, d] = s
        return carry

    lax.fori_loop(0, n_chunks, recurrence, 0)

    def head_norm(n, carry):
        r0 = pl.multiple_of(n * CHUNK, CHUNK)
        for p in range(hpb):
            lanes = slice(p * HEAD_DIM, (p + 1) * HEAD_DIM)
            o = o_ref[pl.ds(r0, CHUNK), lanes] + ob_scr[p, pl.ds(r0, CHUNK), :]
            ms = jnp.mean(o * o, axis=-1, keepdims=True)
            o_ref[pl.ds(r0, CHUNK), lanes] = o * lax.rsqrt(ms + RMS_EPS) * hg_ref[...]
        return carry

    lax.fori_loop(0, n_chunks, head_norm, 0, unroll=4)


def _delta(proj3, conv_w, cols, rows, hg_rows, layer, n_heads):
    bsz, seq, _ = proj3.shape
    n_chunks = seq // CHUNK
    hpb = DELTA_HEADS_PER_PROGRAM
    width = hpb * HEAD_DIM
    n_blocks = n_heads // hpb
    tok = lambda off: pl.BlockSpec((None, seq, width), lambda b, h: (b, 0, off + h))
    cw = lambda off: pl.BlockSpec((None, CONV_K, width), lambda b, h: (layer, 0, off + h))
    return pl.pallas_call(
        _delta_kernel,
        grid=(bsz, n_blocks),
        in_specs=[
            tok(0), tok(n_blocks), tok(2 * n_blocks),
            cw(0), cw(n_blocks), cw(2 * n_blocks),
            pl.BlockSpec((None, None, seq, 4 * hpb), lambda b, h: (b, h, 0, 0)),
            pl.BlockSpec((None, hpb, n_chunks, 1, 2 * CHUNK), lambda b, h: (b, h, 0, 0, 0)),
            pl.BlockSpec((None, 1, HEAD_DIM), lambda b, h: (layer, 0, 0)),
        ],
        out_specs=pl.BlockSpec((None, seq, width), lambda b, h: (b, 0, h)),
        out_shape=jax.ShapeDtypeStruct((bsz, seq, n_heads * HEAD_DIM), F32),
        scratch_shapes=[
            pltpu.VMEM((3, seq + 2 * SUBLANE, HEAD_DIM), F32),
            pltpu.VMEM((hpb, n_chunks, 2 * CHUNK, 2 * HEAD_DIM), F32),
            pltpu.VMEM((hpb, n_chunks, 2 * CHUNK, 2 * CHUNK), F32),
            pltpu.VMEM((hpb, 2, n_chunks, 2 * CHUNK, HEAD_DIM), BF16),
            pltpu.VMEM((hpb, n_chunks, HEAD_DIM, 2 * CHUNK), BF16),
            pltpu.VMEM((hpb, n_chunks, CHUNK, 2 * CHUNK), BF16),
            pltpu.VMEM((hpb, seq, HEAD_DIM), F32),
            pltpu.VMEM((hpb, 2, HEAD_DIM, HEAD_DIM), F32),
        ],
        compiler_params=_compiler_params(("parallel", "parallel")),
        name="delta",
    )(proj3, proj3, proj3, conv_w, conv_w, conv_w, cols, rows, hg_rows)


def _s5_kernel(u_ref, bt_ref, ct_ref, pw_ref, y_ref,
               ut_scr, u_scr, s_scr, yt_scr, m_scr, wsum_scr, woutt_scr, *, bsz):
    sb, ch = S5_BLOCK, S5_GROUP_CH
    rows = u_ref.shape[0] // sb
    groups = LANE // ch
    w = 2 * S5_STATE
    blk = sb * ch

    def cmul(a_re, a_im, b_re, b_im):
        return a_re * b_re - a_im * b_im, a_re * b_im + a_im * b_re

    def lag_strip(d, q, h, lags):
        z = [cmul(ct_ref[d, 0, q, h], ct_ref[d, 1, q, h], pw_ref[d, 0, q][l:l + 1], pw_ref[d, 1, q][l:l + 1])
             for l in lags]
        z_re = jnp.concatenate([t[0] for t in z], axis=0)
        z_im = jnp.concatenate([t[1] for t in z], axis=0)
        over_states = functools.partial(lax.dot_general, dimension_numbers=(((1,), (1,)), ((), ())),
                                        precision=lax.Precision.HIGHEST, preferred_element_type=F32)
        return over_states(bt_ref[d, 0, q, h], z_re) - over_states(bt_ref[d, 1, q, h], z_im)

    lane = lax.broadcasted_iota(jnp.int32, (ch, blk), 1)
    for g in range(groups):
        k_f = lag_strip(0, g // 2, g % 2, range(sb))
        k_b = lag_strip(1, g // 2, g % 2, range(sb - 1, -1, -1))
        for j in range(sb):
            fwd = jnp.where(lane >= j * ch, pltpu.roll(k_f, j * ch, axis=1), 0.0)
            bwd = jnp.where(lane < (j + 1) * ch, pltpu.roll(k_b, (blk - (sb - 1 - j) * ch) % blk, axis=1), 0.0)
            m_scr[g, j * ch:(j + 1) * ch, :] = (fwd + bwd).astype(BF16)

    for j in range(sb):
        t = u_ref[pl.ds(j, rows, stride=sb), :].T
        for g in range(groups):
            ut_scr[g, j * ch:(j + 1) * ch, :] = t[g * ch:(g + 1) * ch, :]

    n_blocks = rows // bsz
    pairs = groups // 2
    for q in range(pairs):
        pw = [[pw_ref[d, part, q] for part in range(2)] for d in range(2)]
        for h in range(2):
            bt = [[bt_ref[d, part, q, h] for part in range(2)] for d in range(2)]
            ct = [[ct_ref[d, part, q, h] for part in range(2)] for d in range(2)]
            for j in range(sb):
                r0 = h * blk + j * ch
                for d, (l_sum, l_out) in enumerate(((sb - 1 - j, j + 1), (j, sb - j))):
                    s_re, s_im = cmul(bt[d][0], bt[d][1], pw[d][0][l_sum:l_sum + 1], pw[d][1][l_sum:l_sum + 1])
                    z_re, z_im = cmul(ct[d][0], ct[d][1], pw[d][0][l_out:l_out + 1], pw[d][1][l_out:l_out + 1])
                    wsum_scr[r0:r0 + ch, 2 * d * w:(2 * d + 1) * w] = s_re.astype(BF16)
                    wsum_scr[r0:r0 + ch, (2 * d + 1) * w:(2 * d + 2) * w] = s_im.astype(BF16)
                    woutt_scr[q, r0:r0 + ch, 2 * d * w:(2 * d + 1) * w] = z_re.astype(BF16)
                    woutt_scr[q, r0:r0 + ch, (2 * d + 1) * w:(2 * d + 2) * w] = (-z_im).astype(BF16)

        for g in range(2):
            u_scr[2 * q + g] = ut_scr[2 * q + g].T.astype(BF16)
        u2 = jnp.concatenate([u_scr[2 * q], u_scr[2 * q + 1]], axis=1)
        summaries = jnp.dot(u2, wsum_scr[...], preferred_element_type=F32)
        for part in range(4):
            s_scr[part, q * rows:(q + 1) * rows, :] = summaries[:, part * w:(part + 1) * w]

    chains = pairs * bsz
    decay = [jnp.concatenate([jnp.broadcast_to(pw_ref[d, part, q][sb:sb + 1], (bsz, w)) for q in range(pairs)], axis=0)
             for d in range(2) for part in range(2)]

    def step(k, xs):
        xf_re, xf_im, xb_re, xb_im = xs
        rf = pl.ds(k, chains, stride=n_blocks)
        rb = pl.ds(n_blocks - 1 - k, chains, stride=n_blocks)
        sf_re, sf_im = s_scr[0, rf, :], s_scr[1, rf, :]
        sb_re, sb_im = s_scr[2, rb, :], s_scr[3, rb, :]
        s_scr[0, rf, :] = xf_re
        s_scr[1, rf, :] = xf_im
        s_scr[2, rb, :] = xb_re
        s_scr[3, rb, :] = xb_im
        nf_re, nf_im = cmul(decay[0], decay[1], xf_re, xf_im)
        nb_re, nb_im = cmul(decay[2], decay[3], xb_re, xb_im)
        return nf_re + sf_re, nf_im + sf_im, nb_re + sb_re, nb_im + sb_im

    zero = jnp.zeros((chains, w), F32)
    lax.fori_loop(0, n_blocks, step, (zero, zero, zero, zero), unroll=2)

    for q in range(pairs):
        entering = jnp.concatenate([s_scr[part, q * rows:(q + 1) * rows, :] for part in range(4)], axis=1)
        carried = _dot_nt(entering.astype(BF16), woutt_scr[q])
        for g in range(2):
            y = (jnp.dot(u_scr[2 * q + g], m_scr[2 * q + g], preferred_element_type=F32)
                 + carried[:, g * blk:(g + 1) * blk])
            y_t = y.T
            for i in range(sb):
                yt_scr[i, (2 * q + g) * ch:(2 * q + g + 1) * ch, :] = y_t[i * ch:(i + 1) * ch, :]

    for i in range(sb):
        y_ref[pl.ds(i, rows, stride=sb), :] = yt_scr[i].T


def _s5(proj, u_block0, b_t, c_t, powers, layer, bsz):
    t = proj.shape[0]
    rows = t // S5_BLOCK
    blk = S5_BLOCK * S5_GROUP_CH
    groups = LANE // S5_GROUP_CH
    pairs = groups // 2
    w = 2 * S5_STATE
    n_tiles = powers.shape[3] // pairs
    placed = pl.BlockSpec((None, 2, 2, pairs, 2, S5_GROUP_CH, w), lambda k: (layer, 0, 0, k, 0, 0, 0))
    return pl.pallas_call(
        functools.partial(_s5_kernel, bsz=bsz),
        grid=(n_tiles,),
        in_specs=[
            pl.BlockSpec((t, LANE), lambda k: (0, u_block0 + k)),
            placed, placed,
            pl.BlockSpec((None, 2, 2, pairs, S5_BLOCK + 1, w), lambda k: (layer, 0, 0, k, 0, 0)),
        ],
        out_specs=pl.BlockSpec((t, LANE), lambda k: (0, k)),
        out_shape=jax.ShapeDtypeStruct((t, n_tiles * LANE), F32),
        scratch_shapes=[
            pltpu.VMEM((groups, blk, rows), F32),
            pltpu.VMEM((groups, rows, blk), BF16),
            pltpu.VMEM((4, pairs * rows, w), F32),
            pltpu.VMEM((S5_BLOCK, LANE, rows), F32),
            pltpu.VMEM((groups, blk, blk), BF16),
            pltpu.VMEM((2 * blk, 4 * w), BF16),
            pltpu.VMEM((pairs, 2 * blk, 4 * w), BF16),
        ],
        compiler_params=_compiler_params(("parallel",)),
        name="s5",
    )(proj, b_t, c_t, powers)


def _s5_params(lam_re, lam_im, log_dt, b_re, b_im, c_re, c_im):
    n_groups = lam_re.shape[1]
    sb, p, c = S5_BLOCK, S5_STATE, S5_GROUP_CH
    l_re, l_im = lam_re.astype(F32), lam_im.astype(F32)
    dt = jnp.exp(log_dt.astype(F32))[..., None]
    mag = jnp.exp(l_re * dt)
    bar_re, bar_im = mag * jnp.cos(l_im * dt), mag * jnp.sin(l_im * dt)
    n_re, n_im = bar_re - 1.0, bar_im
    den = l_re * l_re + l_im * l_im
    k_re = ((n_re * l_re + n_im * l_im) / den)[..., None]
    k_im = ((n_im * l_re - n_re * l_im) / den)[..., None]
    bb_re = k_re * b_re.astype(F32) - k_im * b_im.astype(F32)
    bb_im = k_re * b_im.astype(F32) + k_im * b_re.astype(F32)
    cc_re, cc_im = c_re.astype(F32), c_im.astype(F32)

    steps = jnp.arange(sb + 1, dtype=F32)[:, None, None, None]
    mag_l = jnp.exp(steps * (l_re * dt)[None])
    pw_re, pw_im = mag_l * jnp.cos(steps * (l_im * dt)[None]), mag_l * jnp.sin(steps * (l_im * dt)[None])

    same = jnp.eye(2, dtype=F32)
    def place(t):
        t2 = t.reshape(2, n_groups // 2, 2, c, 1, p) * same[None, None, :, None, :, None]
        return t2.reshape(2, n_groups // 2, 2, c, 2 * p)
    b_t = jnp.stack([place(jnp.swapaxes(bb_re, 2, 3)), place(jnp.swapaxes(bb_im, 2, 3))], axis=1)
    c_t = jnp.stack([place(cc_re), place(cc_im)], axis=1)
    def side_by_side(t):
        t2 = jnp.transpose(t.reshape(sb + 1, 2, n_groups // 2, 2, p), (1, 2, 0, 3, 4))
        return t2.reshape(2, n_groups // 2, sb + 1, 2 * p)
    powers = jnp.stack([side_by_side(pw_re), side_by_side(pw_im)], axis=1)
    return b_t, c_t, powers


def _out_kernel(oa_ref, za_ref, ys_ref, u_ref, zb_ref, ga_ref, gb_ref, x_ref,
                wpa_ref, dsk_ref, wglu_ref, bglu_ref, wpb_ref, bga_ref, bgb_ref, wout_ref, fg_ref,
                o_ref, *, final_norm):
    a = (oa_ref[...] * _silu(za_ref[...])).astype(BF16)
    y_a = jnp.dot(a, wpa_ref[...], preferred_element_type=F32)
    y_s = jax.nn.gelu(ys_ref[...] + u_ref[...] * dsk_ref[...])
    glu = jnp.dot(y_s.astype(BF16), wglu_ref[...], preferred_element_type=F32) + bglu_ref[...]
    y_s = y_s * jax.nn.sigmoid(glu)
    y_b = jnp.dot((y_s * _silu(zb_ref[...])).astype(BF16), wpb_ref[...], preferred_element_type=F32)
    merged = (jax.nn.sigmoid(ga_ref[...] + bga_ref[...]) * y_a
              + jax.nn.sigmoid(gb_ref[...] + bgb_ref[...]) * y_b)
    out = x_ref[...] + jnp.dot(merged.astype(BF16), wout_ref[...], preferred_element_type=F32)
    if final_norm:
        ms = jnp.mean(out * out, axis=-1, keepdims=True)
        out = out * lax.rsqrt(ms + RMS_EPS) * fg_ref[...]
    o_ref[...] = out


def _out_stage(o_a, proj, y_s5, x2, w_pa, d_skip, w_glu, b_glu, w_pb, b_gate, w_out, final_g, *,
               layer, za_block, u_block, zb_block, gate_block0, final_norm):
    t, d = x2.shape
    wa = o_a.shape[1]
    wb = y_s5.shape[1]
    tm = _largest_tile(t, 256, SUBLANE)
    row = lambda width, blk: pl.BlockSpec((tm, width), lambda i: (i, blk))
    const = lambda shape, blk=0: pl.BlockSpec((None,) + shape, lambda i: (layer, 0, blk),
                                              pipeline_mode=pl.Buffered(1))
    return pl.pallas_call(
        functools.partial(_out_kernel, final_norm=final_norm),
        grid=(t // tm,),
        in_specs=[
            row(wa, 0), row(wa, za_block), row(wb, 0), row(wb, u_block), row(wb, zb_block),
            row(d, gate_block0), row(d, gate_block0 + 1), row(d, 0),
            const((wa, d)), const((1, wb)), const((wb, wb)), const((1, wb)), const((wb, d)),
            const((1, d), 0), const((1, d), 1), const((d, d)), const((1, d)),
        ],
        out_specs=pl.BlockSpec((tm, d), lambda i: (i, 0)),
        out_shape=jax.ShapeDtypeStruct((t, d), F32),
        compiler_params=_compiler_params(("parallel",)),
        name="out_stage",
    )(o_a, proj, y_s5, proj, proj, proj, proj, x2, w_pa, d_skip, w_glu, b_glu, w_pb, b_gate, b_gate, w_out,
      final_g)


def _prepare_params(ln_g, w_in, conv_w, a_log, dt_bias, head_norm_g, lam_re, lam_im, log_dt, b_re, b_im,
                    c_re, c_im, d_skip, w_glu, b_glu, w_pa, w_pb, b_gate, w_out, final_g):
    depth, d, _ = w_in.shape
    n_heads = a_log.shape[2]
    wa = n_heads * HEAD_DIM
    wb = d_skip.shape[1]
    n_ba = 4 * n_heads
    o_beta = 4 * wa
    w_in16 = w_in.astype(BF16)
    w_shifted = _shift_weights(w_in16, o_beta, n_ba)
    w_ba = w_in16[:, :, o_beta:o_beta + LANE]
    pad_rows = lambda p: jnp.pad(p.astype(F32).reshape(depth, 1, n_ba // 2),
                                 ((0, 0), (0, 0), (n_ba // 2, LANE - n_ba)))
    row = lambda p: p.astype(F32).reshape(depth, 1, -1)
    return dict(
        ln_g=row(ln_g), w_in16=w_in16, w_shifted=w_shifted, w_ba=w_ba, o_beta=o_beta, conv_w=conv_w.astype(F32), alog=pad_rows(a_log), dtb=pad_rows(dt_bias),
        head_norm_g=row(head_norm_g),
        s5=jax.vmap(_s5_params)(lam_re, lam_im, log_dt, b_re, b_im, c_re, c_im),
        d_skip=row(d_skip), w_glu=w_glu.astype(BF16), b_glu=row(b_glu), w_pa=w_pa.astype(BF16),
        w_pb=w_pb.astype(BF16), b_gate=row(b_gate), w_out=w_out.astype(BF16),
        final_g=jnp.broadcast_to(final_g.astype(F32).reshape(1, 1, d), (depth, 1, d)),
        n_heads=n_heads, wa=wa, wb=wb)


def _layer(x2, bsz, seq, prm, layer, final_norm):
    t, d = x2.shape
    n_heads, wa, wb = prm["n_heads"], prm["wa"], prm["wb"]
    assert wa == wb and seq % (CHUNK * SUBLANE) == 0 and 4 * n_heads <= LANE
    c_qkv, c_za, c_beta, c_alpha, c_u, c_zb = 3 * wa, wa, 2 * n_heads, 2 * n_heads, wb, wb
    n_proj = prm["o_beta"] + prm["w_shifted"].shape[2]

    proj, ba_logits = _inproj(x2, prm["ln_g"], prm["w_in16"], prm["w_shifted"], prm["w_ba"], prm["o_beta"],
                              layer)
    proj3 = proj.reshape(bsz, seq, n_proj)

    gates = _gates(ba_logits, prm["alog"], prm["dtb"], layer, n_heads)
    beta = gates[:, :c_beta].reshape(bsz, seq, 2, n_heads)
    gc = gates[:, c_beta:c_beta + c_alpha].reshape(bsz, seq, 2, n_heads)
    hpb = DELTA_HEADS_PER_PROGRAM
    cols = jnp.stack([gc[:, :, 0], gc[:, :, 1], beta[:, :, 0], beta[:, :, 1]], axis=-1)
    cols = jnp.transpose(cols.reshape(bsz, seq, n_heads // hpb, 4 * hpb), (0, 2, 1, 3))
    rows = jnp.transpose(gc.reshape(bsz, seq // CHUNK, CHUNK, 2, n_heads), (0, 4, 1, 3, 2))
    rows = rows.reshape(bsz, n_heads, seq // CHUNK, 1, 2 * CHUNK)

    o_a = _delta(proj3, prm["conv_w"], cols, rows, prm["head_norm_g"], layer, n_heads)
    o_a = o_a.reshape(t, wa)

    u_off = c_qkv + c_za
    y_s5 = _s5(proj, u_off // LANE, *prm["s5"], layer, bsz)

    return _out_stage(
        o_a, proj, y_s5, x2, prm["w_pa"], prm["d_skip"], prm["w_glu"], prm["b_glu"], prm["w_pb"],
        prm["b_gate"], prm["w_out"], prm["final_g"], layer=layer,
        za_block=c_qkv // wa, u_block=u_off // wb, zb_block=(u_off + c_u) // wb,
        gate_block0=(u_off + c_u + c_zb) // d, final_norm=final_norm)


def kernel(x, ln_g, w_in, conv_w, a_log, dt_bias, head_norm_g, lam_re, lam_im, log_dt, b_re, b_im, c_re, c_im, d_skip, w_glu, b_glu, w_pa, w_pb, b_gate, w_out, final_g):
    bsz, seq, d = x.shape
    depth = ln_g.shape[0]
    prm = _prepare_params(ln_g, w_in, conv_w, a_log, dt_bias, head_norm_g, lam_re, lam_im, log_dt, b_re, b_im,
                          c_re, c_im, d_skip, w_glu, b_glu, w_pa, w_pb, b_gate, w_out, final_g)
    x2 = x.reshape(bsz * seq, d)
    for layer in range(depth):
        x2 = _layer(x2, bsz, seq, prm, layer, final_norm=(layer == depth - 1))
    return x2.reshape(bsz, seq, d)
```

```python
import functools
import math

import jax
import jax.numpy as jnp
from jax import lax
from jax.experimental import pallas as pl
from jax.experimental.pallas import tpu as pltpu

F32 = jnp.float32
BF16 = jnp.bfloat16

LANE = 128
SUBLANE = 8
HEAD_DIM = 128
CHUNK = 64
CONV_K = 5
CONV_PAD = (CONV_K - 1) // 2
DELTA_HEADS_PER_PROGRAM = 2
LEVEL_GROUP_CHUNKS = 8
S5_GROUP_CH = 16
S5_STATE = 64
S5_BLOCK = 16
RMS_EPS = 1e-6
VMEM_LIMIT_BYTES = 56 * 1024 * 1024


def _compiler_params(semantics):
    return pltpu.CompilerParams(dimension_semantics=semantics, vmem_limit_bytes=VMEM_LIMIT_BYTES)


def _silu(x):
    return x * jax.nn.sigmoid(x)


def _largest_tile(n, cap, unit):
    best = unit
    t = unit
    while t <= min(n, cap):
        if n % t == 0:
            best = t
        t += unit
    return best


def _regroup_kernel(a_ref, b_ref, o_ref, *, n_plain, shift):
    @pl.when(pl.program_id(2) < n_plain)
    def _():
        o_ref[...] = a_ref[...]

    @pl.when(pl.program_id(2) >= n_plain)
    def _():
        both = jnp.concatenate([a_ref[...], b_ref[...]], axis=1)
        o_ref[...] = both[:, shift:shift + o_ref.shape[1]]


def _regroup_weights(w, o_skip, n_skip):
    depth, d, n = w.shape
    n_out = n - n_skip
    tn = _largest_tile(math.gcd(o_skip, n_out - o_skip), 1024, LANE)
    tr = _largest_tile(d, 2048, 2 * SUBLANE)
    return pl.pallas_call(
        functools.partial(_regroup_kernel, n_plain=o_skip // tn, shift=n_skip),
        grid=(depth, d // tr, n_out // tn),
        in_specs=[
            pl.BlockSpec((None, tr, tn), lambda l, r, j: (l, r, j)),
            pl.BlockSpec((None, tr, LANE), lambda l, r, j: (l, r, (j + 1) * (tn // LANE))),
        ],
        out_specs=pl.BlockSpec((None, tr, tn), lambda l, r, j: (l, r, j)),
        out_shape=jax.ShapeDtypeStruct((depth, d, n_out), w.dtype),
        compiler_params=_compiler_params(("parallel", "parallel", "parallel")),
        name="regroup",
    )(w, w)


def _inproj_kernel(x_ref, g_ref, w_ref, wba_ref, o_ref, ba_ref, h_ref):
    @pl.when(pl.program_id(1) == 0)
    def _():
        x = x_ref[...]
        ms = jnp.mean(x * x, axis=-1, keepdims=True)
        h_ref[...] = (x * lax.rsqrt(ms + RMS_EPS) * g_ref[...]).astype(BF16)
        ba_ref[...] = jnp.dot(h_ref[...], wba_ref[...], preferred_element_type=F32)

    o_ref[...] = jnp.dot(h_ref[...], w_ref[...], preferred_element_type=F32)


def _inproj(x2, g_rows, w_bf16, w_ba, layer):
    t, d = x2.shape
    n = w_bf16.shape[2]
    tm = _largest_tile(t, 1024, SUBLANE)
    tn = _largest_tile(n, 1280, LANE)
    return pl.pallas_call(
        _inproj_kernel,
        grid=(t // tm, n // tn),
        in_specs=[
            pl.BlockSpec((tm, d), lambda i, j: (i, 0)),
            pl.BlockSpec((None, 1, d), lambda i, j: (layer, 0, 0)),
            pl.BlockSpec((None, d, tn), lambda i, j: (layer, 0, j)),
            pl.BlockSpec((None, d, LANE), lambda i, j: (layer, 0, 0)),
        ],
        out_specs=[pl.BlockSpec((tm, tn), lambda i, j: (i, j)),
                   pl.BlockSpec((tm, LANE), lambda i, j: (i, 0))],
        out_shape=[jax.ShapeDtypeStruct((t, n), F32), jax.ShapeDtypeStruct((t, LANE), F32)],
        scratch_shapes=[pltpu.VMEM((tm, d), BF16)],
        compiler_params=_compiler_params(("parallel", "arbitrary")),
        name="inproj",
    )(x2, g_rows, w_bf16, w_ba)


def _gates_kernel(ba_ref, alog_ref, dtb_ref, o_ref, *, n_heads):
    lane = lax.broadcasted_iota(jnp.int32, (CHUNK, LANE), 1)
    r = lax.broadcasted_iota(jnp.int32, (CHUNK, CHUNK), 0)
    c = lax.broadcasted_iota(jnp.int32, (CHUNK, CHUNK), 1)
    m_prefix = jnp.where(c <= r, 1.0, 0.0).astype(F32)
    m_suffix = jnp.where(c >= r, 1.0, 0.0).astype(F32)
    within_chunk = functools.partial(jnp.dot, precision=lax.Precision.HIGHEST, preferred_element_type=F32)
    neg_rate = -jnp.exp(alog_ref[...])
    for r0 in range(0, ba_ref.shape[0], CHUNK):
        x = ba_ref[r0:r0 + CHUNK, :]
        z = x + dtb_ref[...]
        g = neg_rate * (jnp.maximum(z, 0.0) + jnp.log1p(jnp.exp(-jnp.abs(z))))
        gc = jnp.where(lane < 3 * n_heads, within_chunk(m_prefix, g), within_chunk(m_suffix, g))
        o_ref[r0:r0 + CHUNK, :] = jnp.where(lane < 2 * n_heads, jax.nn.sigmoid(x), gc)


def _gates(ba_logits, alog_rows, dtb_rows, layer, n_heads):
    t = ba_logits.shape[0]
    tm = _largest_tile(t, 512, CHUNK)
    return pl.pallas_call(
        functools.partial(_gates_kernel, n_heads=n_heads),
        grid=(t // tm,),
        in_specs=[
            pl.BlockSpec((tm, LANE), lambda i: (i, 0)),
            pl.BlockSpec((None, 1, LANE), lambda i: (layer, 0, 0)),
            pl.BlockSpec((None, 1, LANE), lambda i: (layer, 0, 0)),
        ],
        out_specs=pl.BlockSpec((tm, LANE), lambda i: (i, 0)),
        out_shape=jax.ShapeDtypeStruct((t, LANE), F32),
        compiler_params=_compiler_params(("parallel",)),
        name="gates",
    )(ba_logits, alog_rows, dtb_rows)


def _dot_nt(a, b):
    return lax.dot_general(a, b, (((1,), (1,)), ((), ())), preferred_element_type=F32)


def _delta_kernel(q_ref, k_ref, v_ref, wq_ref, wk_ref, wv_ref, cols_ref, rows_ref, hg_ref, o_ref,
                  pad_scr, r_scr, mp_scr, wq_scr, kdt_scr, qk_scr, ob_scr, s_scr):
    seq = q_ref.shape[0]
    hpb = q_ref.shape[1] // HEAD_DIM
    n_chunks = seq // CHUNK
    halo = SUBLANE
    c2 = 2 * CHUNK
    operands = ((q_ref, wq_ref), (k_ref, wk_ref), (v_ref, wv_ref))
    for j in range(len(operands)):
        pad_scr[j, 0:halo, :] = jnp.zeros((halo, HEAD_DIM), F32)
        pad_scr[j, seq + halo:seq + 2 * halo, :] = jnp.zeros((halo, HEAD_DIM), F32)

    def conv_silu(j, w, r0):
        acc = None
        for i in range(CONV_K):
            term = pad_scr[j, pl.ds(r0 + halo + i - CONV_PAD, CHUNK), :] * w[i:i + 1, :]
            acc = term if acc is None else acc + term
        return _silu(acc)

    def l2norm(y):
        return y * lax.rsqrt(jnp.sum(y * y, axis=-1, keepdims=True) + RMS_EPS)

    top_rows = lax.broadcasted_iota(jnp.int32, (c2, c2), 0) < CHUNK
    rw = lax.broadcasted_iota(jnp.int32, (CHUNK, c2), 0)
    cw = lax.broadcasted_iota(jnp.int32, (CHUNK, c2), 1)
    fwd_lanes = cw < CHUNK
    cs = jnp.where(fwd_lanes, cw, cw - CHUNK)
    incl_wide = (fwd_lanes & (rw >= cs)) | (~fwd_lanes & (rw <= cs))
    strict_wide = (fwd_lanes & (rw > cs)) | (~fwd_lanes & (rw < cs))
    eye_wide = jnp.where(rw == cs, 1.0, 0.0).astype(F32)

    def block_diag(cat):
        return jnp.concatenate([jnp.where(fwd_lanes, cat, 0.0), jnp.where(fwd_lanes, 0.0, cat)], axis=0)

    for p in range(hpb):
        lanes = slice(p * HEAD_DIM, (p + 1) * HEAD_DIM)

        def fill(n, carry, lanes=lanes):
            r0 = pl.multiple_of(n * CHUNK, CHUNK)
            for j, (src_ref, _) in enumerate(operands):
                pad_scr[j, pl.ds(r0 + halo, CHUNK), :] = src_ref[pl.ds(r0, CHUNK), lanes]
            return carry

        lax.fori_loop(0, n_chunks, fill, 0, unroll=2)

        def setup(n, carry, p=p, lanes=lanes):
            r0 = pl.multiple_of(n * CHUNK, CHUNK)
            q = l2norm(conv_silu(0, wq_ref[:, lanes], r0)) * (HEAD_DIM ** -0.5)
            k = l2norm(conv_silu(1, wk_ref[:, lanes], r0))
            v = conv_silu(2, wv_ref[:, lanes], r0)
            k16 = k.astype(BF16)
            k2_16 = jnp.concatenate([k16, k16], axis=0)
            gram = _dot_nt(jnp.concatenate([q.astype(BF16), k16], axis=0), k2_16)
            qk_wide, kk_wide = gram[:CHUNK], gram[CHUNK:]
            cols = cols_ref[pl.ds(r0, CHUNK), :]
            across = lambda c: jnp.broadcast_to(cols[:, 4 * p + c:4 * p + c + 1], (CHUNK, c2))
            gc2 = jnp.concatenate([across(0), across(1)], axis=0)
            be2 = jnp.concatenate([across(2), across(3)], axis=0)
            gc_row2 = rows_ref[p, n]
            k2 = jnp.concatenate([k, k], axis=0)
            v2 = jnp.concatenate([v, v], axis=0)
            q2 = jnp.concatenate([q, q], axis=0)
            gc_wide = jnp.where(fwd_lanes, gc2[:CHUNK], gc2[CHUNK:])
            be_wide = jnp.where(fwd_lanes, be2[:CHUNK], be2[CHUNK:])
            decay_wide = jnp.exp(gc_wide - gc_row2)
            mp_scr[p, n, 0:CHUNK, :] = -(be_wide * kk_wide * jnp.where(strict_wide, decay_wide, 0.0))
            mp_scr[p, n, CHUNK:c2, :] = eye_wide
            eg2 = jnp.exp(gc2)
            r_scr[p, n] = jnp.concatenate([v2 * be2, k2 * (be2 * eg2)], axis=1)
            qg2 = (q2 * eg2).astype(BF16)
            wq_scr[p, 0, n, CHUNK:c2, :] = qg2[:CHUNK]
            wq_scr[p, 1, n, CHUNK:c2, :] = qg2[CHUNK:]
            g_last2 = jnp.where(top_rows, gc_row2[:, CHUNK - 1:CHUNK], gc_row2[:, CHUNK:CHUNK + 1])
            kdec2 = k2 * jnp.exp(g_last2 - gc2)
            kdt_scr[p, n] = kdec2.T.astype(BF16)
            qk_scr[p, n] = (qk_wide * jnp.where(incl_wide, decay_wide, 0.0)).astype(BF16)
            return carry

        lax.fori_loop(0, n_chunks, setup, 0, unroll=8)

    group = _largest_tile(n_chunks, LEVEL_GROUP_CHUNKS, 1)

    def group_tiles(it):
        return [(p, it * group + g) for g in range(group) for p in range(hpb)]

    def level(_, carry):
        def per_group(it, carry):
            tiles = group_tiles(it)
            outs = []
            for p, n in tiles:
                mp = mp_scr[p, n]
                outs.append(jnp.dot(mp.astype(BF16), block_diag(mp[:CHUNK]).astype(BF16),
                                    preferred_element_type=F32))
            for (p, n), out in zip(tiles, outs):
                mp_scr[p, n, 0:CHUNK, :] = out[:CHUNK]
                mp_scr[p, n, CHUNK:c2, :] += out[CHUNK:]
            return carry

        return lax.fori_loop(0, n_chunks // group, per_group, carry)

    lax.fori_loop(0, int(math.log2(CHUNK)), level, 0)

    def apply_inverse(it, carry):
        tiles = group_tiles(it)
        outs = [jnp.dot(block_diag(mp_scr[p, n, CHUNK:c2, :]).astype(BF16), r_scr[p, n].astype(BF16),
                        preferred_element_type=F32) for p, n in tiles]
        for (p, n), uw in zip(tiles, outs):
            r_scr[p, n] = uw
            w2 = uw[:, HEAD_DIM:].astype(BF16)
            wq_scr[p, 0, n, 0:CHUNK, :] = w2[:CHUNK]
            wq_scr[p, 1, n, 0:CHUNK, :] = w2[CHUNK:]
        return carry

    lax.fori_loop(0, n_chunks // group, apply_inverse, 0)

    s_scr[...] = jnp.zeros(s_scr.shape, F32)
    zero_half = jnp.zeros((CHUNK, HEAD_DIM), BF16)
    chains = [(p, d) for p in range(hpb) for d in range(2)]

    def recurrence(i, carry):
        chunk_of = (i, n_chunks - 1 - i)
        states = [s_scr[p, d] for p, d in chains]
        ws = [jnp.dot(wq_scr[p, d, chunk_of[d]], s.astype(BF16), preferred_element_type=F32)
              for (p, d), s in zip(chains, states)]
        new_states = []
        for (p, d), s, w in zip(chains, states, ws):
            n = chunk_of[d]
            r0 = pl.multiple_of(n * CHUNK, CHUNK)
            v_new = (r_scr[p, n, d * CHUNK:(d + 1) * CHUNK, 0:HEAD_DIM] - w[:CHUNK]).astype(BF16)
            v_pad = jnp.concatenate([v_new, zero_half] if d == 0 else [zero_half, v_new], axis=0)
            o = w[CHUNK:] + jnp.dot(qk_scr[p, n], v_pad, preferred_element_type=F32)
            if d == 0:
                o_ref[pl.ds(r0, CHUNK), p * HEAD_DIM:(p + 1) * HEAD_DIM] = o
            else:
                ob_scr[p, pl.ds(r0, CHUNK), :] = o
            gc_row2 = rows_ref[p, n]
            g_last = gc_row2[:, CHUNK - 1:CHUNK] if d == 0 else gc_row2[:, CHUNK:CHUNK + 1]
            new_states.append(s * jnp.exp(g_last) + jnp.dot(kdt_scr[p, n], v_pad, preferred_element_type=F32))
        for (p, d), s in zip(chains, new_states):
            s_scr[p, d] = s
        return carry

    lax.fori_loop(0, n_chunks, recurrence, 0)

    def head_norm(n, carry):
        r0 = pl.multiple_of(n * CHUNK, CHUNK)
        for p in range(hpb):
            lanes = slice(p * HEAD_DIM, (p + 1) * HEAD_DIM)
            o = o_ref[pl.ds(r0, CHUNK), lanes] + ob_scr[p, pl.ds(r0, CHUNK), :]
            ms = jnp.mean(o * o, axis=-1, keepdims=True)
            o_ref[pl.ds(r0, CHUNK), lanes] = o * lax.rsqrt(ms + RMS_EPS) * hg_ref[...]
        return carry

    lax.fori_loop(0, n_chunks, head_norm, 0, unroll=4)


def _delta(proj3, conv_w, cols, rows, hg_rows, layer, n_heads):
    bsz, seq, _ = proj3.shape
    n_chunks = seq // CHUNK
    hpb = DELTA_HEADS_PER_PROGRAM
    width = hpb * HEAD_DIM
    n_blocks = n_heads // hpb
    tok = lambda off: pl.BlockSpec((None, seq, width), lambda b, h: (b, 0, off + h))
    cw = lambda off: pl.BlockSpec((None, CONV_K, width), lambda b, h: (layer, 0, off + h))
    return pl.pallas_call(
        _delta_kernel,
        grid=(bsz, n_blocks),
        in_specs=[
            tok(0), tok(n_blocks), tok(2 * n_blocks),
            cw(0), cw(n_blocks), cw(2 * n_blocks),
            pl.BlockSpec((None, None, seq, 4 * hpb), lambda b, h: (b, h, 0, 0)),
            pl.BlockSpec((None, hpb, n_chunks, 1, 2 * CHUNK), lambda b, h: (b, h, 0, 0, 0)),
            pl.BlockSpec((None, 1, HEAD_DIM), lambda b, h: (layer, 0, 0)),
        ],
        out_specs=pl.BlockSpec((None, seq, width), lambda b, h: (b, 0, h)),
        out_shape=jax.ShapeDtypeStruct((bsz, seq, n_heads * HEAD_DIM), F32),
        scratch_shapes=[
            pltpu.VMEM((3, seq + 2 * SUBLANE, HEAD_DIM), F32),
            pltpu.VMEM((hpb, n_chunks, 2 * CHUNK, 2 * HEAD_DIM), F32),
            pltpu.VMEM((hpb, n_chunks, 2 * CHUNK, 2 * CHUNK), F32),
            pltpu.VMEM((hpb, 2, n_chunks, 2 * CHUNK, HEAD_DIM), BF16),
            pltpu.VMEM((hpb, n_chunks, HEAD_DIM, 2 * CHUNK), BF16),
            pltpu.VMEM((hpb, n_chunks, CHUNK, 2 * CHUNK), BF16),
            pltpu.VMEM((hpb, seq, HEAD_DIM), F32),
            pltpu.VMEM((hpb, 2, HEAD_DIM, HEAD_DIM), F32),
        ],
        compiler_params=_compiler_params(("parallel", "parallel")),
        name="delta",
    )(proj3, proj3, proj3, conv_w, conv_w, conv_w, cols, rows, hg_rows)


def _s5_kernel(u_ref, bt_ref, ct_ref, pw_ref, y_ref,
               ut_scr, u_scr, s_scr, yt_scr, m_scr, wsum_scr, woutt_scr, *, bsz):
    sb, ch = S5_BLOCK, S5_GROUP_CH
    rows = u_ref.shape[0] // sb
    groups = LANE // ch
    w = 2 * S5_STATE
    blk = sb * ch

    def cmul(a_re, a_im, b_re, b_im):
        return a_re * b_re - a_im * b_im, a_re * b_im + a_im * b_re

    def lag_strip(d, q, h, lags):
        z = [cmul(ct_ref[d, 0, q, h], ct_ref[d, 1, q, h], pw_ref[d, 0, q][l:l + 1], pw_ref[d, 1, q][l:l + 1])
             for l in lags]
        z_re = jnp.concatenate([t[0] for t in z], axis=0)
        z_im = jnp.concatenate([t[1] for t in z], axis=0)
        over_states = functools.partial(lax.dot_general, dimension_numbers=(((1,), (1,)), ((), ())),
                                        precision=lax.Precision.HIGHEST, preferred_element_type=F32)
        return over_states(bt_ref[d, 0, q, h], z_re) - over_states(bt_ref[d, 1, q, h], z_im)

    lane = lax.broadcasted_iota(jnp.int32, (ch, blk), 1)
    for g in range(groups):
        k_f = lag_strip(0, g // 2, g % 2, range(sb))
        k_b = lag_strip(1, g // 2, g % 2, range(sb - 1, -1, -1))
        for j in range(sb):
            fwd = jnp.where(lane >= j * ch, pltpu.roll(k_f, j * ch, axis=1), 0.0)
            bwd = jnp.where(lane < (j + 1) * ch, pltpu.roll(k_b, (blk - (sb - 1 - j) * ch) % blk, axis=1), 0.0)
            m_scr[g, j * ch:(j + 1) * ch, :] = (fwd + bwd).astype(BF16)

    for j in range(sb):
        t = u_ref[pl.ds(j, rows, stride=sb), :].T
        for g in range(groups):
            ut_scr[g, j * ch:(j + 1) * ch, :] = t[g * ch:(g + 1) * ch, :]

    n_blocks = rows // bsz
    pairs = groups // 2
    for q in range(pairs):
        pw = [[pw_ref[d, part, q] for part in range(2)] for d in range(2)]
        for h in range(2):
            bt = [[bt_ref[d, part, q, h] for part in range(2)] for d in range(2)]
            ct = [[ct_ref[d, part, q, h] for part in range(2)] for d in range(2)]
            for j in range(sb):
                r0 = h * blk + j * ch
                for d, (l_sum, l_out) in enumerate(((sb - 1 - j, j + 1), (j, sb - j))):
                    s_re, s_im = cmul(bt[d][0], bt[d][1], pw[d][0][l_sum:l_sum + 1], pw[d][1][l_sum:l_sum + 1])
                    z_re, z_im = cmul(ct[d][0], ct[d][1], pw[d][0][l_out:l_out + 1], pw[d][1][l_out:l_out + 1])
                    wsum_scr[r0:r0 + ch, 2 * d * w:(2 * d + 1) * w] = s_re.astype(BF16)
                    wsum_scr[r0:r0 + ch, (2 * d + 1) * w:(2 * d + 2) * w] = s_im.astype(BF16)
                    woutt_scr[q, r0:r0 + ch, 2 * d * w:(2 * d + 1) * w] = z_re.astype(BF16)
                    woutt_scr[q, r0:r0 + ch, (2 * d + 1) * w:(2 * d + 2) * w] = (-z_im).astype(BF16)

        for g in range(2):
            u_scr[2 * q + g] = ut_scr[2 * q + g].T.astype(BF16)
        u2 = jnp.concatenate([u_scr[2 * q], u_scr[2 * q + 1]], axis=1)
        summaries = jnp.dot(u2, wsum_scr[...], preferred_element_type=F32)
        for part in range(4):
            s_scr[part, q * rows:(q + 1) * rows, :] = summaries[:, part * w:(part + 1) * w]

    chains = pairs * bsz
    decay = [jnp.concatenate([jnp.broadcast_to(pw_ref[d, part, q][sb:sb + 1], (bsz, w)) for q in range(pairs)], axis=0)
             for d in range(2) for part in range(2)]

    def step(k, xs):
        xf_re, xf_im, xb_re, xb_im = xs
        rf = pl.ds(k, chains, stride=n_blocks)
        rb = pl.ds(n_blocks - 1 - k, chains, stride=n_blocks)
        sf_re, sf_im = s_scr[0, rf, :], s_scr[1, rf, :]
        sb_re, sb_im = s_scr[2, rb, :], s_scr[3, rb, :]
        s_scr[0, rf, :] = xf_re
        s_scr[1, rf, :] = xf_im
        s_scr[2, rb, :] = xb_re
        s_scr[3, rb, :] = xb_im
        nf_re, nf_im = cmul(decay[0], decay[1], xf_re, xf_im)
        nb_re, nb_im = cmul(decay[2], decay[3], xb_re, xb_im)
        return nf_re + sf_re, nf_im + sf_im, nb_re + sb_re, nb_im + sb_im

    zero = jnp.zeros((chains, w), F32)
    lax.fori_loop(0, n_blocks, step, (zero, zero, zero, zero), unroll=2)

    for q in range(pairs):
        entering = jnp.concatenate([s_scr[part, q * rows:(q + 1) * rows, :] for part in range(4)], axis=1)
        carried = _dot_nt(entering.astype(BF16), woutt_scr[q])
        for g in range(2):
            y = (jnp.dot(u_scr[2 * q + g], m_scr[2 * q + g], preferred_element_type=F32)
                 + carried[:, g * blk:(g + 1) * blk])
            y_t = y.T
            for i in range(sb):
                yt_scr[i, (2 * q + g) * ch:(2 * q + g + 1) * ch, :] = y_t[i * ch:(i + 1) * ch, :]

    for i in range(sb):
        y_ref[pl.ds(i, rows, stride=sb), :] = yt_scr[i].T


def _s5(proj, u_block0, b_t, c_t, powers, layer, bsz):
    t = proj.shape[0]
    rows = t // S5_BLOCK
    blk = S5_BLOCK * S5_GROUP_CH
    groups = LANE // S5_GROUP_CH
    pairs = groups // 2
    w = 2 * S5_STATE
    n_tiles = powers.shape[3] // pairs
    placed = pl.BlockSpec((None, 2, 2, pairs, 2, S5_GROUP_CH, w), lambda k: (layer, 0, 0, k, 0, 0, 0))
    return pl.pallas_call(
        functools.partial(_s5_kernel, bsz=bsz),
        grid=(n_tiles,),
        in_specs=[
            pl.BlockSpec((t, LANE), lambda k: (0, u_block0 + k)),
            placed, placed,
            pl.BlockSpec((None, 2, 2, pairs, S5_BLOCK + 1, w), lambda k: (layer, 0, 0, k, 0, 0)),
        ],
        out_specs=pl.BlockSpec((t, LANE), lambda k: (0, k)),
        out_shape=jax.ShapeDtypeStruct((t, n_tiles * LANE), F32),
        scratch_shapes=[
            pltpu.VMEM((groups, blk, rows), F32),
            pltpu.VMEM((groups, rows, blk), BF16),
            pltpu.VMEM((4, pairs * rows, w), F32),
            pltpu.VMEM((S5_BLOCK, LANE, rows), F32),
            pltpu.VMEM((groups, blk, blk), BF16),
            pltpu.VMEM((2 * blk, 4 * w), BF16),
            pltpu.VMEM((pairs, 2 * blk, 4 * w), BF16),
        ],
        compiler_params=_compiler_params(("parallel",)),
        name="s5",
    )(proj, b_t, c_t, powers)


def _s5_params(lam_re, lam_im, log_dt, b_re, b_im, c_re, c_im):
    n_groups = lam_re.shape[1]
    sb, p, c = S5_BLOCK, S5_STATE, S5_GROUP_CH
    l_re, l_im = lam_re.astype(F32), lam_im.astype(F32)
    dt = jnp.exp(log_dt.astype(F32))[..., None]
    mag = jnp.exp(l_re * dt)
    bar_re, bar_im = mag * jnp.cos(l_im * dt), mag * jnp.sin(l_im * dt)
    n_re, n_im = bar_re - 1.0, bar_im
    den = l_re * l_re + l_im * l_im
    k_re = ((n_re * l_re + n_im * l_im) / den)[..., None]
    k_im = ((n_im * l_re - n_re * l_im) / den)[..., None]
    bb_re = k_re * b_re.astype(F32) - k_im * b_im.astype(F32)
    bb_im = k_re * b_im.astype(F32) + k_im * b_re.astype(F32)
    cc_re, cc_im = c_re.astype(F32), c_im.astype(F32)

    steps = jnp.arange(sb + 1, dtype=F32)[:, None, None, None]
    mag_l = jnp.exp(steps * (l_re * dt)[None])
    pw_re, pw_im = mag_l * jnp.cos(steps * (l_im * dt)[None]), mag_l * jnp.sin(steps * (l_im * dt)[None])

    same = jnp.eye(2, dtype=F32)
    def place(t):
        t2 = t.reshape(2, n_groups // 2, 2, c, 1, p) * same[None, None, :, None, :, None]
        return t2.reshape(2, n_groups // 2, 2, c, 2 * p)
    b_t = jnp.stack([place(jnp.swapaxes(bb_re, 2, 3)), place(jnp.swapaxes(bb_im, 2, 3))], axis=1)
    c_t = jnp.stack([place(cc_re), place(cc_im)], axis=1)
    def side_by_side(t):
        t2 = jnp.transpose(t.reshape(sb + 1, 2, n_groups // 2, 2, p), (1, 2, 0, 3, 4))
        return t2.reshape(2, n_groups // 2, sb + 1, 2 * p)
    powers = jnp.stack([side_by_side(pw_re), side_by_side(pw_im)], axis=1)
    return b_t, c_t, powers


def _out_kernel(oa_ref, za_ref, ys_ref, u_ref, zb_ref, ga_ref, gb_ref, x_ref,
                wpa_ref, dsk_ref, wglu_ref, bglu_ref, wpb_ref, bga_ref, bgb_ref, wout_ref, fg_ref,
                o_ref, *, final_norm):
    a = (oa_ref[...] * _silu(za_ref[...])).astype(BF16)
    y_a = jnp.dot(a, wpa_ref[...], preferred_element_type=F32)
    y_s = jax.nn.gelu(ys_ref[...] + u_ref[...] * dsk_ref[...])
    glu = jnp.dot(y_s.astype(BF16), wglu_ref[...], preferred_element_type=F32) + bglu_ref[...]
    y_s = y_s * jax.nn.sigmoid(glu)
    y_b = jnp.dot((y_s * _silu(zb_ref[...])).astype(BF16), wpb_ref[...], preferred_element_type=F32)
    merged = (jax.nn.sigmoid(ga_ref[...] + bga_ref[...]) * y_a
              + jax.nn.sigmoid(gb_ref[...] + bgb_ref[...]) * y_b)
    out = x_ref[...] + jnp.dot(merged.astype(BF16), wout_ref[...], preferred_element_type=F32)
    if final_norm:
        ms = jnp.mean(out * out, axis=-1, keepdims=True)
        out = out * lax.rsqrt(ms + RMS_EPS) * fg_ref[...]
    o_ref[...] = out


def _out_stage(o_a, proj, y_s5, x2, w_pa, d_skip, w_glu, b_glu, w_pb, b_gate, w_out, final_g, *,
               layer, za_block, u_block, zb_block, gate_block0, final_norm):
    t, d = x2.shape
    wa = o_a.shape[1]
    wb = y_s5.shape[1]
    tm = _largest_tile(t, 256, SUBLANE)
    row = lambda width, blk: pl.BlockSpec((tm, width), lambda i: (i, blk))
    const = lambda shape, blk=0: pl.BlockSpec((None,) + shape, lambda i: (layer, 0, blk),
                                              pipeline_mode=pl.Buffered(1))
    return pl.pallas_call(
        functools.partial(_out_kernel, final_norm=final_norm),
        grid=(t // tm,),
        in_specs=[
            row(wa, 0), row(wa, za_block), row(wb, 0), row(wb, u_block), row(wb, zb_block),
            row(d, gate_block0), row(d, gate_block0 + 1), row(d, 0),
            const((wa, d)), const((1, wb)), const((wb, wb)), const((1, wb)), const((wb, d)),
            const((1, d), 0), const((1, d), 1), const((d, d)), const((1, d)),
        ],
        out_specs=pl.BlockSpec((tm, d), lambda i: (i, 0)),
        out_shape=jax.ShapeDtypeStruct((t, d), F32),
        compiler_params=_compiler_params(("parallel",)),
        name="out_stage",
    )(o_a, proj, y_s5, proj, proj, proj, proj, x2, w_pa, d_skip, w_glu, b_glu, w_pb, b_gate, b_gate, w_out,
      final_g)


def _prepare_params(ln_g, w_in, conv_w, a_log, dt_bias, head_norm_g, lam_re, lam_im, log_dt, b_re, b_im,
                    c_re, c_im, d_skip, w_glu, b_glu, w_pa, w_pb, b_gate, w_out, final_g):
    depth, d, _ = w_in.shape
    n_heads = a_log.shape[2]
    wa = n_heads * HEAD_DIM
    wb = d_skip.shape[1]
    n_ba = 4 * n_heads
    o_beta = 4 * wa
    w_in16 = w_in.astype(BF16)
    w_cols = _regroup_weights(w_in16, o_beta, n_ba)
    w_ba = w_in16[:, :, o_beta:o_beta + LANE]
    pad_rows = lambda p: jnp.pad(p.astype(F32).reshape(depth, 1, n_ba // 2),
                                 ((0, 0), (0, 0), (n_ba // 2, LANE - n_ba)))
    row = lambda p: p.astype(F32).reshape(depth, 1, -1)
    return dict(
        ln_g=row(ln_g), w_cols=w_cols, w_ba=w_ba, conv_w=conv_w.astype(F32), alog=pad_rows(a_log), dtb=pad_rows(dt_bias),
        head_norm_g=row(head_norm_g),
        s5=jax.vmap(_s5_params)(lam_re, lam_im, log_dt, b_re, b_im, c_re, c_im),
        d_skip=row(d_skip), w_glu=w_glu.astype(BF16), b_glu=row(b_glu), w_pa=w_pa.astype(BF16),
        w_pb=w_pb.astype(BF16), b_gate=row(b_gate), w_out=w_out.astype(BF16),
        final_g=jnp.broadcast_to(final_g.astype(F32).reshape(1, 1, d), (depth, 1, d)),
        n_heads=n_heads, wa=wa, wb=wb)


def _layer(x2, bsz, seq, prm, layer, final_norm):
    t, d = x2.shape
    n_heads, wa, wb = prm["n_heads"], prm["wa"], prm["wb"]
    assert wa == wb and seq % (CHUNK * SUBLANE) == 0 and 4 * n_heads <= LANE
    c_qkv, c_za, c_beta, c_alpha, c_u, c_zb = 3 * wa, wa, 2 * n_heads, 2 * n_heads, wb, wb
    n_proj = prm["w_cols"].shape[2]

    proj, ba_logits = _inproj(x2, prm["ln_g"], prm["w_cols"], prm["w_ba"], layer)
    proj3 = proj.reshape(bsz, seq, n_proj)

    gates = _gates(ba_logits, prm["alog"], prm["dtb"], layer, n_heads)
    beta = gates[:, :c_beta].reshape(bsz, seq, 2, n_heads)
    gc = gates[:, c_beta:c_beta + c_alpha].reshape(bsz, seq, 2, n_heads)
    hpb = DELTA_HEADS_PER_PROGRAM
    cols = jnp.stack([gc[:, :, 0], gc[:, :, 1], beta[:, :, 0], beta[:, :, 1]], axis=-1)
    cols = jnp.transpose(cols.reshape(bsz, seq, n_heads // hpb, 4 * hpb), (0, 2, 1, 3))
    rows = jnp.transpose(gc.reshape(bsz, seq // CHUNK, CHUNK, 2, n_heads), (0, 4, 1, 3, 2))
    rows = rows.reshape(bsz, n_heads, seq // CHUNK, 1, 2 * CHUNK)

    o_a = _delta(proj3, prm["conv_w"], cols, rows, prm["head_norm_g"], layer, n_heads)
    o_a = o_a.reshape(t, wa)

    u_off = c_qkv + c_za
    y_s5 = _s5(proj, u_off // LANE, *prm["s5"], layer, bsz)

    return _out_stage(
        o_a, proj, y_s5, x2, prm["w_pa"], prm["d_skip"], prm["w_glu"], prm["b_glu"], prm["w_pb"],
        prm["b_gate"], prm["w_out"], prm["final_g"], layer=layer,
        za_block=c_qkv // wa, u_block=u_off // wb, zb_block=(u_off + c_u) // wb,
        gate_block0=(u_off + c_u + c_zb) // d, final_norm=final_norm)


def kernel(x, ln_g, w_in, conv_w, a_log, dt_bias, head_norm_g, lam_re, lam_im, log_dt, b_re, b_im, c_re, c_im, d_skip, w_glu, b_glu, w_pa, w_pb, b_gate, w_out, final_g):
    bsz, seq, d = x.shape
    depth = ln_g.shape[0]
    prm = _prepare_params(ln_g, w_in, conv_w, a_log, dt_bias, head_norm_g, lam_re, lam_im, log_dt, b_re, b_im,
                          c_re, c_im, d_skip, w_glu, b_glu, w_pa, w_pb, b_gate, w_out, final_g)
    x2 = x.reshape(bsz * seq, d)
    for layer in range(depth):
        x2 = _layer(x2, bsz, seq, prm, layer, final_norm=(layer == depth - 1))
    return x2.reshape(bsz, seq, d)
```

```python
import functools
import math

import jax
import jax.numpy as jnp
from jax import lax
from jax.experimental import pallas as pl
from jax.experimental.pallas import tpu as pltpu

F32 = jnp.float32
BF16 = jnp.bfloat16

LANE = 128
SUBLANE = 8
HEAD_DIM = 128
CHUNK = 64
CONV_K = 5
CONV_PAD = (CONV_K - 1) // 2
DELTA_HEADS_PER_PROGRAM = 2
LEVEL_GROUP_CHUNKS = 8
S5_GROUP_CH = 16
S5_STATE = 64
S5_BLOCK = 16
RMS_EPS = 1e-6
VMEM_LIMIT_BYTES = 56 * 1024 * 1024


def _compiler_params(semantics):
    return pltpu.CompilerParams(dimension_semantics=semantics, vmem_limit_bytes=VMEM_LIMIT_BYTES)


def _silu(x):
    return x * jax.nn.sigmoid(x)


def _largest_tile(n, cap, unit):
    best = unit
    t = unit
    while t <= min(n, cap):
        if n % t == 0:
            best = t
        t += unit
    return best


def _regroup_kernel(a_ref, b_ref, o_ref, *, n_plain, shift):
    @pl.when(pl.program_id(2) < n_plain)
    def _():
        o_ref[...] = a_ref[...]

    @pl.when(pl.program_id(2) >= n_plain)
    def _():
        both = jnp.concatenate([a_ref[...], b_ref[...]], axis=1)
        o_ref[...] = both[:, shift:shift + o_ref.shape[1]]


def _regroup_weights(w, o_skip, n_skip):
    depth, d, n = w.shape
    n_out = n - n_skip
    tn = _largest_tile(math.gcd(o_skip, n_out - o_skip), 1024, LANE)
    tr = _largest_tile(d, 2048, 2 * SUBLANE)
    return pl.pallas_call(
        functools.partial(_regroup_kernel, n_plain=o_skip // tn, shift=n_skip),
        grid=(depth, d // tr, n_out // tn),
        in_specs=[
            pl.BlockSpec((None, tr, tn), lambda l, r, j: (l, r, j)),
            pl.BlockSpec((None, tr, LANE), lambda l, r, j: (l, r, (j + 1) * (tn // LANE))),
        ],
        out_specs=pl.BlockSpec((None, tr, tn), lambda l, r, j: (l, r, j)),
        out_shape=jax.ShapeDtypeStruct((depth, d, n_out), w.dtype),
        compiler_params=_compiler_params(("parallel", "parallel", "parallel")),
        name="regroup",
    )(w, w)


def _inproj_kernel(x_ref, g_ref, w_ref, wba_ref, o_ref, ba_ref, h_ref):
    @pl.when(pl.program_id(1) == 0)
    def _():
        x = x_ref[...]
        ms = jnp.mean(x * x, axis=-1, keepdims=True)
        h_ref[...] = (x * lax.rsqrt(ms + RMS_EPS) * g_ref[...]).astype(BF16)
        ba_ref[...] = jnp.dot(h_ref[...], wba_ref[...], preferred_element_type=F32)

    o_ref[...] = jnp.dot(h_ref[...], w_ref[...], preferred_element_type=F32)


def _inproj(x2, g_rows, w_bf16, w_ba, layer):
    t, d = x2.shape
    n = w_bf16.shape[2]
    tm = _largest_tile(t, 1024, SUBLANE)
    tn = _largest_tile(n, 1280, LANE)
    return pl.pallas_call(
        _inproj_kernel,
        grid=(t // tm, n // tn),
        in_specs=[
            pl.BlockSpec((tm, d), lambda i, j: (i, 0)),
            pl.BlockSpec((None, 1, d), lambda i, j: (layer, 0, 0)),
            pl.BlockSpec((None, d, tn), lambda i, j: (layer, 0, j)),
            pl.BlockSpec((None, d, LANE), lambda i, j: (layer, 0, 0)),
        ],
        out_specs=[pl.BlockSpec((tm, tn), lambda i, j: (i, j)),
                   pl.BlockSpec((tm, LANE), lambda i, j: (i, 0))],
        out_shape=[jax.ShapeDtypeStruct((t, n), F32), jax.ShapeDtypeStruct((t, LANE), F32)],
        scratch_shapes=[pltpu.VMEM((tm, d), BF16)],
        compiler_params=_compiler_params(("parallel", "arbitrary")),
        name="inproj",
    )(x2, g_rows, w_bf16, w_ba)


def _gates_kernel(ba_ref, alog_ref, dtb_ref, o_ref, *, n_heads):
    lane = lax.broadcasted_iota(jnp.int32, (CHUNK, LANE), 1)
    r = lax.broadcasted_iota(jnp.int32, (CHUNK, CHUNK), 0)
    c = lax.broadcasted_iota(jnp.int32, (CHUNK, CHUNK), 1)
    m_prefix = jnp.where(c <= r, 1.0, 0.0).astype(F32)
    m_suffix = jnp.where(c >= r, 1.0, 0.0).astype(F32)
    within_chunk = functools.partial(jnp.dot, precision=lax.Precision.HIGHEST, preferred_element_type=F32)
    neg_rate = -jnp.exp(alog_ref[...])
    for r0 in range(0, ba_ref.shape[0], CHUNK):
        x = ba_ref[r0:r0 + CHUNK, :]
        z = x + dtb_ref[...]
        g = neg_rate * (jnp.maximum(z, 0.0) + jnp.log1p(jnp.exp(-jnp.abs(z))))
        gc = jnp.where(lane < 3 * n_heads, within_chunk(m_prefix, g), within_chunk(m_suffix, g))
        o_ref[r0:r0 + CHUNK, :] = jnp.where(lane < 2 * n_heads, jax.nn.sigmoid(x), gc)


def _gates(ba_logits, alog_rows, dtb_rows, layer, n_heads):
    t = ba_logits.shape[0]
    tm = _largest_tile(t, 512, CHUNK)
    return pl.pallas_call(
        functools.partial(_gates_kernel, n_heads=n_heads),
        grid=(t // tm,),
        in_specs=[
            pl.BlockSpec((tm, LANE), lambda i: (i, 0)),
            pl.BlockSpec((None, 1, LANE), lambda i: (layer, 0, 0)),
            pl.BlockSpec((None, 1, LANE), lambda i: (layer, 0, 0)),
        ],
        out_specs=pl.BlockSpec((tm, LANE), lambda i: (i, 0)),
        out_shape=jax.ShapeDtypeStruct((t, LANE), F32),
        compiler_params=_compiler_params(("parallel",)),
        name="gates",
    )(ba_logits, alog_rows, dtb_rows)


def _dot_nt(a, b):
    return lax.dot_general(a, b, (((1,), (1,)), ((), ())), preferred_element_type=F32)


def _delta_kernel(q_ref, k_ref, v_ref, wq_ref, wk_ref, wv_ref, cols_ref, rows_ref, hg_ref, o_ref,
                  pad_scr, r_scr, mp_scr, wq_scr, kdt_scr, qk_scr, od_scr, s_scr):
    seq = q_ref.shape[0]
    hpb = q_ref.shape[1] // HEAD_DIM
    n_chunks = seq // CHUNK
    halo = SUBLANE
    c2 = 2 * CHUNK
    operands = ((q_ref, wq_ref), (k_ref, wk_ref), (v_ref, wv_ref))
    for j in range(len(operands)):
        pad_scr[j, 0:halo, :] = jnp.zeros((halo, HEAD_DIM), F32)
        pad_scr[j, seq + halo:seq + 2 * halo, :] = jnp.zeros((halo, HEAD_DIM), F32)

    def conv_silu(j, w, r0):
        acc = None
        for i in range(CONV_K):
            term = pad_scr[j, pl.ds(r0 + halo + i - CONV_PAD, CHUNK), :] * w[i:i + 1, :]
            acc = term if acc is None else acc + term
        return _silu(acc)

    def l2norm(y):
        return y * lax.rsqrt(jnp.sum(y * y, axis=-1, keepdims=True) + RMS_EPS)

    top_rows = lax.broadcasted_iota(jnp.int32, (c2, c2), 0) < CHUNK
    rw = lax.broadcasted_iota(jnp.int32, (CHUNK, c2), 0)
    cw = lax.broadcasted_iota(jnp.int32, (CHUNK, c2), 1)
    fwd_lanes = cw < CHUNK
    cs = jnp.where(fwd_lanes, cw, cw - CHUNK)
    incl_wide = (fwd_lanes & (rw >= cs)) | (~fwd_lanes & (rw <= cs))
    strict_wide = (fwd_lanes & (rw > cs)) | (~fwd_lanes & (rw < cs))
    eye_wide = jnp.where(rw == cs, 1.0, 0.0).astype(F32)

    def block_diag(cat):
        return jnp.concatenate([jnp.where(fwd_lanes, cat, 0.0), jnp.where(fwd_lanes, 0.0, cat)], axis=0)

    for p in range(hpb):
        lanes = slice(p * HEAD_DIM, (p + 1) * HEAD_DIM)

        def fill(n, carry, lanes=lanes):
            r0 = pl.multiple_of(n * CHUNK, CHUNK)
            for j, (src_ref, _) in enumerate(operands):
                pad_scr[j, pl.ds(r0 + halo, CHUNK), :] = src_ref[pl.ds(r0, CHUNK), lanes]
            return carry

        lax.fori_loop(0, n_chunks, fill, 0, unroll=2)

        def setup(n, carry, p=p, lanes=lanes):
            r0 = pl.multiple_of(n * CHUNK, CHUNK)
            q = l2norm(conv_silu(0, wq_ref[:, lanes], r0)) * (HEAD_DIM ** -0.5)
            k = l2norm(conv_silu(1, wk_ref[:, lanes], r0))
            v = conv_silu(2, wv_ref[:, lanes], r0)
            k16 = k.astype(BF16)
            k2_16 = jnp.concatenate([k16, k16], axis=0)
            gram = _dot_nt(jnp.concatenate([q.astype(BF16), k16], axis=0), k2_16)
            qk_wide, kk_wide = gram[:CHUNK], gram[CHUNK:]
            cols = cols_ref[pl.ds(r0, CHUNK), :]
            across = lambda c: jnp.broadcast_to(cols[:, 4 * p + c:4 * p + c + 1], (CHUNK, c2))
            gc2 = jnp.concatenate([across(0), across(1)], axis=0)
            be2 = jnp.concatenate([across(2), across(3)], axis=0)
            gc_row2 = rows_ref[p, n]
            k2 = jnp.concatenate([k, k], axis=0)
            v2 = jnp.concatenate([v, v], axis=0)
            q2 = jnp.concatenate([q, q], axis=0)
            gc_wide = jnp.where(fwd_lanes, gc2[:CHUNK], gc2[CHUNK:])
            be_wide = jnp.where(fwd_lanes, be2[:CHUNK], be2[CHUNK:])
            decay_wide = jnp.exp(gc_wide - gc_row2)
            mp_scr[p, n, 0:CHUNK, :] = -(be_wide * kk_wide * jnp.where(strict_wide, decay_wide, 0.0))
            mp_scr[p, n, CHUNK:c2, :] = eye_wide
            eg2 = jnp.exp(gc2)
            r_scr[p, n] = jnp.concatenate([v2 * be2, k2 * (be2 * eg2)], axis=1)
            qg2 = (q2 * eg2).astype(BF16)
            wq_scr[p, 0, n, CHUNK:c2, :] = qg2[:CHUNK]
            wq_scr[p, 1, n, CHUNK:c2, :] = qg2[CHUNK:]
            g_last2 = jnp.where(top_rows, gc_row2[:, CHUNK - 1:CHUNK], gc_row2[:, CHUNK:CHUNK + 1])
            kdec2 = k2 * jnp.exp(g_last2 - gc2)
            kdt_scr[p, n] = kdec2.T.astype(BF16)
            qk_scr[p, n] = (qk_wide * jnp.where(incl_wide, decay_wide, 0.0)).astype(BF16)
            return carry

        lax.fori_loop(0, n_chunks, setup, 0, unroll=8)

    group = _largest_tile(n_chunks, LEVEL_GROUP_CHUNKS, 1)

    def group_tiles(it):
        return [(p, it * group + g) for g in range(group) for p in range(hpb)]

    def level(_, carry):
        def per_group(it, carry):
            tiles = group_tiles(it)
            outs = []
            for p, n in tiles:
                mp = mp_scr[p, n]
                outs.append(jnp.dot(mp.astype(BF16), block_diag(mp[:CHUNK]).astype(BF16),
                                    preferred_element_type=F32))
            for (p, n), out in zip(tiles, outs):
                mp_scr[p, n, 0:CHUNK, :] = out[:CHUNK]
                mp_scr[p, n, CHUNK:c2, :] += out[CHUNK:]
            return carry

        return lax.fori_loop(0, n_chunks // group, per_group, carry)

    lax.fori_loop(0, int(math.log2(CHUNK)), level, 0)

    def apply_inverse(it, carry):
        tiles = group_tiles(it)
        outs = [jnp.dot(block_diag(mp_scr[p, n, CHUNK:c2, :]).astype(BF16), r_scr[p, n].astype(BF16),
                        preferred_element_type=F32) for p, n in tiles]
        for (p, n), uw in zip(tiles, outs):
            r_scr[p, n] = uw
            w2 = uw[:, HEAD_DIM:].astype(BF16)
            wq_scr[p, 0, n, 0:CHUNK, :] = w2[:CHUNK]
            wq_scr[p, 1, n, 0:CHUNK, :] = w2[CHUNK:]
        return carry

    lax.fori_loop(0, n_chunks // group, apply_inverse, 0)

    s_scr[...] = jnp.zeros(s_scr.shape, F32)
    zero_half = jnp.zeros((CHUNK, HEAD_DIM), BF16)
    chains = [(p, d) for p in range(hpb) for d in range(2)]

    def recurrence(i, carry):
        chunk_of = (i, n_chunks - 1 - i)
        states = [s_scr[p, d] for p, d in chains]
        ws = [jnp.dot(wq_scr[p, d, chunk_of[d]], s.astype(BF16), preferred_element_type=F32)
              for (p, d), s in zip(chains, states)]
        new_states = []
        for (p, d), s, w in zip(chains, states, ws):
            n = chunk_of[d]
            r0 = pl.multiple_of(n * CHUNK, CHUNK)
            v_new = (r_scr[p, n, d * CHUNK:(d + 1) * CHUNK, 0:HEAD_DIM] - w[:CHUNK]).astype(BF16)
            v_pad = jnp.concatenate([v_new, zero_half] if d == 0 else [zero_half, v_new], axis=0)
            o = w[CHUNK:] + jnp.dot(qk_scr[p, n], v_pad, preferred_element_type=F32)
            od_scr[p, d, pl.ds(r0, CHUNK), :] = o
            gc_row2 = rows_ref[p, n]
            g_last = gc_row2[:, CHUNK - 1:CHUNK] if d == 0 else gc_row2[:, CHUNK:CHUNK + 1]
            new_states.append(s * jnp.exp(g_last) + jnp.dot(kdt_scr[p, n], v_pad, preferred_element_type=F32))
        for (p, d), s in zip(chains, new_states):
            s_scr[p, d] = s
        return carry

    lax.fori_loop(0, n_chunks, recurrence, 0)

    def head_norm(n, carry):
        r0 = pl.multiple_of(n * CHUNK, CHUNK)
        for p in range(hpb):
            lanes = slice(p * HEAD_DIM, (p + 1) * HEAD_DIM)
            o = od_scr[p, 0, pl.ds(r0, CHUNK), :] + od_scr[p, 1, pl.ds(r0, CHUNK), :]
            ms = jnp.mean(o * o, axis=-1, keepdims=True)
            o_ref[pl.ds(r0, CHUNK), lanes] = o * lax.rsqrt(ms + RMS_EPS) * hg_ref[...]
        return carry

    lax.fori_loop(0, n_chunks, head_norm, 0, unroll=4)


def _delta(proj3, conv_w, cols, rows, hg_rows, layer, n_heads):
    bsz, seq, _ = proj3.shape
    n_chunks = seq // CHUNK
    hpb = DELTA_HEADS_PER_PROGRAM
    width = hpb * HEAD_DIM
    n_blocks = n_heads // hpb
    tok = lambda off: pl.BlockSpec((None, seq, width), lambda b, h: (b, 0, off + h))
    cw = lambda off: pl.BlockSpec((None, CONV_K, width), lambda b, h: (layer, 0, off + h))
    return pl.pallas_call(
        _delta_kernel,
        grid=(bsz, n_blocks),
        in_specs=[
            tok(0), tok(n_blocks), tok(2 * n_blocks),
            cw(0), cw(n_blocks), cw(2 * n_blocks),
            pl.BlockSpec((None, None, seq, 4 * hpb), lambda b, h: (b, h, 0, 0)),
            pl.BlockSpec((None, hpb, n_chunks, 1, 2 * CHUNK), lambda b, h: (b, h, 0, 0, 0)),
            pl.BlockSpec((None, 1, HEAD_DIM), lambda b, h: (layer, 0, 0)),
        ],
        out_specs=pl.BlockSpec((None, seq, width), lambda b, h: (b, 0, h)),
        out_shape=jax.ShapeDtypeStruct((bsz, seq, n_heads * HEAD_DIM), F32),
        scratch_shapes=[
            pltpu.VMEM((3, seq + 2 * SUBLANE, HEAD_DIM), F32),
            pltpu.VMEM((hpb, n_chunks, 2 * CHUNK, 2 * HEAD_DIM), F32),
            pltpu.VMEM((hpb, n_chunks, 2 * CHUNK, 2 * CHUNK), F32),
            pltpu.VMEM((hpb, 2, n_chunks, 2 * CHUNK, HEAD_DIM), BF16),
            pltpu.VMEM((hpb, n_chunks, HEAD_DIM, 2 * CHUNK), BF16),
            pltpu.VMEM((hpb, n_chunks, CHUNK, 2 * CHUNK), BF16),
            pltpu.VMEM((hpb, 2, seq, HEAD_DIM), F32),
            pltpu.VMEM((hpb, 2, HEAD_DIM, HEAD_DIM), F32),
        ],
        compiler_params=_compiler_params(("parallel", "parallel")),
        name="delta",
    )(proj3, proj3, proj3, conv_w, conv_w, conv_w, cols, rows, hg_rows)


def _s5_kernel(u_ref, bt_ref, ct_ref, pw_ref, y_ref,
               ut_scr, s_scr, yt_scr, m_scr, wsum_scr, woutt_scr, *, bsz):
    sb, ch = S5_BLOCK, S5_GROUP_CH
    rows = u_ref.shape[0] // sb
    groups = LANE // ch
    w = 2 * S5_STATE
    blk = sb * ch

    def cmul(a_re, a_im, b_re, b_im):
        return a_re * b_re - a_im * b_im, a_re * b_im + a_im * b_re

    def lag_strip(d, q, h, lags):
        z = [cmul(ct_ref[d, 0, q, h], ct_ref[d, 1, q, h], pw_ref[d, 0, q][l:l + 1], pw_ref[d, 1, q][l:l + 1])
             for l in lags]
        z_re = jnp.concatenate([t[0] for t in z], axis=0)
        z_im = jnp.concatenate([t[1] for t in z], axis=0)
        over_states = functools.partial(lax.dot_general, dimension_numbers=(((1,), (1,)), ((), ())),
                                        precision=lax.Precision.HIGHEST, preferred_element_type=F32)
        return over_states(bt_ref[d, 0, q, h], z_re) - over_states(bt_ref[d, 1, q, h], z_im)

    lane = lax.broadcasted_iota(jnp.int32, (ch, blk), 1)
    for g in range(groups):
        k_f = lag_strip(0, g // 2, g % 2, range(sb))
        k_b = lag_strip(1, g // 2, g % 2, range(sb - 1, -1, -1))
        for j in range(sb):
            fwd = jnp.where(lane >= j * ch, pltpu.roll(k_f, j * ch, axis=1), 0.0)
            bwd = jnp.where(lane < (j + 1) * ch, pltpu.roll(k_b, (blk - (sb - 1 - j) * ch) % blk, axis=1), 0.0)
            m_scr[g, j * ch:(j + 1) * ch, :] = (fwd + bwd).astype(BF16)

    for j in range(sb):
        t = u_ref[pl.ds(j, rows, stride=sb), :].T.astype(BF16)
        for g in range(groups):
            ut_scr[g, j * ch:(j + 1) * ch, :] = t[g * ch:(g + 1) * ch, :]

    def dot_tn(a, b):
        return lax.dot_general(a, b, (((0,), (0,)), ((), ())), preferred_element_type=F32)

    n_blocks = rows // bsz
    pairs = groups // 2
    for q in range(pairs):
        pw = [[pw_ref[d, part, q] for part in range(2)] for d in range(2)]
        for h in range(2):
            bt = [[bt_ref[d, part, q, h] for part in range(2)] for d in range(2)]
            ct = [[ct_ref[d, part, q, h] for part in range(2)] for d in range(2)]
            for j in range(sb):
                r0 = h * blk + j * ch
                for d, (l_sum, l_out) in enumerate(((sb - 1 - j, j + 1), (j, sb - j))):
                    s_re, s_im = cmul(bt[d][0], bt[d][1], pw[d][0][l_sum:l_sum + 1], pw[d][1][l_sum:l_sum + 1])
                    z_re, z_im = cmul(ct[d][0], ct[d][1], pw[d][0][l_out:l_out + 1], pw[d][1][l_out:l_out + 1])
                    wsum_scr[q, r0:r0 + ch, 2 * d * w:(2 * d + 1) * w] = s_re.astype(BF16)
                    wsum_scr[q, r0:r0 + ch, (2 * d + 1) * w:(2 * d + 2) * w] = s_im.astype(BF16)
                    woutt_scr[q, r0:r0 + ch, 2 * d * w:(2 * d + 1) * w] = z_re.astype(BF16)
                    woutt_scr[q, r0:r0 + ch, (2 * d + 1) * w:(2 * d + 2) * w] = (-z_im).astype(BF16)

        ut2 = jnp.concatenate([ut_scr[2 * q], ut_scr[2 * q + 1]], axis=0)
        summaries = dot_tn(ut2, wsum_scr[q])
        for part in range(4):
            s_scr[part, q * rows:(q + 1) * rows, :] = summaries[:, part * w:(part + 1) * w]

    chains = pairs * bsz
    decay = [jnp.concatenate([jnp.broadcast_to(pw_ref[d, part, q][sb:sb + 1], (bsz, w)) for q in range(pairs)], axis=0)
             for d in range(2) for part in range(2)]

    def step(k, xs):
        xf_re, xf_im, xb_re, xb_im = xs
        rf = pl.ds(k, chains, stride=n_blocks)
        rb = pl.ds(n_blocks - 1 - k, chains, stride=n_blocks)
        sf_re, sf_im = s_scr[0, rf, :], s_scr[1, rf, :]
        sb_re, sb_im = s_scr[2, rb, :], s_scr[3, rb, :]
        s_scr[0, rf, :] = xf_re
        s_scr[1, rf, :] = xf_im
        s_scr[2, rb, :] = xb_re
        s_scr[3, rb, :] = xb_im
        nf_re, nf_im = cmul(decay[0], decay[1], xf_re, xf_im)
        nb_re, nb_im = cmul(decay[2], decay[3], xb_re, xb_im)
        return nf_re + sf_re, nf_im + sf_im, nb_re + sb_re, nb_im + sb_im

    zero = jnp.zeros((chains, w), F32)
    lax.fori_loop(0, n_blocks, step, (zero, zero, zero, zero), unroll=2)

    for q in range(pairs):
        entering = jnp.concatenate([s_scr[part, q * rows:(q + 1) * rows, :] for part in range(4)], axis=1)
        carried_t = _dot_nt(woutt_scr[q], entering.astype(BF16))
        for g in range(2):
            y_t = dot_tn(m_scr[2 * q + g], ut_scr[2 * q + g]) + carried_t[g * blk:(g + 1) * blk]
            for i in range(sb):
                yt_scr[i, (2 * q + g) * ch:(2 * q + g + 1) * ch, :] = y_t[i * ch:(i + 1) * ch, :]

    for i in range(sb):
        y_ref[pl.ds(i, rows, stride=sb), :] = yt_scr[i].T


def _s5(proj, u_block0, b_t, c_t, powers, layer, bsz):
    t = proj.shape[0]
    rows = t // S5_BLOCK
    blk = S5_BLOCK * S5_GROUP_CH
    groups = LANE // S5_GROUP_CH
    pairs = groups // 2
    w = 2 * S5_STATE
    n_tiles = powers.shape[3] // pairs
    placed = pl.BlockSpec((None, 2, 2, pairs, 2, S5_GROUP_CH, w), lambda k: (layer, 0, 0, k, 0, 0, 0))
    return pl.pallas_call(
        functools.partial(_s5_kernel, bsz=bsz),
        grid=(n_tiles,),
        in_specs=[
            pl.BlockSpec((t, LANE), lambda k: (0, u_block0 + k)),
            placed, placed,
            pl.BlockSpec((None, 2, 2, pairs, S5_BLOCK + 1, w), lambda k: (layer, 0, 0, k, 0, 0)),
        ],
        out_specs=pl.BlockSpec((t, LANE), lambda k: (0, k)),
        out_shape=jax.ShapeDtypeStruct((t, n_tiles * LANE), F32),
        scratch_shapes=[
            pltpu.VMEM((groups, blk, rows), BF16),
            pltpu.VMEM((4, pairs * rows, w), F32),
            pltpu.VMEM((S5_BLOCK, LANE, rows), F32),
            pltpu.VMEM((groups, blk, blk), BF16),
            pltpu.VMEM((pairs, 2 * blk, 4 * w), BF16),
            pltpu.VMEM((pairs, 2 * blk, 4 * w), BF16),
        ],
        compiler_params=_compiler_params(("parallel",)),
        name="s5",
    )(proj, b_t, c_t, powers)


def _s5_params(lam_re, lam_im, log_dt, b_re, b_im, c_re, c_im):
    n_groups = lam_re.shape[1]
    sb, p, c = S5_BLOCK, S5_STATE, S5_GROUP_CH
    l_re, l_im = lam_re.astype(F32), lam_im.astype(F32)
    dt = jnp.exp(log_dt.astype(F32))[..., None]
    mag = jnp.exp(l_re * dt)
    bar_re, bar_im = mag * jnp.cos(l_im * dt), mag * jnp.sin(l_im * dt)
    n_re, n_im = bar_re - 1.0, bar_im
    den = l_re * l_re + l_im * l_im
    k_re = ((n_re * l_re + n_im * l_im) / den)[..., None]
    k_im = ((n_im * l_re - n_re * l_im) / den)[..., None]
    bb_re = k_re * b_re.astype(F32) - k_im * b_im.astype(F32)
    bb_im = k_re * b_im.astype(F32) + k_im * b_re.astype(F32)
    cc_re, cc_im = c_re.astype(F32), c_im.astype(F32)

    steps = jnp.arange(sb + 1, dtype=F32)[:, None, None, None]
    mag_l = jnp.exp(steps * (l_re * dt)[None])
    pw_re, pw_im = mag_l * jnp.cos(steps * (l_im * dt)[None]), mag_l * jnp.sin(steps * (l_im * dt)[None])

    same = jnp.eye(2, dtype=F32)
    def place(t):
        t2 = t.reshape(2, n_groups // 2, 2, c, 1, p) * same[None, None, :, None, :, None]
        return t2.reshape(2, n_groups // 2, 2, c, 2 * p)
    b_t = jnp.stack([place(jnp.swapaxes(bb_re, 2, 3)), place(jnp.swapaxes(bb_im, 2, 3))], axis=1)
    c_t = jnp.stack([place(cc_re), place(cc_im)], axis=1)
    def side_by_side(t):
        t2 = jnp.transpose(t.reshape(sb + 1, 2, n_groups // 2, 2, p), (1, 2, 0, 3, 4))
        return t2.reshape(2, n_groups // 2, sb + 1, 2 * p)
    powers = jnp.stack([side_by_side(pw_re), side_by_side(pw_im)], axis=1)
    return b_t, c_t, powers


def _out_kernel(oa_ref, za_ref, ys_ref, u_ref, zb_ref, ga_ref, gb_ref, x_ref,
                wpa_ref, dsk_ref, wglu_ref, bglu_ref, wpb_ref, bga_ref, bgb_ref, wout_ref, fg_ref,
                o_ref, *, final_norm):
    a = (oa_ref[...] * _silu(za_ref[...])).astype(BF16)
    y_a = jnp.dot(a, wpa_ref[...], preferred_element_type=F32)
    y_s = jax.nn.gelu(ys_ref[...] + u_ref[...] * dsk_ref[...])
    glu = jnp.dot(y_s.astype(BF16), wglu_ref[...], preferred_element_type=F32) + bglu_ref[...]
    y_s = y_s * jax.nn.sigmoid(glu)
    y_b = jnp.dot((y_s * _silu(zb_ref[...])).astype(BF16), wpb_ref[...], preferred_element_type=F32)
    merged = (jax.nn.sigmoid(ga_ref[...] + bga_ref[...]) * y_a
              + jax.nn.sigmoid(gb_ref[...] + bgb_ref[...]) * y_b)
    out = x_ref[...] + jnp.dot(merged.astype(BF16), wout_ref[...], preferred_element_type=F32)
    if final_norm:
        ms = jnp.mean(out * out, axis=-1, keepdims=True)
        out = out * lax.rsqrt(ms + RMS_EPS) * fg_ref[...]
    o_ref[...] = out


def _out_stage(o_a, proj, y_s5, x2, w_pa, d_skip, w_glu, b_glu, w_pb, b_gate, w_out, final_g, *,
               layer, za_block, u_block, zb_block, gate_block0, final_norm):
    t, d = x2.shape
    wa = o_a.shape[1]
    wb = y_s5.shape[1]
    tm = _largest_tile(t, 256, SUBLANE)
    row = lambda width, blk: pl.BlockSpec((tm, width), lambda i: (i, blk))
    const = lambda shape, blk=0: pl.BlockSpec((None,) + shape, lambda i: (layer, 0, blk),
                                              pipeline_mode=pl.Buffered(1))
    return pl.pallas_call(
        functools.partial(_out_kernel, final_norm=final_norm),
        grid=(t // tm,),
        in_specs=[
            row(wa, 0), row(wa, za_block), row(wb, 0), row(wb, u_block), row(wb, zb_block),
            row(d, gate_block0), row(d, gate_block0 + 1), row(d, 0),
            const((wa, d)), const((1, wb)), const((wb, wb)), const((1, wb)), const((wb, d)),
            const((1, d), 0), const((1, d), 1), const((d, d)), const((1, d)),
        ],
        out_specs=pl.BlockSpec((tm, d), lambda i: (i, 0)),
        out_shape=jax.ShapeDtypeStruct((t, d), F32),
        compiler_params=_compiler_params(("parallel",)),
        name="out_stage",
    )(o_a, proj, y_s5, proj, proj, proj, proj, x2, w_pa, d_skip, w_glu, b_glu, w_pb, b_gate, b_gate, w_out,
      final_g)


def _prepare_params(ln_g, w_in, conv_w, a_log, dt_bias, head_norm_g, lam_re, lam_im, log_dt, b_re, b_im,
                    c_re, c_im, d_skip, w_glu, b_glu, w_pa, w_pb, b_gate, w_out, final_g):
    depth, d, _ = w_in.shape
    n_heads = a_log.shape[2]
    wa = n_heads * HEAD_DIM
    wb = d_skip.shape[1]
    n_ba = 4 * n_heads
    o_beta = 4 * wa
    w_in16 = w_in.astype(BF16)
    w_cols = _regroup_weights(w_in16, o_beta, n_ba)
    w_ba = w_in16[:, :, o_beta:o_beta + LANE]
    pad_rows = lambda p: jnp.pad(p.astype(F32).reshape(depth, 1, n_ba // 2),
                                 ((0, 0), (0, 0), (n_ba // 2, LANE - n_ba)))
    row = lambda p: p.astype(F32).reshape(depth, 1, -1)
    return dict(
        ln_g=row(ln_g), w_cols=w_cols, w_ba=w_ba, conv_w=conv_w.astype(F32), alog=pad_rows(a_log), dtb=pad_rows(dt_bias),
        head_norm_g=row(head_norm_g),
        s5=jax.vmap(_s5_params)(lam_re, lam_im, log_dt, b_re, b_im, c_re, c_im),
        d_skip=row(d_skip), w_glu=w_glu.astype(BF16), b_glu=row(b_glu), w_pa=w_pa.astype(BF16),
        w_pb=w_pb.astype(BF16), b_gate=row(b_gate), w_out=w_out.astype(BF16),
        final_g=jnp.broadcast_to(final_g.astype(F32).reshape(1, 1, d), (depth, 1, d)),
        n_heads=n_heads, wa=wa, wb=wb)


def _layer(x2, bsz, seq, prm, layer, final_norm):
    t, d = x2.shape
    n_heads, wa, wb = prm["n_heads"], prm["wa"], prm["wb"]
    assert wa == wb and seq % (CHUNK * SUBLANE) == 0 and 4 * n_heads <= LANE
    c_qkv, c_za, c_beta, c_alpha, c_u, c_zb = 3 * wa, wa, 2 * n_heads, 2 * n_heads, wb, wb
    n_proj = prm["w_cols"].shape[2]

    proj, ba_logits = _inproj(x2, prm["ln_g"], prm["w_cols"], prm["w_ba"], layer)
    proj3 = proj.reshape(bsz, seq, n_proj)

    gates = _gates(ba_logits, prm["alog"], prm["dtb"], layer, n_heads)
    beta = gates[:, :c_beta].reshape(bsz, seq, 2, n_heads)
    gc = gates[:, c_beta:c_beta + c_alpha].reshape(bsz, seq, 2, n_heads)
    hpb = DELTA_HEADS_PER_PROGRAM
    cols = jnp.stack([gc[:, :, 0], gc[:, :, 1], beta[:, :, 0], beta[:, :, 1]], axis=-1)
    cols = jnp.transpose(cols.reshape(bsz, seq, n_heads // hpb, 4 * hpb), (0, 2, 1, 3))
    rows = jnp.transpose(gc.reshape(bsz, seq // CHUNK, CHUNK, 2, n_heads), (0, 4, 1, 3, 2))
    rows = rows.reshape(bsz, n_heads, seq // CHUNK, 1, 2 * CHUNK)

    o_a = _delta(proj3, prm["conv_w"], cols, rows, prm["head_norm_g"], layer, n_heads)
    o_a = o_a.reshape(t, wa)

    u_off = c_qkv + c_za
    y_s5 = _s5(proj, u_off // LANE, *prm["s5"], layer, bsz)

    return _out_stage(
        o_a, proj, y_s5, x2, prm["w_pa"], prm["d_skip"], prm["w_glu"], prm["b_glu"], prm["w_pb"],
        prm["b_gate"], prm["w_out"], prm["final_g"], layer=layer,
        za_block=c_qkv // wa, u_block=u_off // wb, zb_block=(u_off + c_u) // wb,
        gate_block0=(u_off + c_u + c_zb) // d, final_norm=final_norm)


def kernel(x, ln_g, w_in, conv_w, a_log, dt_bias, head_norm_g, lam_re, lam_im, log_dt, b_re, b_im, c_re, c_im, d_skip, w_glu, b_glu, w_pa, w_pb, b_gate, w_out, final_g):
    bsz, seq, d = x.shape
    depth = ln_g.shape[0]
    prm = _prepare_params(ln_g, w_in, conv_w, a_log, dt_bias, head_norm_g, lam_re, lam_im, log_dt, b_re, b_im,
                          c_re, c_im, d_skip, w_glu, b_glu, w_pa, w_pb, b_gate, w_out, final_g)
    x2 = x.reshape(bsz * seq, d)
    for layer in range(depth):
        x2 = _layer(x2, bsz, seq, prm, layer, final_norm=(layer == depth - 1))
    return x2.reshape(bsz, seq, d)
```

```python
import functools
import math

import jax
import jax.numpy as jnp
from jax import lax
from jax.experimental import pallas as pl
from jax.experimental.pallas import tpu as pltpu

F32 = jnp.float32
BF16 = jnp.bfloat16

LANE = 128
SUBLANE = 8
HEAD_DIM = 128
CHUNK = 64
CONV_K = 5
CONV_PAD = (CONV_K - 1) // 2
DELTA_HEADS_PER_PROGRAM = 2
LEVEL_GROUP_CHUNKS = 32
S5_GROUP_CH = 16
S5_STATE = 64
S5_BLOCK = 16
RMS_EPS = 1e-6
VMEM_LIMIT_BYTES = 56 * 1024 * 1024


def _compiler_params(semantics):
    return pltpu.CompilerParams(dimension_semantics=semantics, vmem_limit_bytes=VMEM_LIMIT_BYTES)


def _silu(x):
    return x * jax.nn.sigmoid(x)


def _largest_tile(n, cap, unit):
    best = unit
    t = unit
    while t <= min(n, cap):
        if n % t == 0:
            best = t
        t += unit
    return best


def _regroup_kernel(a_ref, b_ref, o_ref, *, n_plain, shift):
    @pl.when(pl.program_id(2) < n_plain)
    def _():
        o_ref[...] = a_ref[...]

    @pl.when(pl.program_id(2) >= n_plain)
    def _():
        both = jnp.concatenate([a_ref[...], b_ref[...]], axis=1)
        o_ref[...] = both[:, shift:shift + o_ref.shape[1]]


def _regroup_weights(w, o_skip, n_skip):
    depth, d, n = w.shape
    n_out = n - n_skip
    tn = _largest_tile(math.gcd(o_skip, n_out - o_skip), 1024, LANE)
    tr = _largest_tile(d, 2048, 2 * SUBLANE)
    return pl.pallas_call(
        functools.partial(_regroup_kernel, n_plain=o_skip // tn, shift=n_skip),
        grid=(depth, d // tr, n_out // tn),
        in_specs=[
            pl.BlockSpec((None, tr, tn), lambda l, r, j: (l, r, j)),
            pl.BlockSpec((None, tr, LANE), lambda l, r, j: (l, r, (j + 1) * (tn // LANE))),
        ],
        out_specs=pl.BlockSpec((None, tr, tn), lambda l, r, j: (l, r, j)),
        out_shape=jax.ShapeDtypeStruct((depth, d, n_out), w.dtype),
        compiler_params=_compiler_params(("parallel", "parallel", "parallel")),
        name="regroup",
    )(w, w)


def _inproj_kernel(x_ref, g_ref, w_ref, wba_ref, o_ref, ba_ref, h_ref):
    @pl.when(pl.program_id(1) == 0)
    def _():
        x = x_ref[...]
        ms = jnp.mean(x * x, axis=-1, keepdims=True)
        h_ref[...] = (x * lax.rsqrt(ms + RMS_EPS) * g_ref[...]).astype(BF16)
        ba_ref[...] = jnp.dot(h_ref[...], wba_ref[...], preferred_element_type=F32)

    o_ref[...] = jnp.dot(h_ref[...], w_ref[...], preferred_element_type=F32)


def _inproj(x2, g_rows, w_bf16, w_ba, layer):
    t, d = x2.shape
    n = w_bf16.shape[2]
    tm = _largest_tile(t, 1024, SUBLANE)
    tn = _largest_tile(n, 1280, LANE)
    return pl.pallas_call(
        _inproj_kernel,
        grid=(t // tm, n // tn),
        in_specs=[
            pl.BlockSpec((tm, d), lambda i, j: (i, 0)),
            pl.BlockSpec((None, 1, d), lambda i, j: (layer, 0, 0)),
            pl.BlockSpec((None, d, tn), lambda i, j: (layer, 0, j)),
            pl.BlockSpec((None, d, LANE), lambda i, j: (layer, 0, 0)),
        ],
        out_specs=[pl.BlockSpec((tm, tn), lambda i, j: (i, j)),
                   pl.BlockSpec((tm, LANE), lambda i, j: (i, 0))],
        out_shape=[jax.ShapeDtypeStruct((t, n), F32), jax.ShapeDtypeStruct((t, LANE), F32)],
        scratch_shapes=[pltpu.VMEM((tm, d), BF16)],
        compiler_params=_compiler_params(("parallel", "arbitrary")),
        name="inproj",
    )(x2, g_rows, w_bf16, w_ba)


def _gates_kernel(ba_ref, alog_ref, dtb_ref, o_ref, *, n_heads):
    lane = lax.broadcasted_iota(jnp.int32, (CHUNK, LANE), 1)
    r = lax.broadcasted_iota(jnp.int32, (CHUNK, CHUNK), 0)
    c = lax.broadcasted_iota(jnp.int32, (CHUNK, CHUNK), 1)
    m_prefix = jnp.where(c <= r, 1.0, 0.0).astype(F32)
    m_suffix = jnp.where(c >= r, 1.0, 0.0).astype(F32)
    within_chunk = functools.partial(jnp.dot, precision=lax.Precision.HIGHEST, preferred_element_type=F32)
    neg_rate = -jnp.exp(alog_ref[...])
    for r0 in range(0, ba_ref.shape[0], CHUNK):
        x = ba_ref[r0:r0 + CHUNK, :]
        z = x + dtb_ref[...]
        g = neg_rate * (jnp.maximum(z, 0.0) + jnp.log1p(jnp.exp(-jnp.abs(z))))
        gc = jnp.where(lane < 3 * n_heads, within_chunk(m_prefix, g), within_chunk(m_suffix, g))
        o_ref[r0:r0 + CHUNK, :] = jnp.where(lane < 2 * n_heads, jax.nn.sigmoid(x), gc)


def _gates(ba_logits, alog_rows, dtb_rows, layer, n_heads):
    t = ba_logits.shape[0]
    tm = _largest_tile(t, 512, CHUNK)
    return pl.pallas_call(
        functools.partial(_gates_kernel, n_heads=n_heads),
        grid=(t // tm,),
        in_specs=[
            pl.BlockSpec((tm, LANE), lambda i: (i, 0)),
            pl.BlockSpec((None, 1, LANE), lambda i: (layer, 0, 0)),
            pl.BlockSpec((None, 1, LANE), lambda i: (layer, 0, 0)),
        ],
        out_specs=pl.BlockSpec((tm, LANE), lambda i: (i, 0)),
        out_shape=jax.ShapeDtypeStruct((t, LANE), F32),
        compiler_params=_compiler_params(("parallel",)),
        name="gates",
    )(ba_logits, alog_rows, dtb_rows)


def _dot_nt(a, b):
    return lax.dot_general(a, b, (((1,), (1,)), ((), ())), preferred_element_type=F32)


def _delta_kernel(q_ref, k_ref, v_ref, wq_ref, wk_ref, wv_ref, cols_ref, rows_ref, hg_ref, o_ref,
                  pad_scr, r_scr, mp_scr, wq_scr, kdt_scr, qk_scr, od_scr, s_scr):
    seq = q_ref.shape[0]
    hpb = q_ref.shape[1] // HEAD_DIM
    n_chunks = seq // CHUNK
    halo = SUBLANE
    c2 = 2 * CHUNK
    operands = ((q_ref, wq_ref), (k_ref, wk_ref), (v_ref, wv_ref))
    for j in range(len(operands)):
        pad_scr[j, 0:halo, :] = jnp.zeros((halo, HEAD_DIM), F32)
        pad_scr[j, seq + halo:seq + 2 * halo, :] = jnp.zeros((halo, HEAD_DIM), F32)

    def conv_silu(j, w, r0):
        acc = None
        for i in range(CONV_K):
            term = pad_scr[j, pl.ds(r0 + halo + i - CONV_PAD, CHUNK), :] * w[i:i + 1, :]
            acc = term if acc is None else acc + term
        return _silu(acc)

    def l2norm(y):
        return y * lax.rsqrt(jnp.sum(y * y, axis=-1, keepdims=True) + RMS_EPS)

    top_rows = lax.broadcasted_iota(jnp.int32, (c2, c2), 0) < CHUNK
    rw = lax.broadcasted_iota(jnp.int32, (CHUNK, c2), 0)
    cw = lax.broadcasted_iota(jnp.int32, (CHUNK, c2), 1)
    fwd_lanes = cw < CHUNK
    cs = jnp.where(fwd_lanes, cw, cw - CHUNK)
    incl_wide = (fwd_lanes & (rw >= cs)) | (~fwd_lanes & (rw <= cs))
    strict_wide = (fwd_lanes & (rw > cs)) | (~fwd_lanes & (rw < cs))
    eye_wide = jnp.where(rw == cs, 1.0, 0.0).astype(F32)

    def block_diag(cat):
        return jnp.concatenate([jnp.where(fwd_lanes, cat, 0.0), jnp.where(fwd_lanes, 0.0, cat)], axis=0)

    for p in range(hpb):
        lanes = slice(p * HEAD_DIM, (p + 1) * HEAD_DIM)

        def fill(n, carry, lanes=lanes):
            r0 = pl.multiple_of(n * CHUNK, CHUNK)
            for j, (src_ref, _) in enumerate(operands):
                pad_scr[j, pl.ds(r0 + halo, CHUNK), :] = src_ref[pl.ds(r0, CHUNK), lanes]
            return carry

        lax.fori_loop(0, n_chunks, fill, 0, unroll=2)

        def setup(n, carry, p=p, lanes=lanes):
            r0 = pl.multiple_of(n * CHUNK, CHUNK)
            q = l2norm(conv_silu(0, wq_ref[:, lanes], r0)) * (HEAD_DIM ** -0.5)
            k = l2norm(conv_silu(1, wk_ref[:, lanes], r0))
            v = conv_silu(2, wv_ref[:, lanes], r0)
            k16 = k.astype(BF16)
            k2_16 = jnp.concatenate([k16, k16], axis=0)
            gram = _dot_nt(jnp.concatenate([q.astype(BF16), k16], axis=0), k2_16)
            qk_wide, kk_wide = gram[:CHUNK], gram[CHUNK:]
            cols = cols_ref[pl.ds(r0, CHUNK), :]
            across = lambda c: jnp.broadcast_to(cols[:, 4 * p + c:4 * p + c + 1], (CHUNK, c2))
            gc2 = jnp.concatenate([across(0), across(1)], axis=0)
            be2 = jnp.concatenate([across(2), across(3)], axis=0)
            gc_row2 = rows_ref[p, n]
            k2 = jnp.concatenate([k, k], axis=0)
            v2 = jnp.concatenate([v, v], axis=0)
            q2 = jnp.concatenate([q, q], axis=0)
            gc_wide = jnp.where(fwd_lanes, gc2[:CHUNK], gc2[CHUNK:])
            be_wide = jnp.where(fwd_lanes, be2[:CHUNK], be2[CHUNK:])
            decay_wide = jnp.exp(gc_wide - gc_row2)
            mp_scr[p, n, 0:CHUNK, :] = -(be_wide * kk_wide * jnp.where(strict_wide, decay_wide, 0.0))
            mp_scr[p, n, CHUNK:c2, :] = eye_wide
            eg2 = jnp.exp(gc2)
            r_scr[p, n] = jnp.concatenate([v2 * be2, k2 * (be2 * eg2)], axis=1)
            qg2 = (q2 * eg2).astype(BF16)
            wq_scr[p, 0, n, CHUNK:c2, :] = qg2[:CHUNK]
            wq_scr[p, 1, n, CHUNK:c2, :] = qg2[CHUNK:]
            g_last2 = jnp.where(top_rows, gc_row2[:, CHUNK - 1:CHUNK], gc_row2[:, CHUNK:CHUNK + 1])
            kdec2 = k2 * jnp.exp(g_last2 - gc2)
            kdt_scr[p, n] = kdec2.T.astype(BF16)
            qk_scr[p, n] = (qk_wide * jnp.where(incl_wide, decay_wide, 0.0)).astype(BF16)
            return carry

        lax.fori_loop(0, n_chunks, setup, 0, unroll=8)

    group = _largest_tile(n_chunks, LEVEL_GROUP_CHUNKS, 1)

    def group_tiles(it):
        return [(p, it * group + g) for g in range(group) for p in range(hpb)]

    def level(_, carry):
        def per_group(it, carry):
            tiles = group_tiles(it)
            outs = []
            for p, n in tiles:
                mp = mp_scr[p, n]
                outs.append(jnp.dot(mp.astype(BF16), block_diag(mp[:CHUNK]).astype(BF16),
                                    preferred_element_type=F32))
            for (p, n), out in zip(tiles, outs):
                mp_scr[p, n, 0:CHUNK, :] = out[:CHUNK]
                mp_scr[p, n, CHUNK:c2, :] += out[CHUNK:]
            return carry

        return lax.fori_loop(0, n_chunks // group, per_group, carry)

    lax.fori_loop(0, int(math.log2(CHUNK)), level, 0)

    def apply_inverse(it, carry):
        tiles = group_tiles(it)
        outs = [jnp.dot(block_diag(mp_scr[p, n, CHUNK:c2, :]).astype(BF16), r_scr[p, n].astype(BF16),
                        preferred_element_type=F32) for p, n in tiles]
        for (p, n), uw in zip(tiles, outs):
            r_scr[p, n] = uw
            w2 = uw[:, HEAD_DIM:].astype(BF16)
            wq_scr[p, 0, n, 0:CHUNK, :] = w2[:CHUNK]
            wq_scr[p, 1, n, 0:CHUNK, :] = w2[CHUNK:]
        return carry

    lax.fori_loop(0, n_chunks // group, apply_inverse, 0)

    s_scr[...] = jnp.zeros(s_scr.shape, F32)
    zero_half = jnp.zeros((CHUNK, HEAD_DIM), BF16)
    chains = [(p, d) for p in range(hpb) for d in range(2)]

    def recurrence(i, carry):
        chunk_of = (i, n_chunks - 1 - i)
        states = [s_scr[p, d] for p, d in chains]
        ws = [jnp.dot(wq_scr[p, d, chunk_of[d]], s.astype(BF16), preferred_element_type=F32)
              for (p, d), s in zip(chains, states)]
        new_states = []
        for (p, d), s, w in zip(chains, states, ws):
            n = chunk_of[d]
            r0 = pl.multiple_of(n * CHUNK, CHUNK)
            v_new = (r_scr[p, n, d * CHUNK:(d + 1) * CHUNK, 0:HEAD_DIM] - w[:CHUNK]).astype(BF16)
            v_pad = jnp.concatenate([v_new, zero_half] if d == 0 else [zero_half, v_new], axis=0)
            o = w[CHUNK:] + jnp.dot(qk_scr[p, n], v_pad, preferred_element_type=F32)
            od_scr[p, d, pl.ds(r0, CHUNK), :] = o
            gc_row2 = rows_ref[p, n]
            g_last = gc_row2[:, CHUNK - 1:CHUNK] if d == 0 else gc_row2[:, CHUNK:CHUNK + 1]
            new_states.append(s * jnp.exp(g_last) + jnp.dot(kdt_scr[p, n], v_pad, preferred_element_type=F32))
        for (p, d), s in zip(chains, new_states):
            s_scr[p, d] = s
        return carry

    lax.fori_loop(0, n_chunks, recurrence, 0)

    def head_norm(n, carry):
        r0 = pl.multiple_of(n * CHUNK, CHUNK)
        for p in range(hpb):
            lanes = slice(p * HEAD_DIM, (p + 1) * HEAD_DIM)
            o = od_scr[p, 0, pl.ds(r0, CHUNK), :] + od_scr[p, 1, pl.ds(r0, CHUNK), :]
            ms = jnp.mean(o * o, axis=-1, keepdims=True)
            o_ref[pl.ds(r0, CHUNK), lanes] = o * lax.rsqrt(ms + RMS_EPS) * hg_ref[...]
        return carry

    lax.fori_loop(0, n_chunks, head_norm, 0, unroll=4)


def _delta(proj3, conv_w, cols, rows, hg_rows, layer, n_heads):
    bsz, seq, _ = proj3.shape
    n_chunks = seq // CHUNK
    hpb = DELTA_HEADS_PER_PROGRAM
    width = hpb * HEAD_DIM
    n_blocks = n_heads // hpb
    tok = lambda off: pl.BlockSpec((None, seq, width), lambda b, h: (b, 0, off + h))
    cw = lambda off: pl.BlockSpec((None, CONV_K, width), lambda b, h: (layer, 0, off + h))
    return pl.pallas_call(
        _delta_kernel,
        grid=(bsz, n_blocks),
        in_specs=[
            tok(0), tok(n_blocks), tok(2 * n_blocks),
            cw(0), cw(n_blocks), cw(2 * n_blocks),
            pl.BlockSpec((None, None, seq, 4 * hpb), lambda b, h: (b, h, 0, 0)),
            pl.BlockSpec((None, hpb, n_chunks, 1, 2 * CHUNK), lambda b, h: (b, h, 0, 0, 0)),
            pl.BlockSpec((None, 1, HEAD_DIM), lambda b, h: (layer, 0, 0)),
        ],
        out_specs=pl.BlockSpec((None, seq, width), lambda b, h: (b, 0, h)),
        out_shape=jax.ShapeDtypeStruct((bsz, seq, n_heads * HEAD_DIM), F32),
        scratch_shapes=[
            pltpu.VMEM((3, seq + 2 * SUBLANE, HEAD_DIM), F32),
            pltpu.VMEM((hpb, n_chunks, 2 * CHUNK, 2 * HEAD_DIM), F32),
            pltpu.VMEM((hpb, n_chunks, 2 * CHUNK, 2 * CHUNK), F32),
            pltpu.VMEM((hpb, 2, n_chunks, 2 * CHUNK, HEAD_DIM), BF16),
            pltpu.VMEM((hpb, n_chunks, HEAD_DIM, 2 * CHUNK), BF16),
            pltpu.VMEM((hpb, n_chunks, CHUNK, 2 * CHUNK), BF16),
            pltpu.VMEM((hpb, 2, seq, HEAD_DIM), F32),
            pltpu.VMEM((hpb, 2, HEAD_DIM, HEAD_DIM), F32),
        ],
        compiler_params=_compiler_params(("parallel", "parallel")),
        name="delta",
    )(proj3, proj3, proj3, conv_w, conv_w, conv_w, cols, rows, hg_rows)


def _s5_kernel(u_ref, bt_ref, ct_ref, pw_ref, y_ref,
               ut_scr, s_scr, yt_scr, m_scr, wsum_scr, woutt_scr, *, bsz):
    sb, ch = S5_BLOCK, S5_GROUP_CH
    rows = u_ref.shape[0] // sb
    groups = LANE // ch
    w = 2 * S5_STATE
    blk = sb * ch

    def cmul(a_re, a_im, b_re, b_im):
        return a_re * b_re - a_im * b_im, a_re * b_im + a_im * b_re

    def lag_strip(d, q, h, lags):
        z = [cmul(ct_ref[d, 0, q, h], ct_ref[d, 1, q, h], pw_ref[d, 0, q][l:l + 1], pw_ref[d, 1, q][l:l + 1])
             for l in lags]
        z_re = jnp.concatenate([t[0] for t in z], axis=0)
        z_im = jnp.concatenate([t[1] for t in z], axis=0)
        over_states = functools.partial(lax.dot_general, dimension_numbers=(((1,), (1,)), ((), ())),
                                        precision=lax.Precision.HIGHEST, preferred_element_type=F32)
        return over_states(bt_ref[d, 0, q, h], z_re) - over_states(bt_ref[d, 1, q, h], z_im)

    lane = lax.broadcasted_iota(jnp.int32, (ch, blk), 1)
    for g in range(groups):
        k_f = lag_strip(0, g // 2, g % 2, range(sb))
        k_b = lag_strip(1, g // 2, g % 2, range(sb - 1, -1, -1))
        for j in range(sb):
            fwd = jnp.where(lane >= j * ch, pltpu.roll(k_f, j * ch, axis=1), 0.0)
            bwd = jnp.where(lane < (j + 1) * ch, pltpu.roll(k_b, (blk - (sb - 1 - j) * ch) % blk, axis=1), 0.0)
            m_scr[g, j * ch:(j + 1) * ch, :] = (fwd + bwd).astype(BF16)

    for j in range(sb):
        t = u_ref[pl.ds(j, rows, stride=sb), :].T.astype(BF16)
        for g in range(groups):
            ut_scr[g, j * ch:(j + 1) * ch, :] = t[g * ch:(g + 1) * ch, :]

    def dot_tn(a, b):
        return lax.dot_general(a, b, (((0,), (0,)), ((), ())), preferred_element_type=F32)

    n_blocks = rows // bsz
    pairs = groups // 2
    for q in range(pairs):
        pw = [[pw_ref[d, part, q] for part in range(2)] for d in range(2)]
        for h in range(2):
            bt = [[bt_ref[d, part, q, h] for part in range(2)] for d in range(2)]
            ct = [[ct_ref[d, part, q, h] for part in range(2)] for d in range(2)]
            for j in range(sb):
                r0 = h * blk + j * ch
                for d, (l_sum, l_out) in enumerate(((sb - 1 - j, j + 1), (j, sb - j))):
                    s_re, s_im = cmul(bt[d][0], bt[d][1], pw[d][0][l_sum:l_sum + 1], pw[d][1][l_sum:l_sum + 1])
                    z_re, z_im = cmul(ct[d][0], ct[d][1], pw[d][0][l_out:l_out + 1], pw[d][1][l_out:l_out + 1])
                    wsum_scr[q, r0:r0 + ch, 2 * d * w:(2 * d + 1) * w] = s_re.astype(BF16)
                    wsum_scr[q, r0:r0 + ch, (2 * d + 1) * w:(2 * d + 2) * w] = s_im.astype(BF16)
                    woutt_scr[q, r0:r0 + ch, 2 * d * w:(2 * d + 1) * w] = z_re.astype(BF16)
                    woutt_scr[q, r0:r0 + ch, (2 * d + 1) * w:(2 * d + 2) * w] = (-z_im).astype(BF16)

        ut2 = jnp.concatenate([ut_scr[2 * q], ut_scr[2 * q + 1]], axis=0)
        summaries = dot_tn(ut2, wsum_scr[q])
        for part in range(4):
            s_scr[part, q * rows:(q + 1) * rows, :] = summaries[:, part * w:(part + 1) * w]

    chains = pairs * bsz
    decay = [jnp.concatenate([jnp.broadcast_to(pw_ref[d, part, q][sb:sb + 1], (bsz, w)) for q in range(pairs)], axis=0)
             for d in range(2) for part in range(2)]

    def step(k, xs):
        xf_re, xf_im, xb_re, xb_im = xs
        rf = pl.ds(k, chains, stride=n_blocks)
        rb = pl.ds(n_blocks - 1 - k, chains, stride=n_blocks)
        sf_re, sf_im = s_scr[0, rf, :], s_scr[1, rf, :]
        sb_re, sb_im = s_scr[2, rb, :], s_scr[3, rb, :]
        s_scr[0, rf, :] = xf_re
        s_scr[1, rf, :] = xf_im
        s_scr[2, rb, :] = xb_re
        s_scr[3, rb, :] = xb_im
        nf_re, nf_im = cmul(decay[0], decay[1], xf_re, xf_im)
        nb_re, nb_im = cmul(decay[2], decay[3], xb_re, xb_im)
        return nf_re + sf_re, nf_im + sf_im, nb_re + sb_re, nb_im + sb_im

    zero = jnp.zeros((chains, w), F32)
    lax.fori_loop(0, n_blocks, step, (zero, zero, zero, zero), unroll=2)

    for q in range(pairs):
        entering = jnp.concatenate([s_scr[part, q * rows:(q + 1) * rows, :] for part in range(4)], axis=1)
        carried_t = _dot_nt(woutt_scr[q], entering.astype(BF16))
        for g in range(2):
            y_t = dot_tn(m_scr[2 * q + g], ut_scr[2 * q + g]) + carried_t[g * blk:(g + 1) * blk]
            for i in range(sb):
                yt_scr[i, (2 * q + g) * ch:(2 * q + g + 1) * ch, :] = y_t[i * ch:(i + 1) * ch, :]

    for i in range(sb):
        y_ref[pl.ds(i, rows, stride=sb), :] = yt_scr[i].T


def _s5(proj, u_block0, b_t, c_t, powers, layer, bsz):
    t = proj.shape[0]
    rows = t // S5_BLOCK
    blk = S5_BLOCK * S5_GROUP_CH
    groups = LANE // S5_GROUP_CH
    pairs = groups // 2
    w = 2 * S5_STATE
    n_tiles = powers.shape[3] // pairs
    placed = pl.BlockSpec((None, 2, 2, pairs, 2, S5_GROUP_CH, w), lambda k: (layer, 0, 0, k, 0, 0, 0))
    return pl.pallas_call(
        functools.partial(_s5_kernel, bsz=bsz),
        grid=(n_tiles,),
        in_specs=[
            pl.BlockSpec((t, LANE), lambda k: (0, u_block0 + k)),
            placed, placed,
            pl.BlockSpec((None, 2, 2, pairs, S5_BLOCK + 1, w), lambda k: (layer, 0, 0, k, 0, 0)),
        ],
        out_specs=pl.BlockSpec((t, LANE), lambda k: (0, k)),
        out_shape=jax.ShapeDtypeStruct((t, n_tiles * LANE), F32),
        scratch_shapes=[
            pltpu.VMEM((groups, blk, rows), BF16),
            pltpu.VMEM((4, pairs * rows, w), F32),
            pltpu.VMEM((S5_BLOCK, LANE, rows), F32),
            pltpu.VMEM((groups, blk, blk), BF16),
            pltpu.VMEM((pairs, 2 * blk, 4 * w), BF16),
            pltpu.VMEM((pairs, 2 * blk, 4 * w), BF16),
        ],
        compiler_params=_compiler_params(("parallel",)),
        name="s5",
    )(proj, b_t, c_t, powers)


def _s5_params(lam_re, lam_im, log_dt, b_re, b_im, c_re, c_im):
    n_groups = lam_re.shape[1]
    sb, p, c = S5_BLOCK, S5_STATE, S5_GROUP_CH
    l_re, l_im = lam_re.astype(F32), lam_im.astype(F32)
    dt = jnp.exp(log_dt.astype(F32))[..., None]
    mag = jnp.exp(l_re * dt)
    bar_re, bar_im = mag * jnp.cos(l_im * dt), mag * jnp.sin(l_im * dt)
    n_re, n_im = bar_re - 1.0, bar_im
    den = l_re * l_re + l_im * l_im
    k_re = ((n_re * l_re + n_im * l_im) / den)[..., None]
    k_im = ((n_im * l_re - n_re * l_im) / den)[..., None]
    bb_re = k_re * b_re.astype(F32) - k_im * b_im.astype(F32)
    bb_im = k_re * b_im.astype(F32) + k_im * b_re.astype(F32)
    cc_re, cc_im = c_re.astype(F32), c_im.astype(F32)

    steps = jnp.arange(sb + 1, dtype=F32)[:, None, None, None]
    mag_l = jnp.exp(steps * (l_re * dt)[None])
    pw_re, pw_im = mag_l * jnp.cos(steps * (l_im * dt)[None]), mag_l * jnp.sin(steps * (l_im * dt)[None])

    same = jnp.eye(2, dtype=F32)
    def place(t):
        t2 = t.reshape(2, n_groups // 2, 2, c, 1, p) * same[None, None, :, None, :, None]
        return t2.reshape(2, n_groups // 2, 2, c, 2 * p)
    b_t = jnp.stack([place(jnp.swapaxes(bb_re, 2, 3)), place(jnp.swapaxes(bb_im, 2, 3))], axis=1)
    c_t = jnp.stack([place(cc_re), place(cc_im)], axis=1)
    def side_by_side(t):
        t2 = jnp.transpose(t.reshape(sb + 1, 2, n_groups // 2, 2, p), (1, 2, 0, 3, 4))
        return t2.reshape(2, n_groups // 2, sb + 1, 2 * p)
    powers = jnp.stack([side_by_side(pw_re), side_by_side(pw_im)], axis=1)
    return b_t, c_t, powers


def _out_kernel(oa_ref, za_ref, ys_ref, u_ref, zb_ref, ga_ref, gb_ref, x_ref,
                wpa_ref, dsk_ref, wglu_ref, bglu_ref, wpb_ref, bga_ref, bgb_ref, wout_ref, fg_ref,
                o_ref, *, final_norm):
    a = (oa_ref[...] * _silu(za_ref[...])).astype(BF16)
    y_a = jnp.dot(a, wpa_ref[...], preferred_element_type=F32)
    y_s = jax.nn.gelu(ys_ref[...] + u_ref[...] * dsk_ref[...])
    glu = jnp.dot(y_s.astype(BF16), wglu_ref[...], preferred_element_type=F32) + bglu_ref[...]
    y_s = y_s * jax.nn.sigmoid(glu)
    y_b = jnp.dot((y_s * _silu(zb_ref[...])).astype(BF16), wpb_ref[...], preferred_element_type=F32)
    merged = (jax.nn.sigmoid(ga_ref[...] + bga_ref[...]) * y_a
              + jax.nn.sigmoid(gb_ref[...] + bgb_ref[...]) * y_b)
    out = x_ref[...] + jnp.dot(merged.astype(BF16), wout_ref[...], preferred_element_type=F32)
    if final_norm:
        ms = jnp.mean(out * out, axis=-1, keepdims=True)
        out = out * lax.rsqrt(ms + RMS_EPS) * fg_ref[...]
    o_ref[...] = out


def _out_stage(o_a, proj, y_s5, x2, w_pa, d_skip, w_glu, b_glu, w_pb, b_gate, w_out, final_g, *,
               layer, za_block, u_block, zb_block, gate_block0, final_norm):
    t, d = x2.shape
    wa = o_a.shape[1]
    wb = y_s5.shape[1]
    tm = _largest_tile(t, 256, SUBLANE)
    row = lambda width, blk: pl.BlockSpec((tm, width), lambda i: (i, blk))
    const = lambda shape, blk=0: pl.BlockSpec((None,) + shape, lambda i: (layer, 0, blk),
                                              pipeline_mode=pl.Buffered(1))
    return pl.pallas_call(
        functools.partial(_out_kernel, final_norm=final_norm),
        grid=(t // tm,),
        in_specs=[
            row(wa, 0), row(wa, za_block), row(wb, 0), row(wb, u_block), row(wb, zb_block),
            row(d, gate_block0), row(d, gate_block0 + 1), row(d, 0),
            const((wa, d)), const((1, wb)), const((wb, wb)), const((1, wb)), const((wb, d)),
            const((1, d), 0), const((1, d), 1), const((d, d)), const((1, d)),
        ],
        out_specs=pl.BlockSpec((tm, d), lambda i: (i, 0)),
        out_shape=jax.ShapeDtypeStruct((t, d), F32),
        compiler_params=_compiler_params(("parallel",)),
        name="out_stage",
    )(o_a, proj, y_s5, proj, proj, proj, proj, x2, w_pa, d_skip, w_glu, b_glu, w_pb, b_gate, b_gate, w_out,
      final_g)


def _prepare_params(ln_g, w_in, conv_w, a_log, dt_bias, head_norm_g, lam_re, lam_im, log_dt, b_re, b_im,
                    c_re, c_im, d_skip, w_glu, b_glu, w_pa, w_pb, b_gate, w_out, final_g):
    depth, d, _ = w_in.shape
    n_heads = a_log.shape[2]
    wa = n_heads * HEAD_DIM
    wb = d_skip.shape[1]
    n_ba = 4 * n_heads
    o_beta = 4 * wa
    w_in16 = w_in.astype(BF16)
    w_cols = _regroup_weights(w_in16, o_beta, n_ba)
    w_ba = w_in16[:, :, o_beta:o_beta + LANE]
    pad_rows = lambda p: jnp.pad(p.astype(F32).reshape(depth, 1, n_ba // 2),
                                 ((0, 0), (0, 0), (n_ba // 2, LANE - n_ba)))
    row = lambda p: p.astype(F32).reshape(depth, 1, -1)
    return dict(
        ln_g=row(ln_g), w_cols=w_cols, w_ba=w_ba, conv_w=conv_w.astype(F32), alog=pad_rows(a_log), dtb=pad_rows(dt_bias),
        head_norm_g=row(head_norm_g),
        s5=jax.vmap(_s5_params)(lam_re, lam_im, log_dt, b_re, b_im, c_re, c_im),
        d_skip=row(d_skip), w_glu=w_glu.astype(BF16), b_glu=row(b_glu), w_pa=w_pa.astype(BF16),
        w_pb=w_pb.astype(BF16), b_gate=row(b_gate), w_out=w_out.astype(BF16),
        final_g=jnp.broadcast_to(final_g.astype(F32).reshape(1, 1, d), (depth, 1, d)),
        n_heads=n_heads, wa=wa, wb=wb)


def _layer(x2, bsz, seq, prm, layer, final_norm):
    t, d = x2.shape
    n_heads, wa, wb = prm["n_heads"], prm["wa"], prm["wb"]
    assert wa == wb and seq % (CHUNK * SUBLANE) == 0 and 4 * n_heads <= LANE
    c_qkv, c_za, c_beta, c_alpha, c_u, c_zb = 3 * wa, wa, 2 * n_heads, 2 * n_heads, wb, wb
    n_proj = prm["w_cols"].shape[2]

    proj, ba_logits = _inproj(x2, prm["ln_g"], prm["w_cols"], prm["w_ba"], layer)
    proj3 = proj.reshape(bsz, seq, n_proj)

    gates = _gates(ba_logits, prm["alog"], prm["dtb"], layer, n_heads)
    beta = gates[:, :c_beta].reshape(bsz, seq, 2, n_heads)
    gc = gates[:, c_beta:c_beta + c_alpha].reshape(bsz, seq, 2, n_heads)
    hpb = DELTA_HEADS_PER_PROGRAM
    cols = jnp.stack([gc[:, :, 0], gc[:, :, 1], beta[:, :, 0], beta[:, :, 1]], axis=-1)
    cols = jnp.transpose(cols.reshape(bsz, seq, n_heads // hpb, 4 * hpb), (0, 2, 1, 3))
    rows = jnp.transpose(gc.reshape(bsz, seq // CHUNK, CHUNK, 2, n_heads), (0, 4, 1, 3, 2))
    rows = rows.reshape(bsz, n_heads, seq // CHUNK, 1, 2 * CHUNK)

    o_a = _delta(proj3, prm["conv_w"], cols, rows, prm["head_norm_g"], layer, n_heads)
    o_a = o_a.reshape(t, wa)

    u_off = c_qkv + c_za
    y_s5 = _s5(proj, u_off // LANE, *prm["s5"], layer, bsz)

    return _out_stage(
        o_a, proj, y_s5, x2, prm["w_pa"], prm["d_skip"], prm["w_glu"], prm["b_glu"], prm["w_pb"],
        prm["b_gate"], prm["w_out"], prm["final_g"], layer=layer,
        za_block=c_qkv // wa, u_block=u_off // wb, zb_block=(u_off + c_u) // wb,
        gate_block0=(u_off + c_u + c_zb) // d, final_norm=final_norm)


def kernel(x, ln_g, w_in, conv_w, a_log, dt_bias, head_norm_g, lam_re, lam_im, log_dt, b_re, b_im, c_re, c_im, d_skip, w_glu, b_glu, w_pa, w_pb, b_gate, w_out, final_g):
    bsz, seq, d = x.shape
    depth = ln_g.shape[0]
    prm = _prepare_params(ln_g, w_in, conv_w, a_log, dt_bias, head_norm_g, lam_re, lam_im, log_dt, b_re, b_im,
                          c_re, c_im, d_skip, w_glu, b_glu, w_pa, w_pb, b_gate, w_out, final_g)
    x2 = x.reshape(bsz * seq, d)
    for layer in range(depth):
        x2 = _layer(x2, bsz, seq, prm, layer, final_norm=(layer == depth - 1))
    return x2.reshape(bsz, seq, d)
```

```python
import functools
import math

import jax
import jax.numpy as jnp
from jax import lax
from jax.experimental import pallas as pl
from jax.experimental.pallas import tpu as pltpu

F32 = jnp.float32
BF16 = jnp.bfloat16

LANE = 128
SUBLANE = 8
HEAD_DIM = 128
CHUNK = 64
CONV_K = 5
CONV_PAD = (CONV_K - 1) // 2
DELTA_HEADS_PER_PROGRAM = 2
LEVEL_GROUP_CHUNKS = 32
S5_GROUP_CH = 16
S5_STATE = 64
S5_BLOCK = 16
RMS_EPS = 1e-6
VMEM_LIMIT_BYTES = 56 * 1024 * 1024


def _compiler_params(semantics):
    return pltpu.CompilerParams(dimension_semantics=semantics, vmem_limit_bytes=VMEM_LIMIT_BYTES)


def _silu(x):
    return x * jax.nn.sigmoid(x)


def _largest_tile(n, cap, unit):
    best = unit
    t = unit
    while t <= min(n, cap):
        if n % t == 0:
            best = t
        t += unit
    return best


def _regroup_kernel(a_ref, b_ref, o_ref, *, n_plain, shift):
    @pl.when(pl.program_id(2) < n_plain)
    def _():
        o_ref[...] = a_ref[...]

    @pl.when(pl.program_id(2) >= n_plain)
    def _():
        both = jnp.concatenate([a_ref[...], b_ref[...]], axis=1)
        o_ref[...] = both[:, shift:shift + o_ref.shape[1]]


def _regroup_weights(w, o_skip, n_skip):
    depth, d, n = w.shape
    n_out = n - n_skip
    tn = _largest_tile(math.gcd(o_skip, n_out - o_skip), 1024, LANE)
    tr = _largest_tile(d, 2048, 2 * SUBLANE)
    return pl.pallas_call(
        functools.partial(_regroup_kernel, n_plain=o_skip // tn, shift=n_skip),
        grid=(depth, d // tr, n_out // tn),
        in_specs=[
            pl.BlockSpec((None, tr, tn), lambda l, r, j: (l, r, j)),
            pl.BlockSpec((None, tr, LANE), lambda l, r, j: (l, r, (j + 1) * (tn // LANE))),
        ],
        out_specs=pl.BlockSpec((None, tr, tn), lambda l, r, j: (l, r, j)),
        out_shape=jax.ShapeDtypeStruct((depth, d, n_out), w.dtype),
        compiler_params=_compiler_params(("parallel", "parallel", "parallel")),
        name="regroup",
    )(w, w)


def _inproj_kernel(x_ref, g_ref, w_ref, wba_ref, o_ref, ba_ref, h_ref):
    @pl.when(pl.program_id(1) == 0)
    def _():
        x = x_ref[...]
        ms = jnp.mean(x * x, axis=-1, keepdims=True)
        h_ref[...] = (x * lax.rsqrt(ms + RMS_EPS) * g_ref[...]).astype(BF16)
        ba_ref[...] = jnp.dot(h_ref[...], wba_ref[...], preferred_element_type=F32)

    o_ref[...] = jnp.dot(h_ref[...], w_ref[...], preferred_element_type=F32)


def _inproj(x2, g_rows, w_bf16, w_ba, layer):
    t, d = x2.shape
    n = w_bf16.shape[2]
    tm = _largest_tile(t, 1024, SUBLANE)
    tn = _largest_tile(n, 1280, LANE)
    return pl.pallas_call(
        _inproj_kernel,
        grid=(t // tm, n // tn),
        in_specs=[
            pl.BlockSpec((tm, d), lambda i, j: (i, 0)),
            pl.BlockSpec((None, 1, d), lambda i, j: (layer, 0, 0)),
            pl.BlockSpec((None, d, tn), lambda i, j: (layer, 0, j)),
            pl.BlockSpec((None, d, LANE), lambda i, j: (layer, 0, 0)),
        ],
        out_specs=[pl.BlockSpec((tm, tn), lambda i, j: (i, j)),
                   pl.BlockSpec((tm, LANE), lambda i, j: (i, 0))],
        out_shape=[jax.ShapeDtypeStruct((t, n), F32), jax.ShapeDtypeStruct((t, LANE), F32)],
        scratch_shapes=[pltpu.VMEM((tm, d), BF16)],
        compiler_params=_compiler_params(("parallel", "arbitrary")),
        name="inproj",
    )(x2, g_rows, w_bf16, w_ba)


def _gates_kernel(ba_ref, alog_ref, dtb_ref, o_ref, *, n_heads):
    lane = lax.broadcasted_iota(jnp.int32, (CHUNK, LANE), 1)
    r = lax.broadcasted_iota(jnp.int32, (CHUNK, CHUNK), 0)
    c = lax.broadcasted_iota(jnp.int32, (CHUNK, CHUNK), 1)
    m_prefix = jnp.where(c <= r, 1.0, 0.0).astype(F32)
    m_suffix = jnp.where(c >= r, 1.0, 0.0).astype(F32)
    within_chunk = functools.partial(jnp.dot, precision=lax.Precision.HIGHEST, preferred_element_type=F32)
    neg_rate = -jnp.exp(alog_ref[...])
    for r0 in range(0, ba_ref.shape[0], CHUNK):
        x = ba_ref[r0:r0 + CHUNK, :]
        z = x + dtb_ref[...]
        g = neg_rate * (jnp.maximum(z, 0.0) + jnp.log1p(jnp.exp(-jnp.abs(z))))
        gc = jnp.where(lane < 3 * n_heads, within_chunk(m_prefix, g), within_chunk(m_suffix, g))
        o_ref[r0:r0 + CHUNK, :] = jnp.where(lane < 2 * n_heads, jax.nn.sigmoid(x), gc)


def _gates(ba_logits, alog_rows, dtb_rows, layer, n_heads):
    t = ba_logits.shape[0]
    tm = _largest_tile(t, 512, CHUNK)
    return pl.pallas_call(
        functools.partial(_gates_kernel, n_heads=n_heads),
        grid=(t // tm,),
        in_specs=[
            pl.BlockSpec((tm, LANE), lambda i: (i, 0)),
            pl.BlockSpec((None, 1, LANE), lambda i: (layer, 0, 0)),
            pl.BlockSpec((None, 1, LANE), lambda i: (layer, 0, 0)),
        ],
        out_specs=pl.BlockSpec((tm, LANE), lambda i: (i, 0)),
        out_shape=jax.ShapeDtypeStruct((t, LANE), F32),
        compiler_params=_compiler_params(("parallel",)),
        name="gates",
    )(ba_logits, alog_rows, dtb_rows)


def _dot_nt(a, b):
    return lax.dot_general(a, b, (((1,), (1,)), ((), ())), preferred_element_type=F32)


def _delta_kernel(q_ref, k_ref, v_ref, wq_ref, wk_ref, wv_ref, cols_ref, rows_ref, hg_ref, o_ref,
                  pad_scr, r_scr, mp_scr, wq_scr, kdt_scr, qk_scr, od_scr, s_scr):
    seq = q_ref.shape[0]
    hpb = q_ref.shape[1] // HEAD_DIM
    n_chunks = seq // CHUNK
    halo = SUBLANE
    c2 = 2 * CHUNK
    operands = ((q_ref, wq_ref), (k_ref, wk_ref), (v_ref, wv_ref))
    for j in range(len(operands)):
        pad_scr[j, 0:halo, :] = jnp.zeros((halo, HEAD_DIM), F32)
        pad_scr[j, seq + halo:seq + 2 * halo, :] = jnp.zeros((halo, HEAD_DIM), F32)

    def conv_silu(j, w, r0):
        acc = None
        for i in range(CONV_K):
            term = pad_scr[j, pl.ds(r0 + halo + i - CONV_PAD, CHUNK), :] * w[i:i + 1, :]
            acc = term if acc is None else acc + term
        return _silu(acc)

    def l2norm(y):
        return y * lax.rsqrt(jnp.sum(y * y, axis=-1, keepdims=True) + RMS_EPS)

    top_rows = lax.broadcasted_iota(jnp.int32, (c2, c2), 0) < CHUNK
    rw = lax.broadcasted_iota(jnp.int32, (CHUNK, c2), 0)
    cw = lax.broadcasted_iota(jnp.int32, (CHUNK, c2), 1)
    fwd_lanes = cw < CHUNK
    cs = jnp.where(fwd_lanes, cw, cw - CHUNK)
    incl_wide = (fwd_lanes & (rw >= cs)) | (~fwd_lanes & (rw <= cs))
    strict_wide = (fwd_lanes & (rw > cs)) | (~fwd_lanes & (rw < cs))
    eye_wide = jnp.where(rw == cs, 1.0, 0.0).astype(F32)

    def block_diag(cat):
        return jnp.concatenate([jnp.where(fwd_lanes, cat, 0.0), jnp.where(fwd_lanes, 0.0, cat)], axis=0)

    for p in range(hpb):
        lanes = slice(p * HEAD_DIM, (p + 1) * HEAD_DIM)

        def fill(n, carry, lanes=lanes):
            r0 = pl.multiple_of(n * CHUNK, CHUNK)
            for j, (src_ref, _) in enumerate(operands):
                pad_scr[j, pl.ds(r0 + halo, CHUNK), :] = src_ref[pl.ds(r0, CHUNK), lanes]
            return carry

        lax.fori_loop(0, n_chunks, fill, 0, unroll=2)

        def setup(n, carry, p=p, lanes=lanes):
            r0 = pl.multiple_of(n * CHUNK, CHUNK)
            q = l2norm(conv_silu(0, wq_ref[:, lanes], r0)) * (HEAD_DIM ** -0.5)
            k = l2norm(conv_silu(1, wk_ref[:, lanes], r0))
            v = conv_silu(2, wv_ref[:, lanes], r0)
            k16 = k.astype(BF16)
            k2_16 = jnp.concatenate([k16, k16], axis=0)
            gram = _dot_nt(jnp.concatenate([q.astype(BF16), k16], axis=0), k2_16)
            qk_wide, kk_wide = gram[:CHUNK], gram[CHUNK:]
            cols = cols_ref[pl.ds(r0, CHUNK), :]
            across = lambda c: jnp.broadcast_to(cols[:, 4 * p + c:4 * p + c + 1], (CHUNK, c2))
            gc2 = jnp.concatenate([across(0), across(1)], axis=0)
            be2 = jnp.concatenate([across(2), across(3)], axis=0)
            gc_row2 = rows_ref[p, n]
            k2 = jnp.concatenate([k, k], axis=0)
            v2 = jnp.concatenate([v, v], axis=0)
            q2 = jnp.concatenate([q, q], axis=0)
            gc_wide = jnp.where(fwd_lanes, gc2[:CHUNK], gc2[CHUNK:])
            be_wide = jnp.where(fwd_lanes, be2[:CHUNK], be2[CHUNK:])
            decay_wide = jnp.exp(gc_wide - gc_row2)
            mp_scr[p, n, 0:CHUNK, :] = -(be_wide * kk_wide * jnp.where(strict_wide, decay_wide, 0.0))
            mp_scr[p, n, CHUNK:c2, :] = eye_wide
            eg2 = jnp.exp(gc2)
            r_scr[p, n] = jnp.concatenate([v2 * be2, k2 * (be2 * eg2)], axis=1)
            qg2 = (q2 * eg2).astype(BF16)
            wq_scr[p, 0, n, CHUNK:c2, :] = qg2[:CHUNK]
            wq_scr[p, 1, n, CHUNK:c2, :] = qg2[CHUNK:]
            g_last2 = jnp.where(top_rows, gc_row2[:, CHUNK - 1:CHUNK], gc_row2[:, CHUNK:CHUNK + 1])
            kdec2 = k2 * jnp.exp(g_last2 - gc2)
            kdt_scr[p, n] = kdec2.T.astype(BF16)
            qk_scr[p, n] = (qk_wide * jnp.where(incl_wide, decay_wide, 0.0)).astype(BF16)
            return carry

        lax.fori_loop(0, n_chunks, setup, 0, unroll=8)

    group = _largest_tile(n_chunks, LEVEL_GROUP_CHUNKS, 1)

    def group_tiles(it):
        return [(p, it * group + g) for g in range(group) for p in range(hpb)]

    def level(_, carry):
        def per_group(it, carry):
            tiles = group_tiles(it)
            outs = []
            for p, n in tiles:
                mp = mp_scr[p, n]
                outs.append(jnp.dot(mp.astype(BF16), block_diag(mp[:CHUNK]).astype(BF16),
                                    preferred_element_type=F32))
            for (p, n), out in zip(tiles, outs):
                mp_scr[p, n, 0:CHUNK, :] = out[:CHUNK]
                mp_scr[p, n, CHUNK:c2, :] += out[CHUNK:]
            return carry

        return lax.fori_loop(0, n_chunks // group, per_group, carry)

    lax.fori_loop(0, int(math.log2(CHUNK)), level, 0)

    def apply_inverse(it, carry):
        tiles = group_tiles(it)
        outs = [jnp.dot(block_diag(mp_scr[p, n, CHUNK:c2, :]).astype(BF16), r_scr[p, n].astype(BF16),
                        preferred_element_type=F32) for p, n in tiles]
        for (p, n), uw in zip(tiles, outs):
            r_scr[p, n] = uw
            w2 = uw[:, HEAD_DIM:].astype(BF16)
            wq_scr[p, 0, n, 0:CHUNK, :] = w2[:CHUNK]
            wq_scr[p, 1, n, 0:CHUNK, :] = w2[CHUNK:]
        return carry

    lax.fori_loop(0, n_chunks // group, apply_inverse, 0)

    s_scr[...] = jnp.zeros(s_scr.shape, F32)
    zero_half = jnp.zeros((CHUNK, HEAD_DIM), BF16)
    chains = [(p, d) for p in range(hpb) for d in range(2)]

    def recurrence(i, carry):
        chunk_of = (i, n_chunks - 1 - i)
        states = [s_scr[p, d] for p, d in chains]
        ws = [jnp.dot(wq_scr[p, d, chunk_of[d]], s.astype(BF16), preferred_element_type=F32)
              for (p, d), s in zip(chains, states)]
        new_states = []
        for (p, d), s, w in zip(chains, states, ws):
            n = chunk_of[d]
            r0 = pl.multiple_of(n * CHUNK, CHUNK)
            v_new = (r_scr[p, n, d * CHUNK:(d + 1) * CHUNK, 0:HEAD_DIM] - w[:CHUNK]).astype(BF16)
            v_pad = jnp.concatenate([v_new, zero_half] if d == 0 else [zero_half, v_new], axis=0)
            o = w[CHUNK:] + jnp.dot(qk_scr[p, n], v_pad, preferred_element_type=F32)
            od_scr[p, d, pl.ds(r0, CHUNK), :] = o
            gc_row2 = rows_ref[p, n]
            g_last = gc_row2[:, CHUNK - 1:CHUNK] if d == 0 else gc_row2[:, CHUNK:CHUNK + 1]
            new_states.append(s * jnp.exp(g_last) + jnp.dot(kdt_scr[p, n], v_pad, preferred_element_type=F32))
        for (p, d), s in zip(chains, new_states):
            s_scr[p, d] = s
        return carry

    lax.fori_loop(0, n_chunks, recurrence, 0)

    def head_norm(n, carry):
        r0 = pl.multiple_of(n * CHUNK, CHUNK)
        for p in range(hpb):
            lanes = slice(p * HEAD_DIM, (p + 1) * HEAD_DIM)
            o = od_scr[p, 0, pl.ds(r0, CHUNK), :] + od_scr[p, 1, pl.ds(r0, CHUNK), :]
            ms = jnp.mean(o * o, axis=-1, keepdims=True)
            o_ref[pl.ds(r0, CHUNK), lanes] = o * lax.rsqrt(ms + RMS_EPS) * hg_ref[...]
        return carry

    lax.fori_loop(0, n_chunks, head_norm, 0, unroll=4)


def _delta(proj3, conv_w, cols, rows, hg_rows, layer, n_heads):
    bsz, seq, _ = proj3.shape
    n_chunks = seq // CHUNK
    hpb = DELTA_HEADS_PER_PROGRAM
    width = hpb * HEAD_DIM
    n_blocks = n_heads // hpb
    tok = lambda off: pl.BlockSpec((None, seq, width), lambda b, h: (b, 0, off + h))
    cw = lambda off: pl.BlockSpec((None, CONV_K, width), lambda b, h: (layer, 0, off + h))
    return pl.pallas_call(
        _delta_kernel,
        grid=(bsz, n_blocks),
        in_specs=[
            tok(0), tok(n_blocks), tok(2 * n_blocks),
            cw(0), cw(n_blocks), cw(2 * n_blocks),
            pl.BlockSpec((None, None, seq, 4 * hpb), lambda b, h: (b, h, 0, 0)),
            pl.BlockSpec((None, hpb, n_chunks, 1, 2 * CHUNK), lambda b, h: (b, h, 0, 0, 0)),
            pl.BlockSpec((None, 1, HEAD_DIM), lambda b, h: (layer, 0, 0)),
        ],
        out_specs=pl.BlockSpec((None, seq, width), lambda b, h: (b, 0, h)),
        out_shape=jax.ShapeDtypeStruct((bsz, seq, n_heads * HEAD_DIM), F32),
        scratch_shapes=[
            pltpu.VMEM((3, seq + 2 * SUBLANE, HEAD_DIM), F32),
            pltpu.VMEM((hpb, n_chunks, 2 * CHUNK, 2 * HEAD_DIM), F32),
            pltpu.VMEM((hpb, n_chunks, 2 * CHUNK, 2 * CHUNK), F32),
            pltpu.VMEM((hpb, 2, n_chunks, 2 * CHUNK, HEAD_DIM), BF16),
            pltpu.VMEM((hpb, n_chunks, HEAD_DIM, 2 * CHUNK), BF16),
            pltpu.VMEM((hpb, n_chunks, CHUNK, 2 * CHUNK), BF16),
            pltpu.VMEM((hpb, 2, seq, HEAD_DIM), F32),
            pltpu.VMEM((hpb, 2, HEAD_DIM, HEAD_DIM), F32),
        ],
        compiler_params=_compiler_params(("parallel", "parallel")),
        name="delta",
    )(proj3, proj3, proj3, conv_w, conv_w, conv_w, cols, rows, hg_rows)


def _s5_kernel(u_ref, bt_ref, ct_ref, pw_ref, y_ref,
               ut_scr, s_scr, yt_scr, m_scr, wsum_scr, woutt_scr, *, bsz):
    sb, ch = S5_BLOCK, S5_GROUP_CH
    rows = u_ref.shape[0] // sb
    groups = LANE // ch
    w = 2 * S5_STATE
    blk = sb * ch

    def cmul(a_re, a_im, b_re, b_im):
        return a_re * b_re - a_im * b_im, a_re * b_im + a_im * b_re

    def lag_strip(d, q, h, lags):
        z = [cmul(ct_ref[d, 0, q, h], ct_ref[d, 1, q, h], pw_ref[d, 0, q][l:l + 1], pw_ref[d, 1, q][l:l + 1])
             for l in lags]
        z_re = jnp.concatenate([t[0] for t in z], axis=0)
        z_im = jnp.concatenate([t[1] for t in z], axis=0)
        over_states = functools.partial(lax.dot_general, dimension_numbers=(((1,), (1,)), ((), ())),
                                        precision=lax.Precision.HIGHEST, preferred_element_type=F32)
        return over_states(bt_ref[d, 0, q, h], z_re) - over_states(bt_ref[d, 1, q, h], z_im)

    lane = lax.broadcasted_iota(jnp.int32, (ch, blk), 1)
    for g in range(groups):
        k_f = lag_strip(0, g // 2, g % 2, range(sb))
        k_b = lag_strip(1, g // 2, g % 2, range(sb - 1, -1, -1))
        for j in range(sb):
            fwd = jnp.where(lane >= j * ch, pltpu.roll(k_f, j * ch, axis=1), 0.0)
            bwd = jnp.where(lane < (j + 1) * ch, pltpu.roll(k_b, (blk - (sb - 1 - j) * ch) % blk, axis=1), 0.0)
            m_scr[g, j * ch:(j + 1) * ch, :] = (fwd + bwd).astype(BF16)

    for j in range(sb):
        t = u_ref[pl.ds(j, rows, stride=sb), :].T.astype(BF16)
        for g in range(groups):
            ut_scr[g, j * ch:(j + 1) * ch, :] = t[g * ch:(g + 1) * ch, :]

    def dot_tn(a, b):
        return lax.dot_general(a, b, (((0,), (0,)), ((), ())), preferred_element_type=F32)

    n_blocks = rows // bsz
    pairs = groups // 2
    chains = pairs * bsz
    for q in range(pairs):
        pw = [[pw_ref[d, part, q] for part in range(2)] for d in range(2)]
        for h in range(2):
            bt = [[bt_ref[d, part, q, h] for part in range(2)] for d in range(2)]
            ct = [[ct_ref[d, part, q, h] for part in range(2)] for d in range(2)]
            for j in range(sb):
                r0 = h * blk + j * ch
                for d, (l_sum, l_out) in enumerate(((sb - 1 - j, j + 1), (j, sb - j))):
                    s_re, s_im = cmul(bt[d][0], bt[d][1], pw[d][0][l_sum:l_sum + 1], pw[d][1][l_sum:l_sum + 1])
                    z_re, z_im = cmul(ct[d][0], ct[d][1], pw[d][0][l_out:l_out + 1], pw[d][1][l_out:l_out + 1])
                    wsum_scr[q, r0:r0 + ch, 2 * d * w:(2 * d + 1) * w] = s_re.astype(BF16)
                    wsum_scr[q, r0:r0 + ch, (2 * d + 1) * w:(2 * d + 2) * w] = s_im.astype(BF16)
                    woutt_scr[q, r0:r0 + ch, 2 * d * w:(2 * d + 1) * w] = z_re.astype(BF16)
                    woutt_scr[q, r0:r0 + ch, (2 * d + 1) * w:(2 * d + 2) * w] = (-z_im).astype(BF16)

        ut2 = jnp.concatenate([ut_scr[2 * q], ut_scr[2 * q + 1]], axis=0)
        summaries = dot_tn(ut2, wsum_scr[q])
        for part in range(4):
            for b in range(bsz):
                s_scr[part, pl.ds(q * bsz + b, n_blocks, stride=chains), :] = (
                    summaries[b * n_blocks:(b + 1) * n_blocks, part * w:(part + 1) * w])

    decay = [jnp.concatenate([jnp.broadcast_to(pw_ref[d, part, q][sb:sb + 1], (bsz, w)) for q in range(pairs)], axis=0)
             for d in range(2) for part in range(2)]

    def step(k, xs):
        xf_re, xf_im, xb_re, xb_im = xs
        rf = pl.ds(pl.multiple_of(k * chains, chains), chains)
        rb = pl.ds(pl.multiple_of((n_blocks - 1 - k) * chains, chains), chains)
        sf_re, sf_im = s_scr[0, rf, :], s_scr[1, rf, :]
        sb_re, sb_im = s_scr[2, rb, :], s_scr[3, rb, :]
        s_scr[0, rf, :] = xf_re
        s_scr[1, rf, :] = xf_im
        s_scr[2, rb, :] = xb_re
        s_scr[3, rb, :] = xb_im
        nf_re, nf_im = cmul(decay[0], decay[1], xf_re, xf_im)
        nb_re, nb_im = cmul(decay[2], decay[3], xb_re, xb_im)
        return nf_re + sf_re, nf_im + sf_im, nb_re + sb_re, nb_im + sb_im

    zero = jnp.zeros((chains, w), F32)
    lax.fori_loop(0, n_blocks, step, (zero, zero, zero, zero), unroll=2)

    for q in range(pairs):
        entering = jnp.concatenate(
            [jnp.concatenate([s_scr[part, pl.ds(q * bsz + b, n_blocks, stride=chains), :] for b in range(bsz)], axis=0)
             for part in range(4)], axis=1)
        carried_t = _dot_nt(woutt_scr[q], entering.astype(BF16))
        for g in range(2):
            y_t = dot_tn(m_scr[2 * q + g], ut_scr[2 * q + g]) + carried_t[g * blk:(g + 1) * blk]
            for i in range(sb):
                yt_scr[i, (2 * q + g) * ch:(2 * q + g + 1) * ch, :] = y_t[i * ch:(i + 1) * ch, :]

    for i in range(sb):
        y_ref[pl.ds(i, rows, stride=sb), :] = yt_scr[i].T


def _s5(proj, u_block0, b_t, c_t, powers, layer, bsz):
    t = proj.shape[0]
    rows = t // S5_BLOCK
    blk = S5_BLOCK * S5_GROUP_CH
    groups = LANE // S5_GROUP_CH
    pairs = groups // 2
    w = 2 * S5_STATE
    n_tiles = powers.shape[3] // pairs
    placed = pl.BlockSpec((None, 2, 2, pairs, 2, S5_GROUP_CH, w), lambda k: (layer, 0, 0, k, 0, 0, 0))
    return pl.pallas_call(
        functools.partial(_s5_kernel, bsz=bsz),
        grid=(n_tiles,),
        in_specs=[
            pl.BlockSpec((t, LANE), lambda k: (0, u_block0 + k)),
            placed, placed,
            pl.BlockSpec((None, 2, 2, pairs, S5_BLOCK + 1, w), lambda k: (layer, 0, 0, k, 0, 0)),
        ],
        out_specs=pl.BlockSpec((t, LANE), lambda k: (0, k)),
        out_shape=jax.ShapeDtypeStruct((t, n_tiles * LANE), F32),
        scratch_shapes=[
            pltpu.VMEM((groups, blk, rows), BF16),
            pltpu.VMEM((4, pairs * rows, w), F32),
            pltpu.VMEM((S5_BLOCK, LANE, rows), F32),
            pltpu.VMEM((groups, blk, blk), BF16),
            pltpu.VMEM((pairs, 2 * blk, 4 * w), BF16),
            pltpu.VMEM((pairs, 2 * blk, 4 * w), BF16),
        ],
        compiler_params=_compiler_params(("parallel",)),
        name="s5",
    )(proj, b_t, c_t, powers)


def _s5_params(lam_re, lam_im, log_dt, b_re, b_im, c_re, c_im):
    n_groups = lam_re.shape[1]
    sb, p, c = S5_BLOCK, S5_STATE, S5_GROUP_CH
    l_re, l_im = lam_re.astype(F32), lam_im.astype(F32)
    dt = jnp.exp(log_dt.astype(F32))[..., None]
    mag = jnp.exp(l_re * dt)
    bar_re, bar_im = mag * jnp.cos(l_im * dt), mag * jnp.sin(l_im * dt)
    n_re, n_im = bar_re - 1.0, bar_im
    den = l_re * l_re + l_im * l_im
    k_re = ((n_re * l_re + n_im * l_im) / den)[..., None]
    k_im = ((n_im * l_re - n_re * l_im) / den)[..., None]
    bb_re = k_re * b_re.astype(F32) - k_im * b_im.astype(F32)
    bb_im = k_re * b_im.astype(F32) + k_im * b_re.astype(F32)
    cc_re, cc_im = c_re.astype(F32), c_im.astype(F32)

    steps = jnp.arange(sb + 1, dtype=F32)[:, None, None, None]
    mag_l = jnp.exp(steps * (l_re * dt)[None])
    pw_re, pw_im = mag_l * jnp.cos(steps * (l_im * dt)[None]), mag_l * jnp.sin(steps * (l_im * dt)[None])

    same = jnp.eye(2, dtype=F32)
    def place(t):
        t2 = t.reshape(2, n_groups // 2, 2, c, 1, p) * same[None, None, :, None, :, None]
        return t2.reshape(2, n_groups // 2, 2, c, 2 * p)
    b_t = jnp.stack([place(jnp.swapaxes(bb_re, 2, 3)), place(jnp.swapaxes(bb_im, 2, 3))], axis=1)
    c_t = jnp.stack([place(cc_re), place(cc_im)], axis=1)
    def side_by_side(t):
        t2 = jnp.transpose(t.reshape(sb + 1, 2, n_groups // 2, 2, p), (1, 2, 0, 3, 4))
        return t2.reshape(2, n_groups // 2, sb + 1, 2 * p)
    powers = jnp.stack([side_by_side(pw_re), side_by_side(pw_im)], axis=1)
    return b_t, c_t, powers


def _out_kernel(oa_ref, za_ref, ys_ref, u_ref, zb_ref, ga_ref, gb_ref, x_ref,
                wpa_ref, dsk_ref, wglu_ref, bglu_ref, wpb_ref, bga_ref, bgb_ref, wout_ref, fg_ref,
                o_ref, *, final_norm):
    a = (oa_ref[...] * _silu(za_ref[...])).astype(BF16)
    y_a = jnp.dot(a, wpa_ref[...], preferred_element_type=F32)
    y_s = jax.nn.gelu(ys_ref[...] + u_ref[...] * dsk_ref[...])
    glu = jnp.dot(y_s.astype(BF16), wglu_ref[...], preferred_element_type=F32) + bglu_ref[...]
    y_s = y_s * jax.nn.sigmoid(glu)
    y_b = jnp.dot((y_s * _silu(zb_ref[...])).astype(BF16), wpb_ref[...], preferred_element_type=F32)
    merged = (jax.nn.sigmoid(ga_ref[...] + bga_ref[...]) * y_a
              + jax.nn.sigmoid(gb_ref[...] + bgb_ref[...]) * y_b)
    out = x_ref[...] + jnp.dot(merged.astype(BF16), wout_ref[...], preferred_element_type=F32)
    if final_norm:
        ms = jnp.mean(out * out, axis=-1, keepdims=True)
        out = out * lax.rsqrt(ms + RMS_EPS) * fg_ref[...]
    o_ref[...] = out


def _out_stage(o_a, proj, y_s5, x2, w_pa, d_skip, w_glu, b_glu, w_pb, b_gate, w_out, final_g, *,
               layer, za_block, u_block, zb_block, gate_block0, final_norm):
    t, d = x2.shape
    wa = o_a.shape[1]
    wb = y_s5.shape[1]
    tm = _largest_tile(t, 256, SUBLANE)
    row = lambda width, blk: pl.BlockSpec((tm, width), lambda i: (i, blk))
    const = lambda shape, blk=0: pl.BlockSpec((None,) + shape, lambda i: (layer, 0, blk),
                                              pipeline_mode=pl.Buffered(1))
    return pl.pallas_call(
        functools.partial(_out_kernel, final_norm=final_norm),
        grid=(t // tm,),
        in_specs=[
            row(wa, 0), row(wa, za_block), row(wb, 0), row(wb, u_block), row(wb, zb_block),
            row(d, gate_block0), row(d, gate_block0 + 1), row(d, 0),
            const((wa, d)), const((1, wb)), const((wb, wb)), const((1, wb)), const((wb, d)),
            const((1, d), 0), const((1, d), 1), const((d, d)), const((1, d)),
        ],
        out_specs=pl.BlockSpec((tm, d), lambda i: (i, 0)),
        out_shape=jax.ShapeDtypeStruct((t, d), F32),
        compiler_params=_compiler_params(("parallel",)),
        name="out_stage",
    )(o_a, proj, y_s5, proj, proj, proj, proj, x2, w_pa, d_skip, w_glu, b_glu, w_pb, b_gate, b_gate, w_out,
      final_g)


def _prepare_params(ln_g, w_in, conv_w, a_log, dt_bias, head_norm_g, lam_re, lam_im, log_dt, b_re, b_im,
                    c_re, c_im, d_skip, w_glu, b_glu, w_pa, w_pb, b_gate, w_out, final_g):
    depth, d, _ = w_in.shape
    n_heads = a_log.shape[2]
    wa = n_heads * HEAD_DIM
    wb = d_skip.shape[1]
    n_ba = 4 * n_heads
    o_beta = 4 * wa
    w_in16 = w_in.astype(BF16)
    w_cols = _regroup_weights(w_in16, o_beta, n_ba)
    w_ba = w_in16[:, :, o_beta:o_beta + LANE]
    pad_rows = lambda p: jnp.pad(p.astype(F32).reshape(depth, 1, n_ba // 2),
                                 ((0, 0), (0, 0), (n_ba // 2, LANE - n_ba)))
    row = lambda p: p.astype(F32).reshape(depth, 1, -1)
    return dict(
        ln_g=row(ln_g), w_cols=w_cols, w_ba=w_ba, conv_w=conv_w.astype(F32), alog=pad_rows(a_log), dtb=pad_rows(dt_bias),
        head_norm_g=row(head_norm_g),
        s5=jax.vmap(_s5_params)(lam_re, lam_im, log_dt, b_re, b_im, c_re, c_im),
        d_skip=row(d_skip), w_glu=w_glu.astype(BF16), b_glu=row(b_glu), w_pa=w_pa.astype(BF16),
        w_pb=w_pb.astype(BF16), b_gate=row(b_gate), w_out=w_out.astype(BF16),
        final_g=jnp.broadcast_to(final_g.astype(F32).reshape(1, 1, d), (depth, 1, d)),
        n_heads=n_heads, wa=wa, wb=wb)


def _layer(x2, bsz, seq, prm, layer, final_norm):
    t, d = x2.shape
    n_heads, wa, wb = prm["n_heads"], prm["wa"], prm["wb"]
    assert wa == wb and seq % (CHUNK * SUBLANE) == 0 and 4 * n_heads <= LANE
    c_qkv, c_za, c_beta, c_alpha, c_u, c_zb = 3 * wa, wa, 2 * n_heads, 2 * n_heads, wb, wb
    n_proj = prm["w_cols"].shape[2]

    proj, ba_logits = _inproj(x2, prm["ln_g"], prm["w_cols"], prm["w_ba"], layer)
    proj3 = proj.reshape(bsz, seq, n_proj)

    gates = _gates(ba_logits, prm["alog"], prm["dtb"], layer, n_heads)
    beta = gates[:, :c_beta].reshape(bsz, seq, 2, n_heads)
    gc = gates[:, c_beta:c_beta + c_alpha].reshape(bsz, seq, 2, n_heads)
    hpb = DELTA_HEADS_PER_PROGRAM
    cols = jnp.stack([gc[:, :, 0], gc[:, :, 1], beta[:, :, 0], beta[:, :, 1]], axis=-1)
    cols = jnp.transpose(cols.reshape(bsz, seq, n_heads // hpb, 4 * hpb), (0, 2, 1, 3))
    rows = jnp.transpose(gc.reshape(bsz, seq // CHUNK, CHUNK, 2, n_heads), (0, 4, 1, 3, 2))
    rows = rows.reshape(bsz, n_heads, seq // CHUNK, 1, 2 * CHUNK)

    o_a = _delta(proj3, prm["conv_w"], cols, rows, prm["head_norm_g"], layer, n_heads)
    o_a = o_a.reshape(t, wa)

    u_off = c_qkv + c_za
    y_s5 = _s5(proj, u_off // LANE, *prm["s5"], layer, bsz)

    return _out_stage(
        o_a, proj, y_s5, x2, prm["w_pa"], prm["d_skip"], prm["w_glu"], prm["b_glu"], prm["w_pb"],
        prm["b_gate"], prm["w_out"], prm["final_g"], layer=layer,
        za_block=c_qkv // wa, u_block=u_off // wb, zb_block=(u_off + c_u) // wb,
        gate_block0=(u_off + c_u + c_zb) // d, final_norm=final_norm)


def kernel(x, ln_g, w_in, conv_w, a_log, dt_bias, head_norm_g, lam_re, lam_im, log_dt, b_re, b_im, c_re, c_im, d_skip, w_glu, b_glu, w_pa, w_pb, b_gate, w_out, final_g):
    bsz, seq, d = x.shape
    depth = ln_g.shape[0]
    prm = _prepare_params(ln_g, w_in, conv_w, a_log, dt_bias, head_norm_g, lam_re, lam_im, log_dt, b_re, b_im,
                          c_re, c_im, d_skip, w_glu, b_glu, w_pa, w_pb, b_gate, w_out, final_g)
    x2 = x.reshape(bsz * seq, d)
    for layer in range(depth):
        x2 = _layer(x2, bsz, seq, prm, layer, final_norm=(layer == depth - 1))
    return x2.reshape(bsz, seq, d)
```

```python
import functools
import math

import jax
import jax.numpy as jnp
from jax import lax
from jax.experimental import pallas as pl
from jax.experimental.pallas import tpu as pltpu

F32 = jnp.float32
BF16 = jnp.bfloat16

LANE = 128
SUBLANE = 8
HEAD_DIM = 128
CHUNK = 64
CONV_K = 5
CONV_PAD = (CONV_K - 1) // 2
DELTA_HEADS_PER_PROGRAM = 2
LEVEL_GROUP_CHUNKS = 32
S5_GROUP_CH = 16
S5_STATE = 64
S5_BLOCK = 16
RMS_EPS = 1e-6
VMEM_LIMIT_BYTES = 56 * 1024 * 1024


def _compiler_params(semantics):
    return pltpu.CompilerParams(dimension_semantics=semantics, vmem_limit_bytes=VMEM_LIMIT_BYTES)


def _silu(x):
    return x * jax.nn.sigmoid(x)


def _largest_tile(n, cap, unit):
    best = unit
    t = unit
    while t <= min(n, cap):
        if n % t == 0:
            best = t
        t += unit
    return best


def _regroup_kernel(a_ref, b_ref, o_ref, *, n_plain, shift):
    @pl.when(pl.program_id(2) < n_plain)
    def _():
        o_ref[...] = a_ref[...]

    @pl.when(pl.program_id(2) >= n_plain)
    def _():
        both = jnp.concatenate([a_ref[...], b_ref[...]], axis=1)
        o_ref[...] = both[:, shift:shift + o_ref.shape[1]]


def _regroup_weights(w, o_skip, n_skip):
    depth, d, n = w.shape
    n_out = n - n_skip
    tn = _largest_tile(math.gcd(o_skip, n_out - o_skip), 1024, LANE)
    tr = _largest_tile(d, 2048, 2 * SUBLANE)
    return pl.pallas_call(
        functools.partial(_regroup_kernel, n_plain=o_skip // tn, shift=n_skip),
        grid=(depth, d // tr, n_out // tn),
        in_specs=[
            pl.BlockSpec((None, tr, tn), lambda l, r, j: (l, r, j)),
            pl.BlockSpec((None, tr, LANE), lambda l, r, j: (l, r, (j + 1) * (tn // LANE))),
        ],
        out_specs=pl.BlockSpec((None, tr, tn), lambda l, r, j: (l, r, j)),
        out_shape=jax.ShapeDtypeStruct((depth, d, n_out), w.dtype),
        compiler_params=_compiler_params(("parallel", "parallel", "parallel")),
        name="regroup",
    )(w, w)


def _inproj_kernel(x_ref, g_ref, w_ref, wba_ref, o_ref, ba_ref, h_ref):
    @pl.when(pl.program_id(1) == 0)
    def _():
        x = x_ref[...]
        ms = jnp.mean(x * x, axis=-1, keepdims=True)
        h_ref[...] = (x * lax.rsqrt(ms + RMS_EPS) * g_ref[...]).astype(BF16)
        ba_ref[...] = jnp.dot(h_ref[...], wba_ref[...], preferred_element_type=F32)

    o_ref[...] = jnp.dot(h_ref[...], w_ref[...], preferred_element_type=F32)


def _inproj(x2, g_rows, w_bf16, w_ba, layer):
    t, d = x2.shape
    n = w_bf16.shape[2]
    tm = _largest_tile(t, 1024, SUBLANE)
    tn = _largest_tile(n, 1280, LANE)
    return pl.pallas_call(
        _inproj_kernel,
        grid=(t // tm, n // tn),
        in_specs=[
            pl.BlockSpec((tm, d), lambda i, j: (i, 0)),
            pl.BlockSpec((None, 1, d), lambda i, j: (layer, 0, 0)),
            pl.BlockSpec((None, d, tn), lambda i, j: (layer, 0, j)),
            pl.BlockSpec((None, d, LANE), lambda i, j: (layer, 0, 0)),
        ],
        out_specs=[pl.BlockSpec((tm, tn), lambda i, j: (i, j)),
                   pl.BlockSpec((tm, LANE), lambda i, j: (i, 0))],
        out_shape=[jax.ShapeDtypeStruct((t, n), F32), jax.ShapeDtypeStruct((t, LANE), F32)],
        scratch_shapes=[pltpu.VMEM((tm, d), BF16)],
        compiler_params=_compiler_params(("parallel", "arbitrary")),
        name="inproj",
    )(x2, g_rows, w_bf16, w_ba)


def _gates_kernel(ba_ref, alog_ref, dtb_ref, o_ref, *, n_heads):
    lane = lax.broadcasted_iota(jnp.int32, (CHUNK, LANE), 1)
    r = lax.broadcasted_iota(jnp.int32, (CHUNK, CHUNK), 0)
    c = lax.broadcasted_iota(jnp.int32, (CHUNK, CHUNK), 1)
    m_prefix = jnp.where(c <= r, 1.0, 0.0).astype(F32)
    m_suffix = jnp.where(c >= r, 1.0, 0.0).astype(F32)
    within_chunk = functools.partial(jnp.dot, precision=lax.Precision.HIGHEST, preferred_element_type=F32)
    neg_rate = -jnp.exp(alog_ref[...])
    for r0 in range(0, ba_ref.shape[0], CHUNK):
        x = ba_ref[r0:r0 + CHUNK, :]
        z = x + dtb_ref[...]
        g = neg_rate * (jnp.maximum(z, 0.0) + jnp.log1p(jnp.exp(-jnp.abs(z))))
        gc = jnp.where(lane < 3 * n_heads, within_chunk(m_prefix, g), within_chunk(m_suffix, g))
        o_ref[r0:r0 + CHUNK, :] = jnp.where(lane < 2 * n_heads, jax.nn.sigmoid(x), gc)


def _gates(ba_logits, alog_rows, dtb_rows, layer, n_heads):
    t = ba_logits.shape[0]
    tm = _largest_tile(t, 512, CHUNK)
    return pl.pallas_call(
        functools.partial(_gates_kernel, n_heads=n_heads),
        grid=(t // tm,),
        in_specs=[
            pl.BlockSpec((tm, LANE), lambda i: (i, 0)),
            pl.BlockSpec((None, 1, LANE), lambda i: (layer, 0, 0)),
            pl.BlockSpec((None, 1, LANE), lambda i: (layer, 0, 0)),
        ],
        out_specs=pl.BlockSpec((tm, LANE), lambda i: (i, 0)),
        out_shape=jax.ShapeDtypeStruct((t, LANE), F32),
        compiler_params=_compiler_params(("parallel",)),
        name="gates",
    )(ba_logits, alog_rows, dtb_rows)


def _dot_nt(a, b):
    return lax.dot_general(a, b, (((1,), (1,)), ((), ())), preferred_element_type=F32)


def _delta_kernel(q_ref, k_ref, v_ref, wq_ref, wk_ref, wv_ref, cols_ref, rows_ref, hg_ref, o_ref,
                  pad_scr, r_scr, mp_scr, wq_scr, kdt_scr, qk_scr, od_scr, s_scr):
    seq = q_ref.shape[0]
    hpb = q_ref.shape[1] // HEAD_DIM
    n_chunks = seq // CHUNK
    halo = SUBLANE
    c2 = 2 * CHUNK
    operands = ((q_ref, wq_ref), (k_ref, wk_ref), (v_ref, wv_ref))
    for j in range(len(operands)):
        pad_scr[j, 0:halo, :] = jnp.zeros((halo, HEAD_DIM), F32)
        pad_scr[j, seq + halo:seq + 2 * halo, :] = jnp.zeros((halo, HEAD_DIM), F32)

    def conv_silu(j, w, r0):
        acc = None
        for i in range(CONV_K):
            term = pad_scr[j, pl.ds(r0 + halo + i - CONV_PAD, CHUNK), :] * w[i:i + 1, :]
            acc = term if acc is None else acc + term
        return _silu(acc)

    def l2norm(y):
        return y * lax.rsqrt(jnp.sum(y * y, axis=-1, keepdims=True) + RMS_EPS)

    top_rows = lax.broadcasted_iota(jnp.int32, (c2, c2), 0) < CHUNK
    rw = lax.broadcasted_iota(jnp.int32, (CHUNK, c2), 0)
    cw = lax.broadcasted_iota(jnp.int32, (CHUNK, c2), 1)
    fwd_lanes = cw < CHUNK
    cs = jnp.where(fwd_lanes, cw, cw - CHUNK)
    incl_wide = (fwd_lanes & (rw >= cs)) | (~fwd_lanes & (rw <= cs))
    strict_wide = (fwd_lanes & (rw > cs)) | (~fwd_lanes & (rw < cs))
    eye_wide = jnp.where(rw == cs, 1.0, 0.0).astype(F32)

    def block_diag(cat):
        return jnp.concatenate([jnp.where(fwd_lanes, cat, 0.0), jnp.where(fwd_lanes, 0.0, cat)], axis=0)

    for p in range(hpb):
        lanes = slice(p * HEAD_DIM, (p + 1) * HEAD_DIM)

        def fill(n, carry, lanes=lanes):
            r0 = pl.multiple_of(n * CHUNK, CHUNK)
            for j, (src_ref, _) in enumerate(operands):
                pad_scr[j, pl.ds(r0 + halo, CHUNK), :] = src_ref[pl.ds(r0, CHUNK), lanes]
            return carry

        lax.fori_loop(0, n_chunks, fill, 0, unroll=2)

        def setup(n, carry, p=p, lanes=lanes):
            r0 = pl.multiple_of(n * CHUNK, CHUNK)
            q = l2norm(conv_silu(0, wq_ref[:, lanes], r0)) * (HEAD_DIM ** -0.5)
            k = l2norm(conv_silu(1, wk_ref[:, lanes], r0))
            v = conv_silu(2, wv_ref[:, lanes], r0)
            k16 = k.astype(BF16)
            k2_16 = jnp.concatenate([k16, k16], axis=0)
            gram = _dot_nt(jnp.concatenate([q.astype(BF16), k16], axis=0), k2_16)
            qk_wide, kk_wide = gram[:CHUNK], gram[CHUNK:]
            cols = cols_ref[pl.ds(r0, CHUNK), :]
            across = lambda c: jnp.broadcast_to(cols[:, 4 * p + c:4 * p + c + 1], (CHUNK, c2))
            gc2 = jnp.concatenate([across(0), across(1)], axis=0)
            be2 = jnp.concatenate([across(2), across(3)], axis=0)
            gc_row2 = rows_ref[p, n]
            k2 = jnp.concatenate([k, k], axis=0)
            v2 = jnp.concatenate([v, v], axis=0)
            q2 = jnp.concatenate([q, q], axis=0)
            gc_wide = jnp.where(fwd_lanes, gc2[:CHUNK], gc2[CHUNK:])
            be_wide = jnp.where(fwd_lanes, be2[:CHUNK], be2[CHUNK:])
            decay_wide = jnp.exp(gc_wide - gc_row2)
            mp_scr[p, n, 0:CHUNK, :] = -(be_wide * kk_wide * jnp.where(strict_wide, decay_wide, 0.0))
            mp_scr[p, n, CHUNK:c2, :] = eye_wide
            eg2 = jnp.exp(gc2)
            r_scr[p, n] = jnp.concatenate([v2 * be2, k2 * (be2 * eg2)], axis=1)
            qg2 = (q2 * eg2).astype(BF16)
            wq_scr[p, 0, n, CHUNK:c2, :] = qg2[:CHUNK]
            wq_scr[p, 1, n, CHUNK:c2, :] = qg2[CHUNK:]
            g_last2 = jnp.where(top_rows, gc_row2[:, CHUNK - 1:CHUNK], gc_row2[:, CHUNK:CHUNK + 1])
            kdec2 = k2 * jnp.exp(g_last2 - gc2)
            kdt_scr[p, n] = kdec2.T.astype(BF16)
            qk_scr[p, n] = (qk_wide * jnp.where(incl_wide, decay_wide, 0.0)).astype(BF16)
            return carry

        lax.fori_loop(0, n_chunks, setup, 0, unroll=8)

    group = _largest_tile(n_chunks, LEVEL_GROUP_CHUNKS, 1)

    def group_tiles(it):
        return [(p, it * group + g) for g in range(group) for p in range(hpb)]

    def level(_, carry):
        def per_group(it, carry):
            tiles = group_tiles(it)
            outs = []
            for p, n in tiles:
                mp = mp_scr[p, n]
                outs.append(jnp.dot(mp.astype(BF16), block_diag(mp[:CHUNK]).astype(BF16),
                                    preferred_element_type=F32))
            for (p, n), out in zip(tiles, outs):
                mp_scr[p, n, 0:CHUNK, :] = out[:CHUNK]
                mp_scr[p, n, CHUNK:c2, :] += out[CHUNK:]
            return carry

        return lax.fori_loop(0, n_chunks // group, per_group, carry)

    lax.fori_loop(0, int(math.log2(CHUNK)), level, 0)

    def apply_inverse(it, carry):
        tiles = group_tiles(it)
        outs = [jnp.dot(block_diag(mp_scr[p, n, CHUNK:c2, :]).astype(BF16), r_scr[p, n].astype(BF16),
                        preferred_element_type=F32) for p, n in tiles]
        for (p, n), uw in zip(tiles, outs):
            r_scr[p, n] = uw
            w2 = uw[:, HEAD_DIM:].astype(BF16)
            wq_scr[p, 0, n, 0:CHUNK, :] = w2[:CHUNK]
            wq_scr[p, 1, n, 0:CHUNK, :] = w2[CHUNK:]
        return carry

    lax.fori_loop(0, n_chunks // group, apply_inverse, 0)

    s_scr[...] = jnp.zeros(s_scr.shape, F32)
    zero_half = jnp.zeros((CHUNK, HEAD_DIM), BF16)
    chains = [(p, d) for p in range(hpb) for d in range(2)]

    def recurrence(i, carry):
        chunk_of = (i, n_chunks - 1 - i)
        states = [s_scr[p, d] for p, d in chains]
        ws = [jnp.dot(wq_scr[p, d, chunk_of[d]], s.astype(BF16), preferred_element_type=F32)
              for (p, d), s in zip(chains, states)]
        new_states = []
        for (p, d), s, w in zip(chains, states, ws):
            n = chunk_of[d]
            r0 = pl.multiple_of(n * CHUNK, CHUNK)
            v_new = (r_scr[p, n, d * CHUNK:(d + 1) * CHUNK, 0:HEAD_DIM] - w[:CHUNK]).astype(BF16)
            v_pad = jnp.concatenate([v_new, zero_half] if d == 0 else [zero_half, v_new], axis=0)
            o = w[CHUNK:] + jnp.dot(qk_scr[p, n], v_pad, preferred_element_type=F32)
            od_scr[p, d, pl.ds(r0, CHUNK), :] = o
            gc_row2 = rows_ref[p, n]
            g_last = gc_row2[:, CHUNK - 1:CHUNK] if d == 0 else gc_row2[:, CHUNK:CHUNK + 1]
            new_states.append(s * jnp.exp(g_last) + jnp.dot(kdt_scr[p, n], v_pad, preferred_element_type=F32))
        for (p, d), s in zip(chains, new_states):
            s_scr[p, d] = s
        return carry

    lax.fori_loop(0, n_chunks, recurrence, 0, unroll=4)

    def head_norm(n, carry):
        r0 = pl.multiple_of(n * CHUNK, CHUNK)
        for p in range(hpb):
            lanes = slice(p * HEAD_DIM, (p + 1) * HEAD_DIM)
            o = od_scr[p, 0, pl.ds(r0, CHUNK), :] + od_scr[p, 1, pl.ds(r0, CHUNK), :]
            ms = jnp.mean(o * o, axis=-1, keepdims=True)
            o_ref[pl.ds(r0, CHUNK), lanes] = o * lax.rsqrt(ms + RMS_EPS) * hg_ref[...]
        return carry

    lax.fori_loop(0, n_chunks, head_norm, 0, unroll=4)


def _delta(proj3, conv_w, cols, rows, hg_rows, layer, n_heads):
    bsz, seq, _ = proj3.shape
    n_chunks = seq // CHUNK
    hpb = DELTA_HEADS_PER_PROGRAM
    width = hpb * HEAD_DIM
    n_blocks = n_heads // hpb
    tok = lambda off: pl.BlockSpec((None, seq, width), lambda b, h: (b, 0, off + h))
    cw = lambda off: pl.BlockSpec((None, CONV_K, width), lambda b, h: (layer, 0, off + h))
    return pl.pallas_call(
        _delta_kernel,
        grid=(bsz, n_blocks),
        in_specs=[
            tok(0), tok(n_blocks), tok(2 * n_blocks),
            cw(0), cw(n_blocks), cw(2 * n_blocks),
            pl.BlockSpec((None, None, seq, 4 * hpb), lambda b, h: (b, h, 0, 0)),
            pl.BlockSpec((None, hpb, n_chunks, 1, 2 * CHUNK), lambda b, h: (b, h, 0, 0, 0)),
            pl.BlockSpec((None, 1, HEAD_DIM), lambda b, h: (layer, 0, 0)),
        ],
        out_specs=pl.BlockSpec((None, seq, width), lambda b, h: (b, 0, h)),
        out_shape=jax.ShapeDtypeStruct((bsz, seq, n_heads * HEAD_DIM), F32),
        scratch_shapes=[
            pltpu.VMEM((3, seq + 2 * SUBLANE, HEAD_DIM), F32),
            pltpu.VMEM((hpb, n_chunks, 2 * CHUNK, 2 * HEAD_DIM), F32),
            pltpu.VMEM((hpb, n_chunks, 2 * CHUNK, 2 * CHUNK), F32),
            pltpu.VMEM((hpb, 2, n_chunks, 2 * CHUNK, HEAD_DIM), BF16),
            pltpu.VMEM((hpb, n_chunks, HEAD_DIM, 2 * CHUNK), BF16),
            pltpu.VMEM((hpb, n_chunks, CHUNK, 2 * CHUNK), BF16),
            pltpu.VMEM((hpb, 2, seq, HEAD_DIM), F32),
            pltpu.VMEM((hpb, 2, HEAD_DIM, HEAD_DIM), F32),
        ],
        compiler_params=_compiler_params(("parallel", "parallel")),
        name="delta",
    )(proj3, proj3, proj3, conv_w, conv_w, conv_w, cols, rows, hg_rows)


def _s5_kernel(u_ref, bt_ref, ct_ref, pw_ref, y_ref,
               ut_scr, s_scr, yt_scr, m_scr, wsum_scr, woutt_scr, *, bsz):
    sb, ch = S5_BLOCK, S5_GROUP_CH
    rows = u_ref.shape[0] // sb
    groups = LANE // ch
    w = 2 * S5_STATE
    blk = sb * ch

    def cmul(a_re, a_im, b_re, b_im):
        return a_re * b_re - a_im * b_im, a_re * b_im + a_im * b_re

    def lag_strip(d, q, h, lags):
        z = [cmul(ct_ref[d, 0, q, h], ct_ref[d, 1, q, h], pw_ref[d, 0, q][l:l + 1], pw_ref[d, 1, q][l:l + 1])
             for l in lags]
        z_re = jnp.concatenate([t[0] for t in z], axis=0)
        z_im = jnp.concatenate([t[1] for t in z], axis=0)
        over_states = functools.partial(lax.dot_general, dimension_numbers=(((1,), (1,)), ((), ())),
                                        precision=lax.Precision.HIGHEST, preferred_element_type=F32)
        return over_states(bt_ref[d, 0, q, h], z_re) - over_states(bt_ref[d, 1, q, h], z_im)

    lane = lax.broadcasted_iota(jnp.int32, (ch, blk), 1)
    for g in range(groups):
        k_f = lag_strip(0, g // 2, g % 2, range(sb))
        k_b = lag_strip(1, g // 2, g % 2, range(sb - 1, -1, -1))
        for j in range(sb):
            fwd = jnp.where(lane >= j * ch, pltpu.roll(k_f, j * ch, axis=1), 0.0)
            bwd = jnp.where(lane < (j + 1) * ch, pltpu.roll(k_b, (blk - (sb - 1 - j) * ch) % blk, axis=1), 0.0)
            m_scr[g, j * ch:(j + 1) * ch, :] = (fwd + bwd).astype(BF16)

    for j in range(sb):
        t = u_ref[pl.ds(j, rows, stride=sb), :].T.astype(BF16)
        for g in range(groups):
            ut_scr[g, j * ch:(j + 1) * ch, :] = t[g * ch:(g + 1) * ch, :]

    def dot_tn(a, b):
        return lax.dot_general(a, b, (((0,), (0,)), ((), ())), preferred_element_type=F32)

    n_blocks = rows // bsz
    pairs = groups // 2
    chains = pairs * bsz
    for q in range(pairs):
        pw = [[pw_ref[d, part, q] for part in range(2)] for d in range(2)]
        for h in range(2):
            bt = [[bt_ref[d, part, q, h] for part in range(2)] for d in range(2)]
            ct = [[ct_ref[d, part, q, h] for part in range(2)] for d in range(2)]
            for j in range(sb):
                r0 = h * blk + j * ch
                for d, (l_sum, l_out) in enumerate(((sb - 1 - j, j + 1), (j, sb - j))):
                    s_re, s_im = cmul(bt[d][0], bt[d][1], pw[d][0][l_sum:l_sum + 1], pw[d][1][l_sum:l_sum + 1])
                    z_re, z_im = cmul(ct[d][0], ct[d][1], pw[d][0][l_out:l_out + 1], pw[d][1][l_out:l_out + 1])
                    wsum_scr[q, r0:r0 + ch, 2 * d * w:(2 * d + 1) * w] = s_re.astype(BF16)
                    wsum_scr[q, r0:r0 + ch, (2 * d + 1) * w:(2 * d + 2) * w] = s_im.astype(BF16)
                    woutt_scr[q, r0:r0 + ch, 2 * d * w:(2 * d + 1) * w] = z_re.astype(BF16)
                    woutt_scr[q, r0:r0 + ch, (2 * d + 1) * w:(2 * d + 2) * w] = (-z_im).astype(BF16)

        ut2 = jnp.concatenate([ut_scr[2 * q], ut_scr[2 * q + 1]], axis=0)
        summaries = dot_tn(ut2, wsum_scr[q])
        for part in range(4):
            for b in range(bsz):
                s_scr[part, pl.ds(q * bsz + b, n_blocks, stride=chains), :] = (
                    summaries[b * n_blocks:(b + 1) * n_blocks, part * w:(part + 1) * w])

    decay = [jnp.concatenate([jnp.broadcast_to(pw_ref[d, part, q][sb:sb + 1], (bsz, w)) for q in range(pairs)], axis=0)
             for d in range(2) for part in range(2)]

    def step(k, xs):
        xf_re, xf_im, xb_re, xb_im = xs
        rf = pl.ds(pl.multiple_of(k * chains, chains), chains)
        rb = pl.ds(pl.multiple_of((n_blocks - 1 - k) * chains, chains), chains)
        sf_re, sf_im = s_scr[0, rf, :], s_scr[1, rf, :]
        sb_re, sb_im = s_scr[2, rb, :], s_scr[3, rb, :]
        s_scr[0, rf, :] = xf_re
        s_scr[1, rf, :] = xf_im
        s_scr[2, rb, :] = xb_re
        s_scr[3, rb, :] = xb_im
        nf_re, nf_im = cmul(decay[0], decay[1], xf_re, xf_im)
        nb_re, nb_im = cmul(decay[2], decay[3], xb_re, xb_im)
        return nf_re + sf_re, nf_im + sf_im, nb_re + sb_re, nb_im + sb_im

    zero = jnp.zeros((chains, w), F32)
    lax.fori_loop(0, n_blocks, step, (zero, zero, zero, zero), unroll=2)

    for q in range(pairs):
        entering = jnp.concatenate(
            [jnp.concatenate([s_scr[part, pl.ds(q * bsz + b, n_blocks, stride=chains), :] for b in range(bsz)], axis=0)
             for part in range(4)], axis=1)
        carried_t = _dot_nt(woutt_scr[q], entering.astype(BF16))
        for g in range(2):
            y_t = dot_tn(m_scr[2 * q + g], ut_scr[2 * q + g]) + carried_t[g * blk:(g + 1) * blk]
            for i in range(sb):
                yt_scr[i, (2 * q + g) * ch:(2 * q + g + 1) * ch, :] = y_t[i * ch:(i + 1) * ch, :]

    for i in range(sb):
        y_ref[pl.ds(i, rows, stride=sb), :] = yt_scr[i].T


def _s5(proj, u_block0, b_t, c_t, powers, layer, bsz):
    t = proj.shape[0]
    rows = t // S5_BLOCK
    blk = S5_BLOCK * S5_GROUP_CH
    groups = LANE // S5_GROUP_CH
    pairs = groups // 2
    w = 2 * S5_STATE
    n_tiles = powers.shape[3] // pairs
    placed = pl.BlockSpec((None, 2, 2, pairs, 2, S5_GROUP_CH, w), lambda k: (layer, 0, 0, k, 0, 0, 0))
    return pl.pallas_call(
        functools.partial(_s5_kernel, bsz=bsz),
        grid=(n_tiles,),
        in_specs=[
            pl.BlockSpec((t, LANE), lambda k: (0, u_block0 + k)),
            placed, placed,
            pl.BlockSpec((None, 2, 2, pairs, S5_BLOCK + 1, w), lambda k: (layer, 0, 0, k, 0, 0)),
        ],
        out_specs=pl.BlockSpec((t, LANE), lambda k: (0, k)),
        out_shape=jax.ShapeDtypeStruct((t, n_tiles * LANE), F32),
        scratch_shapes=[
            pltpu.VMEM((groups, blk, rows), BF16),
            pltpu.VMEM((4, pairs * rows, w), F32),
            pltpu.VMEM((S5_BLOCK, LANE, rows), F32),
            pltpu.VMEM((groups, blk, blk), BF16),
            pltpu.VMEM((pairs, 2 * blk, 4 * w), BF16),
            pltpu.VMEM((pairs, 2 * blk, 4 * w), BF16),
        ],
        compiler_params=_compiler_params(("parallel",)),
        name="s5",
    )(proj, b_t, c_t, powers)


def _s5_params(lam_re, lam_im, log_dt, b_re, b_im, c_re, c_im):
    n_groups = lam_re.shape[1]
    sb, p, c = S5_BLOCK, S5_STATE, S5_GROUP_CH
    l_re, l_im = lam_re.astype(F32), lam_im.astype(F32)
    dt = jnp.exp(log_dt.astype(F32))[..., None]
    mag = jnp.exp(l_re * dt)
    bar_re, bar_im = mag * jnp.cos(l_im * dt), mag * jnp.sin(l_im * dt)
    n_re, n_im = bar_re - 1.0, bar_im
    den = l_re * l_re + l_im * l_im
    k_re = ((n_re * l_re + n_im * l_im) / den)[..., None]
    k_im = ((n_im * l_re - n_re * l_im) / den)[..., None]
    bb_re = k_re * b_re.astype(F32) - k_im * b_im.astype(F32)
    bb_im = k_re * b_im.astype(F32) + k_im * b_re.astype(F32)
    cc_re, cc_im = c_re.astype(F32), c_im.astype(F32)

    steps = jnp.arange(sb + 1, dtype=F32)[:, None, None, None]
    mag_l = jnp.exp(steps * (l_re * dt)[None])
    pw_re, pw_im = mag_l * jnp.cos(steps * (l_im * dt)[None]), mag_l * jnp.sin(steps * (l_im * dt)[None])

    same = jnp.eye(2, dtype=F32)
    def place(t):
        t2 = t.reshape(2, n_groups // 2, 2, c, 1, p) * same[None, None, :, None, :, None]
        return t2.reshape(2, n_groups // 2, 2, c, 2 * p)
    b_t = jnp.stack([place(jnp.swapaxes(bb_re, 2, 3)), place(jnp.swapaxes(bb_im, 2, 3))], axis=1)
    c_t = jnp.stack([place(cc_re), place(cc_im)], axis=1)
    def side_by_side(t):
        t2 = jnp.transpose(t.reshape(sb + 1, 2, n_groups // 2, 2, p), (1, 2, 0, 3, 4))
        return t2.reshape(2, n_groups // 2, sb + 1, 2 * p)
    powers = jnp.stack([side_by_side(pw_re), side_by_side(pw_im)], axis=1)
    return b_t, c_t, powers


def _out_kernel(oa_ref, za_ref, ys_ref, u_ref, zb_ref, ga_ref, gb_ref, x_ref,
                wpa_ref, dsk_ref, wglu_ref, bglu_ref, wpb_ref, bga_ref, bgb_ref, wout_ref, fg_ref,
                o_ref, *, final_norm):
    a = (oa_ref[...] * _silu(za_ref[...])).astype(BF16)
    y_a = jnp.dot(a, wpa_ref[...], preferred_element_type=F32)
    y_s = jax.nn.gelu(ys_ref[...] + u_ref[...] * dsk_ref[...])
    glu = jnp.dot(y_s.astype(BF16), wglu_ref[...], preferred_element_type=F32) + bglu_ref[...]
    y_s = y_s * jax.nn.sigmoid(glu)
    y_b = jnp.dot((y_s * _silu(zb_ref[...])).astype(BF16), wpb_ref[...], preferred_element_type=F32)
    merged = (jax.nn.sigmoid(ga_ref[...] + bga_ref[...]) * y_a
              + jax.nn.sigmoid(gb_ref[...] + bgb_ref[...]) * y_b)
    out = x_ref[...] + jnp.dot(merged.astype(BF16), wout_ref[...], preferred_element_type=F32)
    if final_norm:
        ms = jnp.mean(out * out, axis=-1, keepdims=True)
        out = out * lax.rsqrt(ms + RMS_EPS) * fg_ref[...]
    o_ref[...] = out


def _out_stage(o_a, proj, y_s5, x2, w_pa, d_skip, w_glu, b_glu, w_pb, b_gate, w_out, final_g, *,
               layer, za_block, u_block, zb_block, gate_block0, final_norm):
    t, d = x2.shape
    wa = o_a.shape[1]
    wb = y_s5.shape[1]
    tm = _largest_tile(t, 256, SUBLANE)
    row = lambda width, blk: pl.BlockSpec((tm, width), lambda i: (i, blk))
    const = lambda shape, blk=0: pl.BlockSpec((None,) + shape, lambda i: (layer, 0, blk),
                                              pipeline_mode=pl.Buffered(1))
    return pl.pallas_call(
        functools.partial(_out_kernel, final_norm=final_norm),
        grid=(t // tm,),
        in_specs=[
            row(wa, 0), row(wa, za_block), row(wb, 0), row(wb, u_block), row(wb, zb_block),
            row(d, gate_block0), row(d, gate_block0 + 1), row(d, 0),
            const((wa, d)), const((1, wb)), const((wb, wb)), const((1, wb)), const((wb, d)),
            const((1, d), 0), const((1, d), 1), const((d, d)), const((1, d)),
        ],
        out_specs=pl.BlockSpec((tm, d), lambda i: (i, 0)),
        out_shape=jax.ShapeDtypeStruct((t, d), F32),
        compiler_params=_compiler_params(("parallel",)),
        name="out_stage",
    )(o_a, proj, y_s5, proj, proj, proj, proj, x2, w_pa, d_skip, w_glu, b_glu, w_pb, b_gate, b_gate, w_out,
      final_g)


def _prepare_params(ln_g, w_in, conv_w, a_log, dt_bias, head_norm_g, lam_re, lam_im, log_dt, b_re, b_im,
                    c_re, c_im, d_skip, w_glu, b_glu, w_pa, w_pb, b_gate, w_out, final_g):
    depth, d, _ = w_in.shape
    n_heads = a_log.shape[2]
    wa = n_heads * HEAD_DIM
    wb = d_skip.shape[1]
    n_ba = 4 * n_heads
    o_beta = 4 * wa
    w_in16 = w_in.astype(BF16)
    w_cols = _regroup_weights(w_in16, o_beta, n_ba)
    w_ba = w_in16[:, :, o_beta:o_beta + LANE]
    pad_rows = lambda p: jnp.pad(p.astype(F32).reshape(depth, 1, n_ba // 2),
                                 ((0, 0), (0, 0), (n_ba // 2, LANE - n_ba)))
    row = lambda p: p.astype(F32).reshape(depth, 1, -1)
    return dict(
        ln_g=row(ln_g), w_cols=w_cols, w_ba=w_ba, conv_w=conv_w.astype(F32), alog=pad_rows(a_log), dtb=pad_rows(dt_bias),
        head_norm_g=row(head_norm_g),
        s5=jax.vmap(_s5_params)(lam_re, lam_im, log_dt, b_re, b_im, c_re, c_im),
        d_skip=row(d_skip), w_glu=w_glu.astype(BF16), b_glu=row(b_glu), w_pa=w_pa.astype(BF16),
        w_pb=w_pb.astype(BF16), b_gate=row(b_gate), w_out=w_out.astype(BF16),
        final_g=jnp.broadcast_to(final_g.astype(F32).reshape(1, 1, d), (depth, 1, d)),
        n_heads=n_heads, wa=wa, wb=wb)


def _layer(x2, bsz, seq, prm, layer, final_norm):
    t, d = x2.shape
    n_heads, wa, wb = prm["n_heads"], prm["wa"], prm["wb"]
    assert wa == wb and seq % (CHUNK * SUBLANE) == 0 and 4 * n_heads <= LANE
    c_qkv, c_za, c_beta, c_alpha, c_u, c_zb = 3 * wa, wa, 2 * n_heads, 2 * n_heads, wb, wb
    n_proj = prm["w_cols"].shape[2]

    proj, ba_logits = _inproj(x2, prm["ln_g"], prm["w_cols"], prm["w_ba"], layer)
    proj3 = proj.reshape(bsz, seq, n_proj)

    gates = _gates(ba_logits, prm["alog"], prm["dtb"], layer, n_heads)
    beta = gates[:, :c_beta].reshape(bsz, seq, 2, n_heads)
    gc = gates[:, c_beta:c_beta + c_alpha].reshape(bsz, seq, 2, n_heads)
    hpb = DELTA_HEADS_PER_PROGRAM
    cols = jnp.stack([gc[:, :, 0], gc[:, :, 1], beta[:, :, 0], beta[:, :, 1]], axis=-1)
    cols = jnp.transpose(cols.reshape(bsz, seq, n_heads // hpb, 4 * hpb), (0, 2, 1, 3))
    rows = jnp.transpose(gc.reshape(bsz, seq // CHUNK, CHUNK, 2, n_heads), (0, 4, 1, 3, 2))
    rows = rows.reshape(bsz, n_heads, seq // CHUNK, 1, 2 * CHUNK)

    o_a = _delta(proj3, prm["conv_w"], cols, rows, prm["head_norm_g"], layer, n_heads)
    o_a = o_a.reshape(t, wa)

    u_off = c_qkv + c_za
    y_s5 = _s5(proj, u_off // LANE, *prm["s5"], layer, bsz)

    return _out_stage(
        o_a, proj, y_s5, x2, prm["w_pa"], prm["d_skip"], prm["w_glu"], prm["b_glu"], prm["w_pb"],
        prm["b_gate"], prm["w_out"], prm["final_g"], layer=layer,
        za_block=c_qkv // wa, u_block=u_off // wb, zb_block=(u_off + c_u) // wb,
        gate_block0=(u_off + c_u + c_zb) // d, final_norm=final_norm)


def kernel(x, ln_g, w_in, conv_w, a_log, dt_bias, head_norm_g, lam_re, lam_im, log_dt, b_re, b_im, c_re, c_im, d_skip, w_glu, b_glu, w_pa, w_pb, b_gate, w_out, final_g):
    bsz, seq, d = x.shape
    depth = ln_g.shape[0]
    prm = _prepare_params(ln_g, w_in, conv_w, a_log, dt_bias, head_norm_g, lam_re, lam_im, log_dt, b_re, b_im,
                          c_re, c_im, d_skip, w_glu, b_glu, w_pa, w_pb, b_gate, w_out, final_g)
    x2 = x.reshape(bsz * seq, d)
    for layer in range(depth):
        x2 = _layer(x2, bsz, seq, prm, layer, final_norm=(layer == depth - 1))
    return x2.reshape(bsz, seq, d)
```
